```python
import math
import jax
import jax.numpy as jnp
from jax import lax
import numpy as np

D_MODEL = 4096
BATCH = 2
SEQ = 4096
DEPTH = 2
DEC_BATCH = 32
DEC_SEQ = 16
PAST_LEN = 2048

CHUNK = 64
N_MIXERS = 2
N_SSM_LAYERS = (DEPTH + 1) // 2
N_GLA_LAYERS = DEPTH // 2
EPS = 1e-6

SSM_GROUP_CH = 16
SSM_GROUPS = D_MODEL // SSM_GROUP_CH
SSM_STATE = 64
DT_MIN = 1e-3
DT_MAX = 1e-1

GLA_HEADS = 4
GLA_DK = D_MODEL // 2
GLA_DV = D_MODEL
GLA_DK_HEAD = GLA_DK // GLA_HEADS
GLA_DV_HEAD = GLA_DV // GLA_HEADS
GLA_GATE_RANK = 16
GLA_TAU = 16.0
GLA_IN = 2 * GLA_DK + 2 * GLA_DV + GLA_GATE_RANK

N_MEM = 256
CA_HEADS = 4
CA_HEAD_DIM = D_MODEL // CA_HEADS

MOE_GROUPS = 4
MOE_EXPERTS_PER_GROUP = 8
MOE_EXPERTS = MOE_GROUPS * MOE_EXPERTS_PER_GROUP
MOE_TOP_K = 2
MOE_HIDDEN = D_MODEL // 4
MOE_BLOCK = 128

kernel_name = 'hybrid_s5_gla_hmoe_stream_step'


def rmsnorm(x, g):
    xf = x.astype(jnp.float32)
    xf = xf * lax.rsqrt(jnp.mean(xf * xf, axis=-1, keepdims=True) + EPS)
    return xf.astype(x.dtype) * g


def _s5_discretize(lam_re, lam_im, log_dt, b_re, b_im):
    dt = jnp.exp(log_dt.astype(jnp.float32))[:, None]
    lr = lam_re.astype(jnp.float32)
    li = lam_im.astype(jnp.float32)
    mag = jnp.exp(lr * dt)
    ab_re = mag * jnp.cos(li * dt)
    ab_im = mag * jnp.sin(li * dt)
    nr = ab_re - 1.0
    ni = ab_im
    den = lr * lr + li * li
    f_re = (nr * lr + ni * li) / den
    f_im = (ni * lr - nr * li) / den
    br = b_re.astype(jnp.float32)
    bi = b_im.astype(jnp.float32)
    bb_re = f_re[..., None] * br - f_im[..., None] * bi
    bb_im = f_re[..., None] * bi + f_im[..., None] * br
    return ab_re, ab_im, bb_re, bb_im


def _complex_affine_combine(e1, e2):
    a1r, a1i, b1r, b1i = e1
    a2r, a2i, b2r, b2i = e2
    return (a2r * a1r - a2i * a1i,
            a2r * a1i + a2i * a1r,
            a2r * b1r - a2i * b1i + b2r,
            a2r * b1i + a2i * b1r + b2i)


def s5_mixer(u, h0_re, h0_im, lam_re, lam_im, log_dt, b_re, b_im, c_re, c_im, d_skip, w_glu):
    bsz, L, _ = u.shape
    lc = min(CHUNK, L)
    nc = L // lc
    ab_re, ab_im, bb_re, bb_im = _s5_discretize(lam_re, lam_im, log_dt, b_re, b_im)
    cr = c_re.astype(jnp.float32)
    ci = c_im.astype(jnp.float32)
    uf = u.astype(jnp.float32)
    u_blocks = uf.reshape(bsz, nc, lc, SSM_GROUPS, SSM_GROUP_CH).transpose(1, 0, 2, 3, 4)
    a_re = jnp.broadcast_to(ab_re, (bsz, lc, SSM_GROUPS, SSM_STATE))
    a_im = jnp.broadcast_to(ab_im, (bsz, lc, SSM_GROUPS, SSM_STATE))

    def step(carry, u_c):
        h_re, h_im = carry
        bu_re = jnp.einsum('blgc,gpc->blgp', u_c, bb_re)
        bu_im = jnp.einsum('blgc,gpc->blgp', u_c, bb_im)
        bu_re = bu_re.at[:, 0].add(ab_re * h_re - ab_im * h_im)
        bu_im = bu_im.at[:, 0].add(ab_re * h_im + ab_im * h_re)
        _, _, hs_re, hs_im = lax.associative_scan(
            _complex_affine_combine, (a_re, a_im, bu_re, bu_im), axis=1)
        y_c = (jnp.einsum('blgp,gcp->blgc', hs_re, cr)
               - jnp.einsum('blgp,gcp->blgc', hs_im, ci))
        return (hs_re[:, -1], hs_im[:, -1]), y_c

    (h_re_T, h_im_T), ys = lax.scan(
        step, (h0_re.astype(jnp.float32), h0_im.astype(jnp.float32)), u_blocks)
    y = ys.transpose(1, 0, 2, 3, 4).reshape(bsz, L, D_MODEL) + d_skip.astype(jnp.float32) * uf
    z = jax.nn.gelu(y).astype(u.dtype)
    ag = z @ w_glu
    out = ag[..., :D_MODEL] * jax.nn.sigmoid(ag[..., D_MODEL:])
    return out, h_re_T, h_im_T


def gla_mixer(xn, s0, w_in, w_gate_up, b_gate, norm_g, w_out):
    bsz, L, _ = xn.shape
    lc = min(CHUNK, L)
    nc = L // lc
    proj = xn @ w_in
    q = proj[..., :GLA_DK]
    k = proj[..., GLA_DK:2 * GLA_DK]
    v = proj[..., 2 * GLA_DK:2 * GLA_DK + GLA_DV]
    r = proj[..., 2 * GLA_DK + GLA_DV:2 * GLA_DK + 2 * GLA_DV]
    glr = proj[..., 2 * GLA_DK + 2 * GLA_DV:]
    gate_logits = (glr @ w_gate_up + b_gate).astype(jnp.float32)
    log_a = jax.nn.log_sigmoid(gate_logits) / GLA_TAU

    def heads(t, dh):
        return t.astype(jnp.float32).reshape(bsz, nc, lc, GLA_HEADS, dh).transpose(1, 0, 2, 3, 4)

    qs = heads(q, GLA_DK_HEAD) * (GLA_DK_HEAD ** -0.5)
    ks = heads(k, GLA_DK_HEAD)
    vs = heads(v, GLA_DV_HEAD)
    las = heads(log_a, GLA_DK_HEAD)

    def step(s, xs):
        q_c, k_c, v_c, la_c = xs
        cum = jnp.cumsum(la_c, axis=1)
        end = cum[:, -1]
        k_dec = k_c * jnp.exp(end[:, None] - cum)
        s_new = jnp.exp(end)[..., None] * s + jnp.einsum('blhk,blhv->bhkv', k_dec, v_c)
        o_c = jnp.einsum('blhk,bhkv->blhv', q_c, s_new)
        return s_new, o_c

    s_T, os_ = lax.scan(step, s0.astype(jnp.float32), (qs, ks, vs, las))
    o = os_.transpose(1, 0, 2, 3, 4).reshape(bsz, L, GLA_HEADS, GLA_DV_HEAD)
    o = o * lax.rsqrt(jnp.mean(o * o, axis=-1, keepdims=True) + EPS)
    o = o.reshape(bsz, L, GLA_DV).astype(xn.dtype) * norm_g
    o = o * jax.nn.silu(r)
    return o @ w_out, s_T


def mem_kv(mem, g, w_kv):
    bsz = mem.shape[0]
    kv = rmsnorm(mem, g) @ w_kv
    mk = kv[..., :D_MODEL].reshape(bsz, N_MEM, CA_HEADS, CA_HEAD_DIM)
    mv = kv[..., D_MODEL:].reshape(bsz, N_MEM, CA_HEADS, CA_HEAD_DIM)
    return mk, mv


def cross_attend(xn, mk, mv, w_q, w_o):
    bsz, L, _ = xn.shape
    q = (xn @ w_q).reshape(bsz, L, CA_HEADS, CA_HEAD_DIM)
    s = jnp.einsum('blhd,bmhd->bhlm', q, mk).astype(jnp.float32) * (CA_HEAD_DIM ** -0.5)
    p = jax.nn.softmax(s, axis=-1).astype(xn.dtype)
    o = jnp.einsum('bhlm,bmhd->blhd', p, mv).reshape(bsz, L, D_MODEL)
    return o @ w_o


def _grouped_expert_mlp(xf, expert_ids, gates, w_gate_up, w_down):
    t, d = xf.shape
    n = t * MOE_TOP_K
    flat_e = expert_ids.reshape(n).astype(jnp.int32)
    flat_tok = jnp.repeat(jnp.arange(t, dtype=jnp.int32), MOE_TOP_K)
    flat_g = gates.reshape(n)
    order = jnp.argsort(flat_e)
    se = flat_e[order]
    stok = flat_tok[order]
    sg = flat_g[order]
    counts = jnp.zeros((MOE_EXPERTS,), jnp.int32).at[flat_e].add(1)
    starts = jnp.cumsum(counts) - counts
    padded = (counts + MOE_BLOCK - 1) // MOE_BLOCK * MOE_BLOCK
    pends = jnp.cumsum(padded)
    pstarts = pends - padded
    dest = pstarts[se] + (jnp.arange(n, dtype=jnp.int32) - starts[se])
    n_blocks = -(-(n + MOE_EXPERTS * (MOE_BLOCK - 1)) // MOE_BLOCK)
    rows = jnp.zeros((n_blocks * MOE_BLOCK, d), xf.dtype).at[dest].set(xf[stok])
    block_starts = jnp.arange(n_blocks, dtype=jnp.int32) * MOE_BLOCK
    block_e = jnp.minimum(jnp.searchsorted(pends, block_starts, side='right'), MOE_EXPERTS - 1)

    def expert_block(args):
        xb, e = args
        gu = xb @ w_gate_up[e]
        return (jax.nn.silu(gu[:, :MOE_HIDDEN]) * gu[:, MOE_HIDDEN:]) @ w_down[e]

    out = lax.map(expert_block, (rows.reshape(n_blocks, MOE_BLOCK, d), block_e))
    y_sorted = out.reshape(n_blocks * MOE_BLOCK, d)[dest] * sg[:, None].astype(xf.dtype)
    return jax.ops.segment_sum(y_sorted, stok, num_segments=t)


def hier_moe(xn, w_group, b_group, w_expert, b_expert, w_gate_up, w_down):
    bsz, L, d = xn.shape
    t = bsz * L
    xf = xn.reshape(t, d)
    g_logits = (xf @ w_group + b_group).astype(jnp.float32)
    g_prob = jax.nn.softmax(g_logits, axis=-1)
    _, g_idx = lax.top_k(g_logits, 1)
    g_w = jnp.take_along_axis(g_prob, g_idx, axis=1)
    e_logits = (xf @ w_expert + b_expert).astype(jnp.float32).reshape(t, MOE_GROUPS, MOE_EXPERTS_PER_GROUP)
    e_logits = jnp.take_along_axis(e_logits, g_idx[:, :, None], axis=1)[:, 0]
    e_prob = jax.nn.softmax(e_logits, axis=-1)
    top_p, top_i = lax.top_k(e_prob, MOE_TOP_K)
    gates = g_w * top_p / jnp.sum(top_p, axis=-1, keepdims=True)
    expert_ids = g_idx * MOE_EXPERTS_PER_GROUP + top_i
    y = _grouped_expert_mlp(xf, expert_ids, gates, w_gate_up, w_down)
    return y.reshape(bsz, L, d)


def _trunk(x, mem_k, mem_v, h_re, h_im, gla_s, params):
    (norm_mixer, norm_ca, norm_moe, norm_final,
     ssm_lambda_re, ssm_lambda_im, ssm_log_dt, ssm_b_re, ssm_b_im, ssm_c_re, ssm_c_im, ssm_d, ssm_w_glu,
     gla_w_in, gla_w_gate_up, gla_b_gate, gla_norm, gla_w_out,
     ca_w_q, ca_w_o,
     moe_w_group, moe_b_group, moe_w_expert, moe_b_expert, moe_w_gate_up, moe_w_down) = params
    new_re, new_im, new_gla = [], [], []
    for i in range(DEPTH):
        j = i // N_MIXERS
        xn = rmsnorm(x, norm_mixer[i])
        if i % N_MIXERS == 0:
            m, hr, hi = s5_mixer(xn, h_re[j], h_im[j], ssm_lambda_re[j], ssm_lambda_im[j], ssm_log_dt[j],
                                 ssm_b_re[j], ssm_b_im[j], ssm_c_re[j], ssm_c_im[j], ssm_d[j], ssm_w_glu[j])
            new_re.append(hr)
            new_im.append(hi)
        else:
            m, s = gla_mixer(xn, gla_s[j], gla_w_in[j], gla_w_gate_up[j], gla_b_gate[j], gla_norm[j], gla_w_out[j])
            new_gla.append(s)
        x = x + m
        x = x + cross_attend(rmsnorm(x, norm_ca[i]), mem_k[i], mem_v[i], ca_w_q[i], ca_w_o[i])
        x = x + hier_moe(rmsnorm(x, norm_moe[i]), moe_w_group[i], moe_b_group[i], moe_w_expert[i],
                         moe_b_expert[i], moe_w_gate_up[i], moe_w_down[i])
    y = rmsnorm(x, norm_final)
    return y, jnp.stack(new_re), jnp.stack(new_im), jnp.stack(new_gla)


def setup_inputs(seed: int = 0) -> dict:
    key = jax.random.key(seed)
    ks = jax.random.split(key, 48)
    f32 = jnp.float32

    def nrm(i, shape, scale):
        return jax.random.normal(ks[i], shape, f32) * scale

    def gain(i, shape):
        return 1.0 + nrm(i, shape, 0.02)

    lam_im_base = np.pi * jnp.arange(SSM_STATE, dtype=f32)
    return {
        'x_prompt': nrm(0, (BATCH, SEQ, D_MODEL), 1.0),
        'x_sample': nrm(1, (DEC_BATCH, DEC_SEQ, D_MODEL), 1.0),
        'mem_prompt': nrm(2, (BATCH, N_MEM, D_MODEL), 1.0),
        'state_ssm_re': nrm(3, (N_SSM_LAYERS, DEC_BATCH, SSM_GROUPS, SSM_STATE), 0.1),
        'state_ssm_im': nrm(4, (N_SSM_LAYERS, DEC_BATCH, SSM_GROUPS, SSM_STATE), 0.1),
        'state_gla': nrm(5, (N_GLA_LAYERS, DEC_BATCH, GLA_HEADS, GLA_DK_HEAD, GLA_DV_HEAD), 1.0),
        'cache_mem_k': nrm(6, (DEPTH, DEC_BATCH, N_MEM, CA_HEADS, CA_HEAD_DIM), 1.0),
        'cache_mem_v': nrm(7, (DEPTH, DEC_BATCH, N_MEM, CA_HEADS, CA_HEAD_DIM), 1.0),
        'norm_mixer': gain(8, (DEPTH, D_MODEL)),
        'norm_ca': gain(9, (DEPTH, D_MODEL)),
        'norm_moe': gain(10, (DEPTH, D_MODEL)),
        'norm_final': gain(11, (D_MODEL,)),
        'ssm_lambda_re': -0.5 + nrm(12, (N_SSM_LAYERS, SSM_GROUPS, SSM_STATE), 0.01),
        'ssm_lambda_im': lam_im_base + nrm(13, (N_SSM_LAYERS, SSM_GROUPS, SSM_STATE), 0.01),
        'ssm_log_dt': jax.random.uniform(ks[14], (N_SSM_LAYERS, SSM_GROUPS), f32,
                                         minval=math.log(DT_MIN), maxval=math.log(DT_MAX)),
        'ssm_b_re': nrm(15, (N_SSM_LAYERS, SSM_GROUPS, SSM_STATE, SSM_GROUP_CH), (2 * SSM_GROUP_CH) ** -0.5),
        'ssm_b_im': nrm(16, (N_SSM_LAYERS, SSM_GROUPS, SSM_STATE, SSM_GROUP_CH), (2 * SSM_GROUP_CH) ** -0.5),
        'ssm_c_re': nrm(17, (N_SSM_LAYERS, SSM_GROUPS, SSM_GROUP_CH, SSM_STATE), 0.5),
        'ssm_c_im': nrm(18, (N_SSM_LAYERS, SSM_GROUPS, SSM_GROUP_CH, SSM_STATE), 0.5),
        'ssm_d': nrm(19, (N_SSM_LAYERS, D_MODEL), 1.0),
        'ssm_w_glu': nrm(20, (N_SSM_LAYERS, D_MODEL, 2 * D_MODEL), D_MODEL ** -0.5),
        'gla_w_in': nrm(21, (N_GLA_LAYERS, D_MODEL, GLA_IN), D_MODEL ** -0.5),
        'gla_w_gate_up': nrm(22, (N_GLA_LAYERS, GLA_GATE_RANK, GLA_DK), GLA_GATE_RANK ** -0.5),
        'gla_b_gate': nrm(23, (N_GLA_LAYERS, GLA_DK), 0.5),
        'gla_norm': gain(24, (N_GLA_LAYERS, GLA_DV)),
        'gla_w_out': nrm(25, (N_GLA_LAYERS, GLA_DV, D_MODEL), GLA_DV ** -0.5),
        'ca_mem_norm': gain(26, (DEPTH, D_MODEL)),
        'ca_w_q': nrm(27, (DEPTH, D_MODEL, D_MODEL), D_MODEL ** -0.5),
        'ca_w_kv': nrm(28, (DEPTH, D_MODEL, 2 * D_MODEL), D_MODEL ** -0.5),
        'ca_w_o': nrm(29, (DEPTH, D_MODEL, D_MODEL), D_MODEL ** -0.5),
        'moe_w_group': nrm(30, (DEPTH, D_MODEL, MOE_GROUPS), D_MODEL ** -0.5),
        'moe_b_group': nrm(31, (DEPTH, MOE_GROUPS), 0.01),
        'moe_w_expert': nrm(32, (DEPTH, D_MODEL, MOE_EXPERTS), D_MODEL ** -0.5),
        'moe_b_expert': nrm(33, (DEPTH, MOE_EXPERTS), 0.01),
        'moe_w_gate_up': nrm(34, (DEPTH, MOE_EXPERTS, D_MODEL, 2 * MOE_HIDDEN), D_MODEL ** -0.5),
        'moe_w_down': nrm(35, (DEPTH, MOE_EXPERTS, MOE_HIDDEN, D_MODEL), MOE_HIDDEN ** -0.5),
    }


def reference(x_prompt, x_sample, mem_prompt, state_ssm_re, state_ssm_im, state_gla, cache_mem_k, cache_mem_v,
              norm_mixer, norm_ca, norm_moe, norm_final,
              ssm_lambda_re, ssm_lambda_im, ssm_log_dt, ssm_b_re, ssm_b_im, ssm_c_re, ssm_c_im, ssm_d, ssm_w_glu,
              gla_w_in, gla_w_gate_up, gla_b_gate, gla_norm, gla_w_out,
              ca_mem_norm, ca_w_q, ca_w_kv, ca_w_o,
              moe_w_group, moe_b_group, moe_w_expert, moe_b_expert, moe_w_gate_up, moe_w_down):
    params = (norm_mixer, norm_ca, norm_moe, norm_final,
              ssm_lambda_re, ssm_lambda_im, ssm_log_dt, ssm_b_re, ssm_b_im, ssm_c_re, ssm_c_im, ssm_d, ssm_w_glu,
              gla_w_in, gla_w_gate_up, gla_b_gate, gla_norm, gla_w_out,
              ca_w_q, ca_w_o,
              moe_w_group, moe_b_group, moe_w_expert, moe_b_expert, moe_w_gate_up, moe_w_down)

    mem_pairs = [mem_kv(mem_prompt, ca_mem_norm[i], ca_w_kv[i]) for i in range(DEPTH)]
    prompt_mem_k = jnp.stack([pair[0] for pair in mem_pairs])
    prompt_mem_v = jnp.stack([pair[1] for pair in mem_pairs])
    zero_re = jnp.zeros((N_SSM_LAYERS, BATCH, SSM_GROUPS, SSM_STATE), jnp.float32)
    zero_gla = jnp.zeros((N_GLA_LAYERS, BATCH, GLA_HEADS, GLA_DK_HEAD, GLA_DV_HEAD), jnp.float32)
    y_prompt, prompt_ssm_re, prompt_ssm_im, prompt_gla = _trunk(
        x_prompt, prompt_mem_k, prompt_mem_v, zero_re, zero_re, zero_gla, params)

    y_sample, sample_ssm_re, sample_ssm_im, sample_gla = _trunk(
        x_sample, cache_mem_k, cache_mem_v, state_ssm_re, state_ssm_im, state_gla, params)

    return (y_prompt, y_sample, prompt_ssm_re, prompt_ssm_im, prompt_gla, prompt_mem_k, prompt_mem_v,
            sample_ssm_re, sample_ssm_im, sample_gla)
```

```python
import functools
import math

import jax
import jax.numpy as jnp
import numpy as np
from jax import lax
from jax.experimental import pallas as pl
from jax.experimental.pallas import tpu as pltpu

F32 = jnp.float32
BF16 = jnp.bfloat16

EPS = 1e-6
D_MODEL = 4096
DEPTH = 2

SSM_GROUP_CH = 16
SSM_GROUPS = D_MODEL // SSM_GROUP_CH
SSM_STATE = 64
SSM_SEQS = 32
SSM_BLK_CH = 128
SSM_BLK_GROUPS = SSM_BLK_CH // SSM_GROUP_CH
SSM_NBLK = D_MODEL // SSM_BLK_CH
SSM_BLK_STATE = SSM_BLK_GROUPS * SSM_STATE

GLA_HEADS = 4
GLA_DK = D_MODEL // 2
GLA_DV = D_MODEL
GLA_DK_HEAD = GLA_DK // GLA_HEADS
GLA_DV_HEAD = GLA_DV // GLA_HEADS
GLA_GATE_RANK = 16
GLA_TAU = 16.0
GLA_MAIN = 2 * GLA_DK + 2 * GLA_DV
GLA_CHUNK = 64

N_MEM = 256
CA_HEADS = 4
CA_HEAD_DIM = D_MODEL // CA_HEADS

MOE_GROUPS = 4
MOE_EPG = 8
MOE_EXPERTS = MOE_GROUPS * MOE_EPG
MOE_TOP_K = 2
MOE_HIDDEN = D_MODEL // 4
MOE_TILE = 256
ROUTER_LANES = 128

SUBLANES = 8
VMEM_LIMIT = 56 * 1024 * 1024


def _cparams(sem):
    return pltpu.CompilerParams(dimension_semantics=sem, vmem_limit_bytes=VMEM_LIMIT)


def _rms_body(n_add, emit_sum, x_ref, *refs):
    adds = refs[:n_add]
    g_ref = refs[n_add]
    outs = refs[n_add + 1:]
    x = x_ref[...].astype(F32)
    for a in adds:
        x = x + a[...].astype(F32)
    ms = jnp.mean(x * x, axis=-1, keepdims=True)
    xn = (x * lax.rsqrt(ms + EPS)) * g_ref[...]
    if emit_sum:
        outs[0][...] = x
        outs[1][...] = xn.astype(outs[1].dtype)
    else:
        outs[0][...] = xn.astype(outs[0].dtype)


def rmsnorm(x, g, adds=(), out_dtype=BF16, emit_sum=False, tm=256):
    m, d = x.shape
    assert m % tm == 0
    row = pl.BlockSpec((tm, d), lambda i: (i, 0))
    out_shape = [jax.ShapeDtypeStruct((m, d), out_dtype)]
    out_specs = [row]
    if emit_sum:
        out_shape.insert(0, jax.ShapeDtypeStruct((m, d), F32))
        out_specs.insert(0, row)
    res = pl.pallas_call(
        functools.partial(_rms_body, len(adds), emit_sum),
        out_shape=out_shape,
        grid=(m // tm,),
        in_specs=[row] * (1 + len(adds)) + [pl.BlockSpec((1, d), lambda i: (0, 0))],
        out_specs=out_specs,
        compiler_params=_cparams(("arbitrary",)),
        name="rmsnorm",
    )(x, *adds, g.reshape(1, d).astype(F32))
    return res if emit_sum else res[0]


def _mm_body(mode, x_ref, *refs):
    if mode == "glu_res":
        w_ref, w2_ref, res_ref, o_ref, wb_ref, wb2_ref = refs
    elif mode == "res":
        w_ref, res_ref, o_ref, wb_ref = refs
    else:
        w_ref, o_ref, wb_ref = refs

    @pl.when(pl.program_id(1) == 0)
    def _():
        wb_ref[...] = w_ref[...].astype(BF16)
        if mode == "glu_res":
            wb2_ref[...] = w2_ref[...].astype(BF16)

    x = x_ref[...]
    acc = jnp.dot(x, wb_ref[...], preferred_element_type=F32)
    if mode == "glu_res":
        gate = jnp.dot(x, wb2_ref[...], preferred_element_type=F32)
        o_ref[...] = res_ref[...] + acc * jax.nn.sigmoid(gate)
    elif mode == "res":
        o_ref[...] = res_ref[...] + acc
    else:
        o_ref[...] = acc.astype(o_ref.dtype)


def matmul(x, w, mode="plain", res=None, out_dtype=F32, n_out=None, col0=0, tm=512, tn=512):
    m, k = x.shape
    n = n_out if n_out is not None else w.shape[1]
    if m % tm:
        tm = m
    assert m % tm == 0 and n % tn == 0 and col0 % tn == 0
    c0 = col0 // tn
    nb = n // tn
    in_specs = [pl.BlockSpec((tm, k), lambda j, i: (i, 0)),
                pl.BlockSpec((k, tn), lambda j, i: (0, c0 + j))]
    args = [x, w]
    scratch = [pltpu.VMEM((k, tn), BF16)]
    if mode == "glu_res":
        in_specs.append(pl.BlockSpec((k, tn), lambda j, i: (0, c0 + nb + j)))
        args.append(w)
        scratch.append(pltpu.VMEM((k, tn), BF16))
    if mode in ("res", "glu_res"):
        in_specs.append(pl.BlockSpec((tm, tn), lambda j, i: (i, j)))
        args.append(res)
        out_dtype = F32
    return pl.pallas_call(
        functools.partial(_mm_body, mode),
        out_shape=jax.ShapeDtypeStruct((m, n), out_dtype),
        grid=(nb, m // tm),
        in_specs=in_specs,
        out_specs=pl.BlockSpec((tm, tn), lambda j, i: (i, j)),
        scratch_shapes=scratch,
        compiler_params=_cparams(("arbitrary", "arbitrary")),
        name="matmul_" + mode,
    )(*args)


def _s5_body(lt, emit_y, u_ref, h0_ref, ar_ref, ai_ref, bd_ref, *refs):
    if emit_y:
        cbd_ref, d_ref, z_ref, ht_ref, h_ref, bu_ref, hs_ref = refs
    else:
        ht_ref, h_ref, bu_ref = refs
    tb = pl.program_id(2)
    ns = SSM_BLK_STATE

    @pl.when(tb == 0)
    def _():
        h_ref[...] = h0_ref[...]

    u = u_ref[...].reshape(lt * SUBLANES, SSM_BLK_CH)
    bu_ref[...] = jnp.dot(u.astype(BF16), bd_ref[...], preferred_element_type=F32)
    ar = jnp.broadcast_to(ar_ref[...], (SUBLANES, ns))
    ai = jnp.broadcast_to(ai_ref[...], (SUBLANES, ns))

    def step(t, carry):
        hr, hi = carry
        r0 = pl.multiple_of(t * SUBLANES, SUBLANES)
        nr = ar * hr - ai * hi + bu_ref[pl.ds(r0, SUBLANES), 0:ns]
        ni = ar * hi + ai * hr + bu_ref[pl.ds(r0, SUBLANES), ns:2 * ns]
        if emit_y:
            hs_ref[pl.ds(r0, SUBLANES), 0:ns] = nr
            hs_ref[pl.ds(r0, SUBLANES), ns:2 * ns] = ni
        return nr, ni

    hr, hi = lax.fori_loop(0, lt, step, (h_ref[:, 0:ns], h_ref[:, ns:2 * ns]), unroll=4)
    h_ref[:, 0:ns] = hr
    h_ref[:, ns:2 * ns] = hi
    ht_ref[:, 0:ns] = hr
    ht_ref[:, ns:2 * ns] = hi
    if emit_y:
        y = jnp.dot(hs_ref[...].astype(BF16), cbd_ref[...], preferred_element_type=F32)
        y = y + d_ref[...] * u
        z_ref[...] = jax.nn.gelu(y, approximate=True).reshape(lt, SUBLANES, SSM_BLK_CH)


def _s5_scan(ut, h0, ar, ai, bd, cbd, dsk, emit_y):
    ltot = ut.shape[0]
    lt = min(ltot, 64)
    assert ltot % lt == 0
    ns2 = 2 * SSM_BLK_STATE
    nsg = SSM_SEQS // SUBLANES
    in_specs = [
        pl.BlockSpec((lt, SUBLANES, SSM_BLK_CH), lambda c, s, t: (t, s, c)),
        pl.BlockSpec((None, SUBLANES, ns2), lambda c, s, t: (c, s, 0)),
        pl.BlockSpec((None, 1, SSM_BLK_STATE), lambda c, s, t: (c, 0, 0)),
        pl.BlockSpec((None, 1, SSM_BLK_STATE), lambda c, s, t: (c, 0, 0)),
        pl.BlockSpec((None, SSM_BLK_CH, ns2), lambda c, s, t: (c, 0, 0)),
    ]
    args = [ut, h0, ar, ai, bd]
    ht_shape = jax.ShapeDtypeStruct((SSM_NBLK, SSM_SEQS, ns2), F32)
    ht_spec = pl.BlockSpec((None, SUBLANES, ns2), lambda c, s, t: (c, s, 0))
    scratch = [pltpu.VMEM((SUBLANES, ns2), F32), pltpu.VMEM((lt * SUBLANES, ns2), F32)]
    if emit_y:
        in_specs += [pl.BlockSpec((None, ns2, SSM_BLK_CH), lambda c, s, t: (c, 0, 0)),
                     pl.BlockSpec((None, 1, SSM_BLK_CH), lambda c, s, t: (c, 0, 0))]
        args += [cbd, dsk]
        out_shape = [jax.ShapeDtypeStruct(ut.shape, F32), ht_shape]
        out_specs = [pl.BlockSpec((lt, SUBLANES, SSM_BLK_CH), lambda c, s, t: (t, s, c)), ht_spec]
        scratch.append(pltpu.VMEM((lt * SUBLANES, ns2), F32))
    else:
        out_shape = [ht_shape]
        out_specs = [ht_spec]
    res = pl.pallas_call(
        functools.partial(_s5_body, lt, emit_y),
        out_shape=out_shape,
        grid=(SSM_NBLK, nsg, ltot // lt),
        in_specs=in_specs,
        out_specs=out_specs,
        scratch_shapes=scratch,
        compiler_params=_cparams(("arbitrary", "arbitrary", "arbitrary")),
        name="s5_scan_y" if emit_y else "s5_scan_state",
    )(*args)
    return (res[0], res[1]) if emit_y else (None, res[0])


def _s5_carry_body(nb, nseg, e_ref, h0_ref, ar_ref, ai_ref, hs_ref, he_ref):
    ns = SSM_BLK_STATE
    ar = ar_ref[...]
    ai = ai_ref[...]
    for b in range(nb):
        hr = h0_ref[b:b + 1, 0:ns]
        hi = h0_ref[b:b + 1, ns:2 * ns]
        for j in range(nseg):
            s = b * nseg + j
            hs_ref[s:s + 1, 0:ns] = hr
            hs_ref[s:s + 1, ns:2 * ns] = hi
            er = e_ref[s:s + 1, 0:ns]
            ei = e_ref[s:s + 1, ns:2 * ns]
            hr, hi = ar * hr - ai * hi + er, ar * hi + ai * hr + ei
        he_ref[b:b + 1, 0:ns] = hr
        he_ref[b:b + 1, ns:2 * ns] = hi


def _s5_carry(e0, h0, ar_seg, ai_seg, nb, nseg):
    ns2 = 2 * SSM_BLK_STATE
    return pl.pallas_call(
        functools.partial(_s5_carry_body, nb, nseg),
        out_shape=[jax.ShapeDtypeStruct((SSM_NBLK, SSM_SEQS, ns2), F32),
                   jax.ShapeDtypeStruct((SSM_NBLK, nb, ns2), F32)],
        grid=(SSM_NBLK,),
        in_specs=[pl.BlockSpec((None, SSM_SEQS, ns2), lambda c: (c, 0, 0)),
                  pl.BlockSpec((None, nb, ns2), lambda c: (c, 0, 0)),
                  pl.BlockSpec((None, 1, SSM_BLK_STATE), lambda c: (c, 0, 0)),
                  pl.BlockSpec((None, 1, SSM_BLK_STATE), lambda c: (c, 0, 0))],
        out_specs=[pl.BlockSpec((None, SSM_SEQS, ns2), lambda c: (c, 0, 0)),
                   pl.BlockSpec((None, nb, ns2), lambda c: (c, 0, 0))],
        compiler_params=_cparams(("arbitrary",)),
        name="s5_carry",
    )(e0, h0, ar_seg, ai_seg)


def _s5_params(lam_re, lam_im, log_dt, b_re, b_im, c_re, c_im, d_skip, seg_len):
    dt = jnp.exp(log_dt.astype(F32))[:, None]
    lr = lam_re.astype(F32)
    li = lam_im.astype(F32)
    mag = jnp.exp(lr * dt)
    ab_re = mag * jnp.cos(li * dt)
    ab_im = mag * jnp.sin(li * dt)
    nr = ab_re - 1.0
    ni = ab_im
    den = lr * lr + li * li
    f_re = (nr * lr + ni * li) / den
    f_im = (ni * lr - nr * li) / den
    br = b_re.astype(F32)
    bi = b_im.astype(F32)
    bb_re = f_re[..., None] * br - f_im[..., None] * bi
    bb_im = f_re[..., None] * bi + f_im[..., None] * br
    mag_s = jnp.exp(lr * dt * seg_len)
    as_re = mag_s * jnp.cos(li * dt * seg_len)
    as_im = mag_s * jnp.sin(li * dt * seg_len)
    eye = jnp.eye(SSM_BLK_GROUPS, dtype=F32)

    def in_proj(w):
        w = w.reshape(SSM_NBLK, SSM_BLK_GROUPS, SSM_STATE, SSM_GROUP_CH).transpose(0, 1, 3, 2)
        return jnp.einsum("bjcp,jk->bjckp", w, eye).reshape(SSM_NBLK, SSM_BLK_CH, SSM_BLK_STATE)

    def out_proj(w):
        w = w.reshape(SSM_NBLK, SSM_BLK_GROUPS, SSM_GROUP_CH, SSM_STATE).transpose(0, 1, 3, 2)
        return jnp.einsum("bjpc,jk->bjpkc", w, eye).reshape(SSM_NBLK, SSM_BLK_STATE, SSM_BLK_CH)

    bd = jnp.concatenate([in_proj(bb_re), in_proj(bb_im)], axis=-1).astype(BF16)
    cbd = jnp.concatenate([out_proj(c_re.astype(F32)), -out_proj(c_im.astype(F32))], axis=1).astype(BF16)
    blk = lambda v: v.reshape(SSM_NBLK, 1, SSM_BLK_STATE)
    return (blk(ab_re), blk(ab_im), blk(as_re), blk(as_im), bd, cbd,
            d_skip.astype(F32).reshape(SSM_NBLK, 1, SSM_BLK_CH))


def _state_to_blocks(h_re, h_im):
    nb = h_re.shape[0]
    f = lambda h: h.astype(F32).reshape(nb, SSM_NBLK, SSM_BLK_STATE).transpose(1, 0, 2)
    return jnp.concatenate([f(h_re), f(h_im)], axis=-1)


def _blocks_to_state(hb):
    nb = hb.shape[1]
    f = lambda h: h.transpose(1, 0, 2).reshape(nb, SSM_GROUPS, SSM_STATE)
    return f(hb[..., :SSM_BLK_STATE]), f(hb[..., SSM_BLK_STATE:])


def s5_layer(u, h0_re, h0_im, prm):
    nb, L, d = u.shape
    nseg = SSM_SEQS // nb
    seg = L // nseg
    lam_re, lam_im, log_dt, b_re, b_im, c_re, c_im, d_skip = prm
    ar, ai, as_re, as_im, bd, cbd, dsk = _s5_params(lam_re, lam_im, log_dt, b_re, b_im, c_re, c_im, d_skip, seg)
    ut = u.reshape(nb * nseg, seg, d).transpose(1, 0, 2)
    h0b = _state_to_blocks(h0_re, h0_im)
    if nseg == 1:
        zt, hT = _s5_scan(ut, h0b, ar, ai, bd, cbd, dsk, True)
    else:
        zero = jnp.zeros((SSM_NBLK, SSM_SEQS, 2 * SSM_BLK_STATE), F32)
        _, e0 = _s5_scan(ut, zero, ar, ai, bd, None, None, False)
        hstart, hT = _s5_carry(e0, h0b, as_re, as_im, nb, nseg)
        zt, _ = _s5_scan(ut, hstart, ar, ai, bd, cbd, dsk, True)
    z = zt.astype(BF16).transpose(1, 0, 2).reshape(nb * L, d)
    hr, hi = _blocks_to_state(hT)
    return z, hr, hi


def _gla_gate_body(x_ref, w1_ref, w2_ref, b_ref, o_ref):
    glr = jnp.dot(x_ref[...], w1_ref[...].astype(BF16), preferred_element_type=F32)
    logits = jnp.dot(glr.astype(BF16), w2_ref[...].astype(BF16), preferred_element_type=F32) + b_ref[...]
    o_ref[...] = jax.nn.log_sigmoid(logits) / GLA_TAU


def gla_gate(xn, w_glr, w_gate_up, b_gate, tm=512):
    m, k = xn.shape
    rp = 128
    w1 = jnp.pad(w_glr.astype(F32), ((0, 0), (0, rp - GLA_GATE_RANK)))
    w2 = jnp.pad(w_gate_up.astype(F32), ((0, rp - GLA_GATE_RANK), (0, 0)))
    return pl.pallas_call(
        _gla_gate_body,
        out_shape=jax.ShapeDtypeStruct((m, GLA_DK), F32),
        grid=(m // tm,),
        in_specs=[pl.BlockSpec((tm, k), lambda i: (i, 0)),
                  pl.BlockSpec((k, rp), lambda i: (0, 0)),
                  pl.BlockSpec((rp, GLA_DK), lambda i: (0, 0)),
                  pl.BlockSpec((1, GLA_DK), lambda i: (0, 0))],
        out_specs=pl.BlockSpec((tm, GLA_DK), lambda i: (i, 0)),
        compiler_params=_cparams(("arbitrary",)),
        name="gla_gate",
    )(xn, w1, w2, b_gate.reshape(1, GLA_DK).astype(F32))


def _gla_body(lc, q_ref, k_ref, v_ref, r_ref, la_ref, s0_ref, g_ref, o_ref, st_ref, s_ref):
    c = pl.program_id(2)

    @pl.when(c == 0)
    def _():
        s_ref[...] = s0_ref[...].astype(F32)

    la = la_ref[...]
    la_hi = la.astype(BF16)
    la_lo = (la - la_hi.astype(F32)).astype(BF16)
    row = lax.broadcasted_iota(jnp.int32, (lc, lc), 0)
    col = lax.broadcasted_iota(jnp.int32, (lc, lc), 1)
    tri = (col <= row).astype(BF16)
    cum = (jnp.dot(tri, la_hi, preferred_element_type=F32)
           + jnp.dot(tri, la_lo, preferred_element_type=F32))
    end = cum[lc - 1:lc, :]
    kd = (k_ref[...].astype(F32) * jnp.exp(end - cum)).astype(BF16)
    ones = jnp.ones((lc, 128), BF16)
    tn = (((0,), (0,)), ((), ()))
    end_col = (lax.dot_general(la_hi, ones, tn, preferred_element_type=F32)
               + lax.dot_general(la_lo, ones, tn, preferred_element_type=F32))
    decay = jnp.exp(end_col[:, 0:1])
    s_new = decay * s_ref[...] + lax.dot_general(kd, v_ref[...], tn, preferred_element_type=F32)
    s_ref[...] = s_new
    qs = (q_ref[...].astype(F32) * (GLA_DK_HEAD ** -0.5)).astype(BF16)
    o = jnp.dot(qs, s_new.astype(BF16), preferred_element_type=F32)
    o = o * lax.rsqrt(jnp.mean(o * o, axis=-1, keepdims=True) + EPS)
    o = o * g_ref[...]
    o_ref[...] = (o * jax.nn.silu(r_ref[...].astype(F32))).astype(o_ref.dtype)

    @pl.when(c == pl.num_programs(2) - 1)
    def _():
        st_ref[...] = s_new


def gla_recurrence(proj, la, row0, nb, L, s0, norm_g):
    lc = min(GLA_CHUNK, L)
    nc = L // lc
    base = row0 // lc
    kq = GLA_DK // GLA_DK_HEAD
    kv = 2 * GLA_DK // GLA_DV_HEAD
    kr = kv + GLA_HEADS
    rows = lambda b, h, c: base + b * nc + c
    o, st = pl.pallas_call(
        functools.partial(_gla_body, lc),
        out_shape=[jax.ShapeDtypeStruct((nb * L, GLA_DV), BF16),
                   jax.ShapeDtypeStruct((nb, GLA_HEADS, GLA_DK_HEAD, GLA_DV_HEAD), F32)],
        grid=(nb, GLA_HEADS, nc),
        in_specs=[pl.BlockSpec((lc, GLA_DK_HEAD), lambda b, h, c: (rows(b, h, c), h)),
                  pl.BlockSpec((lc, GLA_DK_HEAD), lambda b, h, c: (rows(b, h, c), kq + h)),
                  pl.BlockSpec((lc, GLA_DV_HEAD), lambda b, h, c: (rows(b, h, c), kv + h)),
                  pl.BlockSpec((lc, GLA_DV_HEAD), lambda b, h, c: (rows(b, h, c), kr + h)),
                  pl.BlockSpec((lc, GLA_DK_HEAD), lambda b, h, c: (rows(b, h, c), h)),
                  pl.BlockSpec((None, None, GLA_DK_HEAD, GLA_DV_HEAD), lambda b, h, c: (b, h, 0, 0)),
                  pl.BlockSpec((1, GLA_DV_HEAD), lambda b, h, c: (0, h))],
        out_specs=[pl.BlockSpec((lc, GLA_DV_HEAD), lambda b, h, c: (b * nc + c, h)),
                   pl.BlockSpec((None, None, GLA_DK_HEAD, GLA_DV_HEAD), lambda b, h, c: (b, h, 0, 0))],
        scratch_shapes=[pltpu.VMEM((GLA_DK_HEAD, GLA_DV_HEAD), F32)],
        compiler_params=_cparams(("arbitrary", "arbitrary", "arbitrary")),
        name="gla_recurrence",
    )(proj, proj, proj, proj, la, s0, norm_g.reshape(1, GLA_DV).astype(F32))
    return o, st


def _attn_body(q_ref, k_ref, v_ref, o_ref):
    nt = (((1,), (1,)), ((), ()))
    for h in range(CA_HEADS):
        sl = slice(h * CA_HEAD_DIM, (h + 1) * CA_HEAD_DIM)
        kh = k_ref[:, sl].astype(BF16)
        vh = v_ref[:, sl].astype(BF16)
        s = lax.dot_general(q_ref[:, sl], kh, nt, preferred_element_type=F32) * (CA_HEAD_DIM ** -0.5)
        s = s - jnp.max(s, axis=-1, keepdims=True)
        p = jnp.exp(s)
        p = p / jnp.sum(p, axis=-1, keepdims=True)
        o_ref[:, sl] = jnp.dot(p.astype(BF16), vh, preferred_element_type=F32).astype(o_ref.dtype)


def mem_attention(q, row0, nb, L, mk, mv):
    tl = min(L, 512)
    nl = L // tl
    base = row0 // tl
    return pl.pallas_call(
        _attn_body,
        out_shape=jax.ShapeDtypeStruct((nb * L, D_MODEL), BF16),
        grid=(nb, nl),
        in_specs=[pl.BlockSpec((tl, D_MODEL), lambda b, i: (base + b * nl + i, 0)),
                  pl.BlockSpec((None, N_MEM, D_MODEL), lambda b, i: (b, 0, 0)),
                  pl.BlockSpec((None, N_MEM, D_MODEL), lambda b, i: (b, 0, 0))],
        out_specs=pl.BlockSpec((tl, D_MODEL), lambda b, i: (b * nl + i, 0)),
        compiler_params=_cparams(("arbitrary", "arbitrary")),
        name="mem_attention",
    )(q, mk, mv)


def _router_body(x_ref, g_ref, w_ref, b_ref, xn_ref, ids_ref, gates_ref):
    x = x_ref[...]
    ms = jnp.mean(x * x, axis=-1, keepdims=True)
    xn = (x * lax.rsqrt(ms + EPS)) * g_ref[...]
    xn_ref[...] = xn.astype(xn_ref.dtype)
    xh = xn.astype(BF16)
    xl = (xn - xh.astype(F32)).astype(BF16)
    w = w_ref[...]
    wh = w.astype(BF16)
    wl = (w - wh.astype(F32)).astype(BF16)
    logits = (jnp.dot(xh, wh, preferred_element_type=F32) + jnp.dot(xh, wl, preferred_element_type=F32)
              + jnp.dot(xl, wh, preferred_element_type=F32)) + b_ref[...]
    tm = logits.shape[0]
    col = lax.broadcasted_iota(jnp.int32, (tm, ROUTER_LANES), 1).astype(F32)
    neg = jnp.float32(-jnp.inf)
    first = lambda mask: jnp.min(jnp.where(mask, col, float(ROUTER_LANES)), axis=-1, keepdims=True)
    gl = jnp.where(col < MOE_GROUPS, logits, neg)
    gmax = jnp.max(gl, axis=-1, keepdims=True)
    gidx = first(gl == gmax)
    g_w = 1.0 / jnp.sum(jnp.exp(gl - gmax), axis=-1, keepdims=True)
    lo = MOE_GROUPS + gidx * MOE_EPG
    el = jnp.where((col >= lo) & (col < lo + MOE_EPG), logits, neg)
    m1 = jnp.max(el, axis=-1, keepdims=True)
    i1 = first(el == m1)
    z = jnp.sum(jnp.exp(el - m1), axis=-1, keepdims=True)
    p1 = 1.0 / z
    el2 = jnp.where(col == i1, neg, el)
    m2 = jnp.max(el2, axis=-1, keepdims=True)
    i2 = first(el2 == m2)
    p2 = jnp.exp(m2 - m1) / z
    tot = p1 + p2
    ids = jnp.where(col == 0, i1 - MOE_GROUPS, jnp.where(col == 1, i2 - MOE_GROUPS, 0.0))
    ids_ref[...] = ids.astype(jnp.int32)
    gates_ref[...] = jnp.where(col == 0, g_w * p1 / tot, jnp.where(col == 1, g_w * p2 / tot, 0.0))


def moe_router(x, g, w_group, b_group, w_expert, b_expert, tm=256):
    m, d = x.shape
    npad = ROUTER_LANES - MOE_GROUPS - MOE_EXPERTS
    w = jnp.pad(jnp.concatenate([w_group, w_expert], axis=1).astype(F32), ((0, 0), (0, npad)))
    b = jnp.pad(jnp.concatenate([b_group, b_expert]).astype(F32), (0, npad)).reshape(1, ROUTER_LANES)
    row = lambda n: pl.BlockSpec((tm, n), lambda i: (i, 0))
    xn, ids, gates = pl.pallas_call(
        _router_body,
        out_shape=[jax.ShapeDtypeStruct((m, d), F32),
                   jax.ShapeDtypeStruct((m, ROUTER_LANES), jnp.int32),
                   jax.ShapeDtypeStruct((m, ROUTER_LANES), F32)],
        grid=(m // tm,),
        in_specs=[row(d), pl.BlockSpec((1, d), lambda i: (0, 0)),
                  pl.BlockSpec((d, ROUTER_LANES), lambda i: (0, 0)),
                  pl.BlockSpec((1, ROUTER_LANES), lambda i: (0, 0))],
        out_specs=[row(d), row(ROUTER_LANES), row(ROUTER_LANES)],
        compiler_params=_cparams(("arbitrary",)),
        name="moe_router",
    )(x, g.reshape(1, d).astype(F32), w, b)
    return xn, ids[:, :MOE_TOP_K], gates


def _gather_body(tm, src_ref, nt_ref, x_hbm, o_ref, buf_ref, sem):
    i = pl.program_id(0)

    @pl.when(i < nt_ref[0])
    def _():
        def copy(r):
            return pltpu.make_async_copy(x_hbm.at[pl.ds(src_ref[i * tm + r], 1), :],
                                         buf_ref.at[pl.ds(r, 1), :], sem)

        def start(r, _):
            copy(r).start()
            return 0

        def wait(r, _):
            copy(r).wait()
            return 0

        lax.fori_loop(0, tm, start, 0)
        lax.fori_loop(0, tm, wait, 0)
        o_ref[...] = buf_ref[...].astype(o_ref.dtype)

    @pl.when(i >= nt_ref[0])
    def _():
        o_ref[...] = jnp.zeros_like(o_ref)


def moe_gather(x, row_src, nt_used, n_tiles, tm):
    d = x.shape[1]
    return pl.pallas_call(
        functools.partial(_gather_body, tm),
        out_shape=jax.ShapeDtypeStruct((n_tiles * tm, d), BF16),
        grid_spec=pltpu.PrefetchScalarGridSpec(
            num_scalar_prefetch=2,
            grid=(n_tiles,),
            in_specs=[pl.BlockSpec(memory_space=pl.ANY)],
            out_specs=pl.BlockSpec((tm, d), lambda i, src, nt: (i, 0)),
            scratch_shapes=[pltpu.VMEM((tm, d), x.dtype), pltpu.SemaphoreType.DMA(())],
        ),
        compiler_params=_cparams(("arbitrary",)),
        name="moe_gather",
    )(row_src, nt_used, x)


def _expert_changed(te_ref, r):
    prev = te_ref[jnp.maximum(r - 1, 0)]
    return jnp.logical_or(r == 0, te_ref[r] != prev)


def _moe_up_body(te_ref, nt_ref, x_ref, wg_ref, wu_ref, h_ref, wgb_ref, wub_ref):
    r = pl.program_id(1)

    @pl.when(_expert_changed(te_ref, r))
    def _():
        wgb_ref[...] = wg_ref[...].astype(BF16)
        wub_ref[...] = wu_ref[...].astype(BF16)

    @pl.when(r < nt_ref[0])
    def _():
        x = x_ref[...]
        gate = jnp.dot(x, wgb_ref[...], preferred_element_type=F32)
        up = jnp.dot(x, wub_ref[...], preferred_element_type=F32)
        h_ref[...] = (jax.nn.silu(gate) * up).astype(h_ref.dtype)

    @pl.when(r >= nt_ref[0])
    def _():
        h_ref[...] = jnp.zeros_like(h_ref)


def _moe_down_body(te_ref, nt_ref, h_ref, wd_ref, y_ref, wdb_ref):
    r = pl.program_id(1)

    @pl.when(_expert_changed(te_ref, r))
    def _():
        wdb_ref[...] = wd_ref[...].astype(BF16)

    @pl.when(r < nt_ref[0])
    def _():
        y_ref[...] = jnp.dot(h_ref[...], wdb_ref[...], preferred_element_type=F32)

    @pl.when(r >= nt_ref[0])
    def _():
        y_ref[...] = jnp.zeros_like(y_ref)


def moe_experts(xs, tile_e, nt_used, w_gate_up, w_down, layer, tm, tf=512, tnd=1024):
    p, d = xs.shape
    n_tiles = p // tm
    f = MOE_HIDDEN
    nfc = f // tf
    h = pl.pallas_call(
        _moe_up_body,
        out_shape=jax.ShapeDtypeStruct((p, f), BF16),
        grid_spec=pltpu.PrefetchScalarGridSpec(
            num_scalar_prefetch=2,
            grid=(nfc, n_tiles),
            in_specs=[pl.BlockSpec((tm, d), lambda c, r, te, nt: (r, 0)),
                      pl.BlockSpec((None, None, d, tf), lambda c, r, te, nt: (layer, te[r], 0, c)),
                      pl.BlockSpec((None, None, d, tf), lambda c, r, te, nt: (layer, te[r], 0, nfc + c))],
            out_specs=pl.BlockSpec((tm, tf), lambda c, r, te, nt: (r, c)),
            scratch_shapes=[pltpu.VMEM((d, tf), BF16), pltpu.VMEM((d, tf), BF16)],
        ),
        compiler_params=_cparams(("arbitrary", "arbitrary")),
        name="moe_up",
    )(tile_e, nt_used, xs, w_gate_up, w_gate_up)
    return pl.pallas_call(
        _moe_down_body,
        out_shape=jax.ShapeDtypeStruct((p, d), F32),
        grid_spec=pltpu.PrefetchScalarGridSpec(
            num_scalar_prefetch=2,
            grid=(d // tnd, n_tiles),
            in_specs=[pl.BlockSpec((tm, f), lambda c, r, te, nt: (r, 0)),
                      pl.BlockSpec((None, None, f, tnd), lambda c, r, te, nt: (layer, te[r], 0, c))],
            out_specs=pl.BlockSpec((tm, tnd), lambda c, r, te, nt: (r, c)),
            scratch_shapes=[pltpu.VMEM((f, tnd), BF16)],
        ),
        compiler_params=_cparams(("arbitrary", "arbitrary")),
        name="moe_down",
    )(tile_e, nt_used, h, w_down)


def _combine_body(tt, pos_ref, x_ref, g_ref, y_hbm, o_ref, buf_ref, sem):
    i = pl.program_id(0)

    def copy(t, k):
        return pltpu.make_async_copy(y_hbm.at[pl.ds(pos_ref[(i * tt + t) * MOE_TOP_K + k], 1), :],
                                     buf_ref.at[k, pl.ds(t, 1), :], sem)

    def start(t, _):
        for k in range(MOE_TOP_K):
            copy(t, k).start()
        return 0

    def wait(t, _):
        for k in range(MOE_TOP_K):
            copy(t, k).wait()
        return 0

    lax.fori_loop(0, tt, start, 0)
    lax.fori_loop(0, tt, wait, 0)
    g = g_ref[...]
    y = g[:, 0:1] * buf_ref[0] + g[:, 1:2] * buf_ref[1]
    o_ref[...] = x_ref[...] + y


def moe_combine(x, gates, pos, y_sorted, tt=256):
    m, d = x.shape
    return pl.pallas_call(
        functools.partial(_combine_body, tt),
        out_shape=jax.ShapeDtypeStruct((m, d), F32),
        grid_spec=pltpu.PrefetchScalarGridSpec(
            num_scalar_prefetch=1,
            grid=(m // tt,),
            in_specs=[pl.BlockSpec((tt, d), lambda i, pos: (i, 0)),
                      pl.BlockSpec((tt, ROUTER_LANES), lambda i, pos: (i, 0)),
                      pl.BlockSpec(memory_space=pl.ANY)],
            out_specs=pl.BlockSpec((tt, d), lambda i, pos: (i, 0)),
            scratch_shapes=[pltpu.VMEM((MOE_TOP_K, tt, d), F32), pltpu.SemaphoreType.DMA(())],
        ),
        compiler_params=_cparams(("arbitrary",)),
        name="moe_combine",
    )(pos, x, gates, y_sorted)


def _moe_plan(ids, tm):
    t = ids.shape[0]
    n = t * MOE_TOP_K
    n_tiles = -(-(n + MOE_EXPERTS * (tm - 1)) // tm)
    flat_e = ids.reshape(n)
    order = jnp.argsort(flat_e, stable=True).astype(jnp.int32)
    se = flat_e[order]
    counts = jnp.zeros((MOE_EXPERTS,), jnp.int32).at[flat_e].add(1)
    starts = jnp.cumsum(counts) - counts
    padded = (counts + tm - 1) // tm * tm
    pends = jnp.cumsum(padded)
    pstarts = pends - padded
    dest = pstarts[se] + (jnp.arange(n, dtype=jnp.int32) - starts[se])
    row_src = jnp.zeros((n_tiles * tm,), jnp.int32).at[dest].set(order // MOE_TOP_K)
    pos = jnp.zeros((n,), jnp.int32).at[order].set(dest)
    tile_start = jnp.arange(n_tiles, dtype=jnp.int32) * tm
    tile_e = jnp.minimum(jnp.searchsorted(pends, tile_start, side="right"), MOE_EXPERTS - 1).astype(jnp.int32)
    nt_used = (pends[-1] // tm).astype(jnp.int32).reshape(1)
    return row_src, pos, tile_e, nt_used, n_tiles


def hier_moe(x, g, layer, w_group, b_group, w_expert, b_expert, w_gate_up, w_down):
    xn, ids, gates = moe_router(x, g, w_group, b_group, w_expert, b_expert)
    row_src, pos, tile_e, nt_used, n_tiles = _moe_plan(ids, MOE_TILE)
    xs = moe_gather(xn, row_src, nt_used, n_tiles, MOE_TILE)
    y = moe_experts(xs, tile_e, nt_used, w_gate_up, w_down, layer, MOE_TILE)
    return moe_combine(x, gates, pos, y)


def kernel(x_prompt, x_sample, mem_prompt, state_ssm_re, state_ssm_im, state_gla, cache_mem_k, cache_mem_v, norm_mixer, norm_ca, norm_moe, norm_final, ssm_lambda_re, ssm_lambda_im, ssm_log_dt, ssm_b_re, ssm_b_im, ssm_c_re, ssm_c_im, ssm_d, ssm_w_glu, gla_w_in, gla_w_gate_up, gla_b_gate, gla_norm, gla_w_out, ca_mem_norm, ca_w_q, ca_w_kv, ca_w_o, moe_w_group, moe_b_group, moe_w_expert, moe_b_expert, moe_w_gate_up, moe_w_down):
    bp, lp, d = x_prompt.shape
    bs, ls, _ = x_sample.shape
    np_ = bp * lp
    ns_ = bs * ls

    mem = mem_prompt.reshape(bp * N_MEM, d)
    pk, pv = [], []
    for i in range(DEPTH):
        mn = rmsnorm(mem, ca_mem_norm[i])
        kv = matmul(mn, ca_w_kv[i], out_dtype=F32)
        pk.append(kv[:, :d].reshape(bp, N_MEM, d))
        pv.append(kv[:, d:].reshape(bp, N_MEM, d))
    prompt_mem_k = jnp.stack(pk).reshape(DEPTH, bp, N_MEM, CA_HEADS, CA_HEAD_DIM)
    prompt_mem_v = jnp.stack(pv).reshape(DEPTH, bp, N_MEM, CA_HEADS, CA_HEAD_DIM)
    cache_k = cache_mem_k.reshape(DEPTH, bs, N_MEM, d)
    cache_v = cache_mem_v.reshape(DEPTH, bs, N_MEM, d)

    x = jnp.concatenate([x_prompt.reshape(np_, d), x_sample.reshape(ns_, d)], axis=0)
    zero_ssm = jnp.zeros((bp, SSM_GROUPS, SSM_STATE), F32)
    zero_gla = jnp.zeros((bp, GLA_HEADS, GLA_DK_HEAD, GLA_DV_HEAD), F32)
    p_re, p_im, s_re, s_im, p_gla, s_gla = [], [], [], [], [], []

    for i in range(DEPTH):
        j = i // 2
        if i % 2 == 0:
            xn = rmsnorm(x, norm_mixer[i], out_dtype=F32)
            prm = (ssm_lambda_re[j], ssm_lambda_im[j], ssm_log_dt[j], ssm_b_re[j], ssm_b_im[j],
                   ssm_c_re[j], ssm_c_im[j], ssm_d[j])
            zp, hr, hi = s5_layer(xn[:np_].reshape(bp, lp, d), zero_ssm, zero_ssm, prm)
            p_re.append(hr)
            p_im.append(hi)
            zs, hr, hi = s5_layer(xn[np_:].reshape(bs, ls, d), state_ssm_re[j], state_ssm_im[j], prm)
            s_re.append(hr)
            s_im.append(hi)
            z = jnp.concatenate([zp, zs], axis=0)
            x = matmul(z, ssm_w_glu[j], mode="glu_res", res=x, n_out=d, tn=256)
        else:
            xn = rmsnorm(x, norm_mixer[i])
            proj = matmul(xn, gla_w_in[j], out_dtype=BF16, n_out=GLA_MAIN)
            la = gla_gate(xn, gla_w_in[j][:, GLA_MAIN:], gla_w_gate_up[j], gla_b_gate[j])
            op, st = gla_recurrence(proj, la, 0, bp, lp, zero_gla, gla_norm[j])
            p_gla.append(st)
            os_, st = gla_recurrence(proj, la, np_, bs, ls, state_gla[j], gla_norm[j])
            s_gla.append(st)
            x = matmul(jnp.concatenate([op, os_], axis=0), gla_w_out[j], mode="res", res=x)

        xn = rmsnorm(x, norm_ca[i])
        q = matmul(xn, ca_w_q[i], out_dtype=BF16)
        ap = mem_attention(q, 0, bp, lp, pk[i], pv[i])
        as_ = mem_attention(q, np_, bs, ls, cache_k[i], cache_v[i])
        x = matmul(jnp.concatenate([ap, as_], axis=0), ca_w_o[i], mode="res", res=x)

        x = hier_moe(x, norm_moe[i], i, moe_w_group[i], moe_b_group[i], moe_w_expert[i], moe_b_expert[i],
                     moe_w_gate_up, moe_w_down)

    y = rmsnorm(x, norm_final, out_dtype=F32)
    return (y[:np_].reshape(bp, lp, d), y[np_:].reshape(bs, ls, d),
            jnp.stack(p_re), jnp.stack(p_im), jnp.stack(p_gla), prompt_mem_k, prompt_mem_v,
            jnp.stack(s_re), jnp.stack(s_im), jnp.stack(s_gla))
```

```python
import functools
import math

import jax
import jax.numpy as jnp
import numpy as np
from jax import lax
from jax.experimental import pallas as pl
from jax.experimental.pallas import tpu as pltpu

F32 = jnp.float32
BF16 = jnp.bfloat16

EPS = 1e-6
D_MODEL = 4096
DEPTH = 2

SSM_GROUP_CH = 16
SSM_GROUPS = D_MODEL // SSM_GROUP_CH
SSM_STATE = 64
SSM_SEQS = 32
SSM_BLK_CH = 128
SSM_BLK_GROUPS = SSM_BLK_CH // SSM_GROUP_CH
SSM_NBLK = D_MODEL // SSM_BLK_CH
SSM_BLK_STATE = SSM_BLK_GROUPS * SSM_STATE

GLA_HEADS = 4
GLA_DK = D_MODEL // 2
GLA_DV = D_MODEL
GLA_DK_HEAD = GLA_DK // GLA_HEADS
GLA_DV_HEAD = GLA_DV // GLA_HEADS
GLA_GATE_RANK = 16
GLA_TAU = 16.0
GLA_MAIN = 2 * GLA_DK + 2 * GLA_DV
GLA_CHUNK = 64

N_MEM = 256
CA_HEADS = 4
CA_HEAD_DIM = D_MODEL // CA_HEADS

MOE_GROUPS = 4
MOE_EPG = 8
MOE_EXPERTS = MOE_GROUPS * MOE_EPG
MOE_TOP_K = 2
MOE_HIDDEN = D_MODEL // 4
MOE_TILE = 256
GATHER_GROUP = 8
ROUTER_LANES = 128

SUBLANES = 8
VMEM_LIMIT = 56 * 1024 * 1024


def _cparams(sem):
    return pltpu.CompilerParams(dimension_semantics=sem, vmem_limit_bytes=VMEM_LIMIT)


def _rms_body(n_add, emit_sum, x_ref, *refs):
    adds = refs[:n_add]
    g_ref = refs[n_add]
    outs = refs[n_add + 1:]
    x = x_ref[...].astype(F32)
    for a in adds:
        x = x + a[...].astype(F32)
    ms = jnp.mean(x * x, axis=-1, keepdims=True)
    xn = (x * lax.rsqrt(ms + EPS)) * g_ref[...]
    if emit_sum:
        outs[0][...] = x
        outs[1][...] = xn.astype(outs[1].dtype)
    else:
        outs[0][...] = xn.astype(outs[0].dtype)


def rmsnorm(x, g, adds=(), out_dtype=BF16, emit_sum=False, tm=256, row0=0, nrows=None):
    d = x.shape[1]
    m = x.shape[0] if nrows is None else nrows
    assert m % tm == 0 and row0 % tm == 0
    b0 = row0 // tm
    src = pl.BlockSpec((tm, d), lambda i: (b0 + i, 0))
    row = pl.BlockSpec((tm, d), lambda i: (i, 0))
    out_shape = [jax.ShapeDtypeStruct((m, d), out_dtype)]
    out_specs = [row]
    if emit_sum:
        out_shape.insert(0, jax.ShapeDtypeStruct((m, d), F32))
        out_specs.insert(0, row)
    res = pl.pallas_call(
        functools.partial(_rms_body, len(adds), emit_sum),
        out_shape=out_shape,
        grid=(m // tm,),
        in_specs=[src] * (1 + len(adds)) + [pl.BlockSpec((1, d), lambda i: (0, 0))],
        out_specs=out_specs,
        compiler_params=_cparams(("arbitrary",)),
        name="rmsnorm",
    )(x, *adds, g.reshape(1, d).astype(F32))
    return res if emit_sum else res[0]


def _mm_body(mode, x_ref, *refs):
    if mode == "glu_res":
        w_ref, w2_ref, res_ref, o_ref, wb_ref, wb2_ref = refs
    elif mode == "res":
        w_ref, res_ref, o_ref, wb_ref = refs
    else:
        w_ref, o_ref, wb_ref = refs

    @pl.when(pl.program_id(1) == 0)
    def _():
        wb_ref[...] = w_ref[...].astype(BF16)
        if mode == "glu_res":
            wb2_ref[...] = w2_ref[...].astype(BF16)

    x = x_ref[...]
    acc = jnp.dot(x, wb_ref[...], preferred_element_type=F32)
    if mode == "glu_res":
        gate = jnp.dot(x, wb2_ref[...], preferred_element_type=F32)
        o_ref[...] = res_ref[...] + acc * jax.nn.sigmoid(gate)
    elif mode == "res":
        o_ref[...] = res_ref[...] + acc
    else:
        o_ref[...] = acc.astype(o_ref.dtype)


def matmul(x, w, layer, mode="plain", res=None, out_dtype=F32, n_out=None, tm=512, tn=512):
    m, k = x.shape
    n = n_out if n_out is not None else w.shape[2]
    if m % tm:
        tm = m
    assert m % tm == 0 and n % tn == 0
    nb = n // tn
    in_specs = [pl.BlockSpec((tm, k), lambda j, i: (i, 0)),
                pl.BlockSpec((None, k, tn), lambda j, i: (layer, 0, j))]
    args = [x, w]
    scratch = [pltpu.VMEM((k, tn), BF16)]
    if mode == "glu_res":
        in_specs.append(pl.BlockSpec((None, k, tn), lambda j, i: (layer, 0, nb + j)))
        args.append(w)
        scratch.append(pltpu.VMEM((k, tn), BF16))
    if mode in ("res", "glu_res"):
        in_specs.append(pl.BlockSpec((tm, tn), lambda j, i: (i, j)))
        args.append(res)
        out_dtype = F32
    return pl.pallas_call(
        functools.partial(_mm_body, mode),
        out_shape=jax.ShapeDtypeStruct((m, n), out_dtype),
        grid=(nb, m // tm),
        in_specs=in_specs,
        out_specs=pl.BlockSpec((tm, tn), lambda j, i: (i, j)),
        scratch_shapes=scratch,
        compiler_params=_cparams(("arbitrary", "arbitrary")),
        name="matmul_" + mode,
    )(*args)


def _s5_body(lt, emit_y, u_ref, h0_ref, ar_ref, ai_ref, bd_ref, *refs):
    if emit_y:
        cbd_ref, d_ref, z_ref, ht_ref, h_ref, bu_ref, hs_ref = refs
    else:
        ht_ref, h_ref, bu_ref = refs
    tb = pl.program_id(2)
    ns = SSM_BLK_STATE

    @pl.when(tb == 0)
    def _():
        h_ref[...] = h0_ref[...]

    u = u_ref[...].reshape(lt * SUBLANES, SSM_BLK_CH)
    bu_ref[...] = jnp.dot(u.astype(BF16), bd_ref[...], preferred_element_type=F32)
    ar = jnp.broadcast_to(ar_ref[...], (SUBLANES, ns))
    ai = jnp.broadcast_to(ai_ref[...], (SUBLANES, ns))

    def step(t, carry):
        hr, hi = carry
        r0 = pl.multiple_of(t * SUBLANES, SUBLANES)
        nr = ar * hr - ai * hi + bu_ref[pl.ds(r0, SUBLANES), 0:ns]
        ni = ar * hi + ai * hr + bu_ref[pl.ds(r0, SUBLANES), ns:2 * ns]
        if emit_y:
            hs_ref[pl.ds(r0, SUBLANES), 0:ns] = nr
            hs_ref[pl.ds(r0, SUBLANES), ns:2 * ns] = ni
        return nr, ni

    hr, hi = lax.fori_loop(0, lt, step, (h_ref[:, 0:ns], h_ref[:, ns:2 * ns]), unroll=4)
    h_ref[:, 0:ns] = hr
    h_ref[:, ns:2 * ns] = hi
    ht_ref[:, 0:ns] = hr
    ht_ref[:, ns:2 * ns] = hi
    if emit_y:
        y = jnp.dot(hs_ref[...].astype(BF16), cbd_ref[...], preferred_element_type=F32)
        y = y + d_ref[...] * u
        z_ref[...] = jax.nn.gelu(y, approximate=True).reshape(lt, SUBLANES, SSM_BLK_CH)


def _s5_scan(ut, h0, ar, ai, bd, cbd, dsk, emit_y):
    ltot = ut.shape[0]
    lt = min(ltot, 64)
    assert ltot % lt == 0
    ns2 = 2 * SSM_BLK_STATE
    nsg = SSM_SEQS // SUBLANES
    in_specs = [
        pl.BlockSpec((lt, SUBLANES, SSM_BLK_CH), lambda c, s, t: (t, s, c)),
        pl.BlockSpec((None, SUBLANES, ns2), lambda c, s, t: (c, s, 0)),
        pl.BlockSpec((None, 1, SSM_BLK_STATE), lambda c, s, t: (c, 0, 0)),
        pl.BlockSpec((None, 1, SSM_BLK_STATE), lambda c, s, t: (c, 0, 0)),
        pl.BlockSpec((None, SSM_BLK_CH, ns2), lambda c, s, t: (c, 0, 0)),
    ]
    args = [ut, h0, ar, ai, bd]
    ht_shape = jax.ShapeDtypeStruct((SSM_NBLK, SSM_SEQS, ns2), F32)
    ht_spec = pl.BlockSpec((None, SUBLANES, ns2), lambda c, s, t: (c, s, 0))
    scratch = [pltpu.VMEM((SUBLANES, ns2), F32), pltpu.VMEM((lt * SUBLANES, ns2), F32)]
    if emit_y:
        in_specs += [pl.BlockSpec((None, ns2, SSM_BLK_CH), lambda c, s, t: (c, 0, 0)),
                     pl.BlockSpec((None, 1, SSM_BLK_CH), lambda c, s, t: (c, 0, 0))]
        args += [cbd, dsk]
        out_shape = [jax.ShapeDtypeStruct(ut.shape, F32), ht_shape]
        out_specs = [pl.BlockSpec((lt, SUBLANES, SSM_BLK_CH), lambda c, s, t: (t, s, c)), ht_spec]
        scratch.append(pltpu.VMEM((lt * SUBLANES, ns2), F32))
    else:
        out_shape = [ht_shape]
        out_specs = [ht_spec]
    res = pl.pallas_call(
        functools.partial(_s5_body, lt, emit_y),
        out_shape=out_shape,
        grid=(SSM_NBLK, nsg, ltot // lt),
        in_specs=in_specs,
        out_specs=out_specs,
        scratch_shapes=scratch,
        compiler_params=_cparams(("arbitrary", "arbitrary", "arbitrary")),
        name="s5_scan_y" if emit_y else "s5_scan_state",
    )(*args)
    return (res[0], res[1]) if emit_y else (None, res[0])


def _s5_carry_body(nb, nseg, e_ref, h0_ref, ar_ref, ai_ref, hs_ref, he_ref):
    ns = SSM_BLK_STATE
    ar = ar_ref[...]
    ai = ai_ref[...]
    for b in range(nb):
        hr = h0_ref[b:b + 1, 0:ns]
        hi = h0_ref[b:b + 1, ns:2 * ns]
        for j in range(nseg):
            s = b * nseg + j
            hs_ref[s:s + 1, 0:ns] = hr
            hs_ref[s:s + 1, ns:2 * ns] = hi
            er = e_ref[s:s + 1, 0:ns]
            ei = e_ref[s:s + 1, ns:2 * ns]
            hr, hi = ar * hr - ai * hi + er, ar * hi + ai * hr + ei
        he_ref[b:b + 1, 0:ns] = hr
        he_ref[b:b + 1, ns:2 * ns] = hi


def _s5_carry(e0, h0, ar_seg, ai_seg, nb, nseg):
    ns2 = 2 * SSM_BLK_STATE
    return pl.pallas_call(
        functools.partial(_s5_carry_body, nb, nseg),
        out_shape=[jax.ShapeDtypeStruct((SSM_NBLK, SSM_SEQS, ns2), F32),
                   jax.ShapeDtypeStruct((SSM_NBLK, nb, ns2), F32)],
        grid=(SSM_NBLK,),
        in_specs=[pl.BlockSpec((None, SSM_SEQS, ns2), lambda c: (c, 0, 0)),
                  pl.BlockSpec((None, nb, ns2), lambda c: (c, 0, 0)),
                  pl.BlockSpec((None, 1, SSM_BLK_STATE), lambda c: (c, 0, 0)),
                  pl.BlockSpec((None, 1, SSM_BLK_STATE), lambda c: (c, 0, 0))],
        out_specs=[pl.BlockSpec((None, SSM_SEQS, ns2), lambda c: (c, 0, 0)),
                   pl.BlockSpec((None, nb, ns2), lambda c: (c, 0, 0))],
        compiler_params=_cparams(("arbitrary",)),
        name="s5_carry",
    )(e0, h0, ar_seg, ai_seg)


def _s5_params(lam_re, lam_im, log_dt, b_re, b_im, c_re, c_im, d_skip, seg_len):
    dt = jnp.exp(log_dt.astype(F32))[:, None]
    lr = lam_re.astype(F32)
    li = lam_im.astype(F32)
    mag = jnp.exp(lr * dt)
    ab_re = mag * jnp.cos(li * dt)
    ab_im = mag * jnp.sin(li * dt)
    nr = ab_re - 1.0
    ni = ab_im
    den = lr * lr + li * li
    f_re = (nr * lr + ni * li) / den
    f_im = (ni * lr - nr * li) / den
    br = b_re.astype(F32)
    bi = b_im.astype(F32)
    bb_re = f_re[..., None] * br - f_im[..., None] * bi
    bb_im = f_re[..., None] * bi + f_im[..., None] * br
    mag_s = jnp.exp(lr * dt * seg_len)
    as_re = mag_s * jnp.cos(li * dt * seg_len)
    as_im = mag_s * jnp.sin(li * dt * seg_len)
    eye = jnp.eye(SSM_BLK_GROUPS, dtype=F32)

    def in_proj(w):
        w = w.reshape(SSM_NBLK, SSM_BLK_GROUPS, SSM_STATE, SSM_GROUP_CH).transpose(0, 1, 3, 2)
        return jnp.einsum("bjcp,jk->bjckp", w, eye).reshape(SSM_NBLK, SSM_BLK_CH, SSM_BLK_STATE)

    def out_proj(w):
        w = w.reshape(SSM_NBLK, SSM_BLK_GROUPS, SSM_GROUP_CH, SSM_STATE).transpose(0, 1, 3, 2)
        return jnp.einsum("bjpc,jk->bjpkc", w, eye).reshape(SSM_NBLK, SSM_BLK_STATE, SSM_BLK_CH)

    bd = jnp.concatenate([in_proj(bb_re), in_proj(bb_im)], axis=-1).astype(BF16)
    cbd = jnp.concatenate([out_proj(c_re.astype(F32)), -out_proj(c_im.astype(F32))], axis=1).astype(BF16)
    blk = lambda v: v.reshape(SSM_NBLK, 1, SSM_BLK_STATE)
    return (blk(ab_re), blk(ab_im), blk(as_re), blk(as_im), bd, cbd,
            d_skip.astype(F32).reshape(SSM_NBLK, 1, SSM_BLK_CH))


def _state_to_blocks(h_re, h_im):
    nb = h_re.shape[0]
    f = lambda h: h.astype(F32).reshape(nb, SSM_NBLK, SSM_BLK_STATE).transpose(1, 0, 2)
    return jnp.concatenate([f(h_re), f(h_im)], axis=-1)


def _blocks_to_state(hb):
    nb = hb.shape[1]
    f = lambda h: h.transpose(1, 0, 2).reshape(nb, SSM_GROUPS, SSM_STATE)
    return f(hb[..., :SSM_BLK_STATE]), f(hb[..., SSM_BLK_STATE:])


def s5_layer(u, h0_re, h0_im, prm):
    nb, L, d = u.shape
    nseg = SSM_SEQS // nb
    seg = L // nseg
    lam_re, lam_im, log_dt, b_re, b_im, c_re, c_im, d_skip = prm
    ar, ai, as_re, as_im, bd, cbd, dsk = _s5_params(lam_re, lam_im, log_dt, b_re, b_im, c_re, c_im, d_skip, seg)
    ut = u.reshape(nb * nseg, seg, d).transpose(1, 0, 2)
    h0b = _state_to_blocks(h0_re, h0_im)
    if nseg == 1:
        zt, hT = _s5_scan(ut, h0b, ar, ai, bd, cbd, dsk, True)
    else:
        zero = jnp.zeros((SSM_NBLK, SSM_SEQS, 2 * SSM_BLK_STATE), F32)
        _, e0 = _s5_scan(ut, zero, ar, ai, bd, None, None, False)
        hstart, hT = _s5_carry(e0, h0b, as_re, as_im, nb, nseg)
        zt, _ = _s5_scan(ut, hstart, ar, ai, bd, cbd, dsk, True)
    z = zt.astype(BF16).transpose(1, 0, 2).reshape(nb * L, d)
    hr, hi = _blocks_to_state(hT)
    return z, hr, hi


def _gla_gate_body(x_ref, w1_ref, w2_ref, b_ref, o_ref):
    glr = jnp.dot(x_ref[...], w1_ref[...].astype(BF16), preferred_element_type=F32)
    logits = jnp.dot(glr.astype(BF16), w2_ref[...].astype(BF16), preferred_element_type=F32) + b_ref[...]
    o_ref[...] = jax.nn.log_sigmoid(logits) / GLA_TAU


def gla_gate(xn, w_glr, w_gate_up, b_gate, tm=512):
    m, k = xn.shape
    rp = 128
    w1 = jnp.pad(w_glr.astype(F32), ((0, 0), (0, rp - GLA_GATE_RANK)))
    w2 = jnp.pad(w_gate_up.astype(F32), ((0, rp - GLA_GATE_RANK), (0, 0)))
    return pl.pallas_call(
        _gla_gate_body,
        out_shape=jax.ShapeDtypeStruct((m, GLA_DK), F32),
        grid=(m // tm,),
        in_specs=[pl.BlockSpec((tm, k), lambda i: (i, 0)),
                  pl.BlockSpec((k, rp), lambda i: (0, 0)),
                  pl.BlockSpec((rp, GLA_DK), lambda i: (0, 0)),
                  pl.BlockSpec((1, GLA_DK), lambda i: (0, 0))],
        out_specs=pl.BlockSpec((tm, GLA_DK), lambda i: (i, 0)),
        compiler_params=_cparams(("arbitrary",)),
        name="gla_gate",
    )(xn, w1, w2, b_gate.reshape(1, GLA_DK).astype(F32))


def _gla_body(lc, q_ref, k_ref, v_ref, r_ref, la_ref, s0_ref, g_ref, o_ref, st_ref, s_ref):
    c = pl.program_id(2)

    @pl.when(c == 0)
    def _():
        s_ref[...] = s0_ref[...].astype(F32)

    la = la_ref[...]
    la_hi = la.astype(BF16)
    la_lo = (la - la_hi.astype(F32)).astype(BF16)
    row = lax.broadcasted_iota(jnp.int32, (lc, lc), 0)
    col = lax.broadcasted_iota(jnp.int32, (lc, lc), 1)
    tri = (col <= row).astype(BF16)
    cum = (jnp.dot(tri, la_hi, preferred_element_type=F32)
           + jnp.dot(tri, la_lo, preferred_element_type=F32))
    end = cum[lc - 1:lc, :]
    kd = (k_ref[...].astype(F32) * jnp.exp(end - cum)).astype(BF16)
    ones = jnp.ones((lc, 128), BF16)
    tn = (((0,), (0,)), ((), ()))
    end_col = (lax.dot_general(la_hi, ones, tn, preferred_element_type=F32)
               + lax.dot_general(la_lo, ones, tn, preferred_element_type=F32))
    decay = jnp.exp(end_col[:, 0:1])
    s_new = decay * s_ref[...] + lax.dot_general(kd, v_ref[...], tn, preferred_element_type=F32)
    s_ref[...] = s_new
    qs = (q_ref[...].astype(F32) * (GLA_DK_HEAD ** -0.5)).astype(BF16)
    o = jnp.dot(qs, s_new.astype(BF16), preferred_element_type=F32)
    o = o * lax.rsqrt(jnp.mean(o * o, axis=-1, keepdims=True) + EPS)
    o = o * g_ref[...]
    o_ref[...] = (o * jax.nn.silu(r_ref[...].astype(F32))).astype(o_ref.dtype)

    @pl.when(c == pl.num_programs(2) - 1)
    def _():
        st_ref[...] = s_new


def _drop_alias_ref(body, n_in, *refs):
    return body(*refs[:n_in], *refs[n_in + 1:])


def gla_recurrence(proj, la, row0, nb, L, s0, norm_g, prev=None):
    ntok = proj.shape[0]
    lc = min(GLA_CHUNK, L)
    nc = L // lc
    base = row0 // lc
    kq = GLA_DK // GLA_DK_HEAD
    kv = 2 * GLA_DK // GLA_DV_HEAD
    kr = kv + GLA_HEADS
    rows = lambda b, h, c: base + b * nc + c
    in_specs = [pl.BlockSpec((lc, GLA_DK_HEAD), lambda b, h, c: (rows(b, h, c), h)),
                pl.BlockSpec((lc, GLA_DK_HEAD), lambda b, h, c: (rows(b, h, c), kq + h)),
                pl.BlockSpec((lc, GLA_DV_HEAD), lambda b, h, c: (rows(b, h, c), kv + h)),
                pl.BlockSpec((lc, GLA_DV_HEAD), lambda b, h, c: (rows(b, h, c), kr + h)),
                pl.BlockSpec((lc, GLA_DK_HEAD), lambda b, h, c: (rows(b, h, c), h)),
                pl.BlockSpec((None, None, GLA_DK_HEAD, GLA_DV_HEAD), lambda b, h, c: (b, h, 0, 0)),
                pl.BlockSpec((1, GLA_DV_HEAD), lambda b, h, c: (0, h))]
    args = [proj, proj, proj, proj, la, s0, norm_g.reshape(1, GLA_DV).astype(F32)]
    body = functools.partial(_gla_body, lc)
    aliases = {}
    if prev is not None:
        in_specs.append(pl.BlockSpec(memory_space=pl.ANY))
        args.append(prev)
        aliases = {len(args) - 1: 0}
        body = functools.partial(_drop_alias_ref, body, len(args) - 1)
    o, st = pl.pallas_call(
        body,
        out_shape=[jax.ShapeDtypeStruct((ntok, GLA_DV), BF16),
                   jax.ShapeDtypeStruct((nb, GLA_HEADS, GLA_DK_HEAD, GLA_DV_HEAD), F32)],
        grid=(nb, GLA_HEADS, nc),
        in_specs=in_specs,
        out_specs=[pl.BlockSpec((lc, GLA_DV_HEAD), lambda b, h, c: (rows(b, h, c), h)),
                   pl.BlockSpec((None, None, GLA_DK_HEAD, GLA_DV_HEAD), lambda b, h, c: (b, h, 0, 0))],
        scratch_shapes=[pltpu.VMEM((GLA_DK_HEAD, GLA_DV_HEAD), F32)],
        input_output_aliases=aliases,
        compiler_params=_cparams(("arbitrary", "arbitrary", "arbitrary")),
        name="gla_recurrence",
    )(*args)
    return o, st


def _attn_body(per_head, q_ref, k_ref, v_ref, o_ref):
    nt = (((1,), (1,)), ((), ()))
    for h in range(CA_HEADS):
        sl = slice(h * CA_HEAD_DIM, (h + 1) * CA_HEAD_DIM)
        if per_head:
            kh = k_ref[:, h, :].astype(BF16)
            vh = v_ref[:, h, :].astype(BF16)
        else:
            kh = k_ref[:, sl].astype(BF16)
            vh = v_ref[:, sl].astype(BF16)
        s = lax.dot_general(q_ref[:, sl], kh, nt, preferred_element_type=F32) * (CA_HEAD_DIM ** -0.5)
        s = s - jnp.max(s, axis=-1, keepdims=True)
        p = jnp.exp(s)
        p = p / jnp.sum(p, axis=-1, keepdims=True)
        o_ref[:, sl] = jnp.dot(p.astype(BF16), vh, preferred_element_type=F32).astype(o_ref.dtype)


def mem_attention(q, row0, nb, L, mk, mv, layer=None, prev=None):
    ntok = q.shape[0]
    tl = min(L, 512)
    nl = L // tl
    base = row0 // tl
    rows = lambda b, i: (base + b * nl + i, 0)
    if layer is None:
        mem_spec = pl.BlockSpec((None, N_MEM, D_MODEL), lambda b, i: (b, 0, 0))
    else:
        mem_spec = pl.BlockSpec((None, None, N_MEM, CA_HEADS, CA_HEAD_DIM), lambda b, i: (layer, b, 0, 0, 0))
    in_specs = [pl.BlockSpec((tl, D_MODEL), rows), mem_spec, mem_spec]
    args = [q, mk, mv]
    body = functools.partial(_attn_body, layer is not None)
    aliases = {}
    if prev is not None:
        in_specs.append(pl.BlockSpec(memory_space=pl.ANY))
        args.append(prev)
        aliases = {len(args) - 1: 0}
        body = functools.partial(_drop_alias_ref, body, len(args) - 1)
    return pl.pallas_call(
        body,
        out_shape=jax.ShapeDtypeStruct((ntok, D_MODEL), BF16),
        grid=(nb, nl),
        in_specs=in_specs,
        out_specs=pl.BlockSpec((tl, D_MODEL), rows),
        input_output_aliases=aliases,
        compiler_params=_cparams(("arbitrary", "arbitrary")),
        name="mem_attention",
    )(*args)


def _router_body(x_ref, g_ref, w_ref, b_ref, xn_ref, ids_ref, gates_ref):
    x = x_ref[...]
    ms = jnp.mean(x * x, axis=-1, keepdims=True)
    xn = (x * lax.rsqrt(ms + EPS)) * g_ref[...]
    xn_ref[...] = xn.astype(xn_ref.dtype)
    xh = xn.astype(BF16)
    xl = (xn - xh.astype(F32)).astype(BF16)
    w = w_ref[...]
    wh = w.astype(BF16)
    wl = (w - wh.astype(F32)).astype(BF16)
    logits = (jnp.dot(xh, wh, preferred_element_type=F32) + jnp.dot(xh, wl, preferred_element_type=F32)
              + jnp.dot(xl, wh, preferred_element_type=F32)) + b_ref[...]
    tm = logits.shape[0]
    col = lax.broadcasted_iota(jnp.int32, (tm, ROUTER_LANES), 1).astype(F32)
    neg = jnp.float32(-jnp.inf)
    first = lambda mask: jnp.min(jnp.where(mask, col, float(ROUTER_LANES)), axis=-1, keepdims=True)
    gl = jnp.where(col < MOE_GROUPS, logits, neg)
    gmax = jnp.max(gl, axis=-1, keepdims=True)
    gidx = first(gl == gmax)
    g_w = 1.0 / jnp.sum(jnp.exp(gl - gmax), axis=-1, keepdims=True)
    lo = MOE_GROUPS + gidx * MOE_EPG
    el = jnp.where((col >= lo) & (col < lo + MOE_EPG), logits, neg)
    m1 = jnp.max(el, axis=-1, keepdims=True)
    i1 = first(el == m1)
    z = jnp.sum(jnp.exp(el - m1), axis=-1, keepdims=True)
    p1 = 1.0 / z
    el2 = jnp.where(col == i1, neg, el)
    m2 = jnp.max(el2, axis=-1, keepdims=True)
    i2 = first(el2 == m2)
    p2 = jnp.exp(m2 - m1) / z
    tot = p1 + p2
    ids = jnp.where(col == 0, i1 - MOE_GROUPS, jnp.where(col == 1, i2 - MOE_GROUPS, 0.0))
    ids_ref[...] = ids.astype(jnp.int32)
    gates_ref[...] = jnp.where(col == 0, g_w * p1 / tot, jnp.where(col == 1, g_w * p2 / tot, 0.0))


def moe_router(x, g, w_group, b_group, w_expert, b_expert, tm=256):
    m, d = x.shape
    npad = ROUTER_LANES - MOE_GROUPS - MOE_EXPERTS
    w = jnp.pad(jnp.concatenate([w_group, w_expert], axis=1).astype(F32), ((0, 0), (0, npad)))
    b = jnp.pad(jnp.concatenate([b_group, b_expert]).astype(F32), (0, npad)).reshape(1, ROUTER_LANES)
    row = lambda n: pl.BlockSpec((tm, n), lambda i: (i, 0))
    xn, ids, gates = pl.pallas_call(
        _router_body,
        out_shape=[jax.ShapeDtypeStruct((m, d), F32),
                   jax.ShapeDtypeStruct((m, ROUTER_LANES), jnp.int32),
                   jax.ShapeDtypeStruct((m, ROUTER_LANES), F32)],
        grid=(m // tm,),
        in_specs=[row(d), pl.BlockSpec((1, d), lambda i: (0, 0)),
                  pl.BlockSpec((d, ROUTER_LANES), lambda i: (0, 0)),
                  pl.BlockSpec((1, ROUTER_LANES), lambda i: (0, 0))],
        out_specs=[row(d), row(ROUTER_LANES), row(ROUTER_LANES)],
        compiler_params=_cparams(("arbitrary",)),
        name="moe_router",
    )(x, g.reshape(1, d).astype(F32), w, b)
    return xn, ids[:, :MOE_TOP_K], gates


def _gather_body(tm, src_ref, cnt_ref, x_hbm, o_ref, buf_ref, sem):
    i = pl.program_id(0)
    n_groups = cnt_ref[i]

    @pl.when(n_groups < tm // GATHER_GROUP)
    def _():
        buf_ref[...] = jnp.zeros_like(buf_ref)

    def copy(r):
        return pltpu.make_async_copy(x_hbm.at[pl.ds(src_ref[i * tm + r], 1), :],
                                     buf_ref.at[pl.ds(r, 1), :], sem)

    def start(g, _):
        for u in range(GATHER_GROUP):
            copy(g * GATHER_GROUP + u).start()
        return 0

    def wait(g, _):
        for u in range(GATHER_GROUP):
            copy(g * GATHER_GROUP + u).wait()
        return 0

    lax.fori_loop(0, n_groups, start, 0)
    lax.fori_loop(0, n_groups, wait, 0)
    o_ref[...] = buf_ref[...].astype(o_ref.dtype)


def moe_gather(x, row_src, tile_groups, n_tiles, tm):
    d = x.shape[1]
    return pl.pallas_call(
        functools.partial(_gather_body, tm),
        out_shape=jax.ShapeDtypeStruct((n_tiles * tm, d), BF16),
        grid_spec=pltpu.PrefetchScalarGridSpec(
            num_scalar_prefetch=2,
            grid=(n_tiles,),
            in_specs=[pl.BlockSpec(memory_space=pl.ANY)],
            out_specs=pl.BlockSpec((tm, d), lambda i, src, cnt: (i, 0)),
            scratch_shapes=[pltpu.VMEM((tm, d), x.dtype), pltpu.SemaphoreType.DMA(())],
        ),
        compiler_params=_cparams(("arbitrary",)),
        name="moe_gather",
    )(row_src, tile_groups, x)


def _expert_tile_loop(tm, ts_ref, tc_ref, src_hbm, dst_hbm, col0, ncol, in_buf, out_buf, sem_in, sem_out, compute):
    e = pl.program_id(0)
    t0 = ts_ref[e]
    n = tc_ref[e]
    n_tiles = dst_hbm.shape[0] // tm
    zero_from = jnp.where(e == MOE_EXPERTS - 1, t0 + n, n_tiles)

    def fetch(t, slot):
        return pltpu.make_async_copy(src_hbm.at[pl.ds((t0 + t) * tm, tm), :], in_buf.at[slot], sem_in.at[slot])

    def put(t, slot):
        return pltpu.make_async_copy(out_buf.at[slot], dst_hbm.at[pl.ds((t0 + t) * tm, tm), pl.ds(col0, ncol)],
                                     sem_out.at[slot])

    @pl.when(n > 0)
    def _():
        fetch(0, 0).start()

    def body(t, _):
        slot = lax.rem(t, 2)

        @pl.when(t + 1 < n)
        def _():
            fetch(t + 1, 1 - slot).start()

        fetch(t, slot).wait()

        @pl.when(t >= 2)
        def _():
            put(t - 2, slot).wait()

        out_buf[slot] = compute(in_buf[slot]).astype(out_buf.dtype)
        put(t, slot).start()
        return 0

    lax.fori_loop(0, n, body, 0)

    @pl.when(n >= 2)
    def _():
        put(n - 2, lax.rem(n, 2)).wait()

    @pl.when(n >= 1)
    def _():
        put(n - 1, lax.rem(n - 1, 2)).wait()

    @pl.when(zero_from < n_tiles)
    def _():
        out_buf[0] = jnp.zeros(out_buf.shape[1:], out_buf.dtype)

        def zstart(t, _):
            put(t - t0, 0).start()
            return 0

        def zwait(t, _):
            put(t - t0, 0).wait()
            return 0

        lax.fori_loop(zero_from, n_tiles, zstart, 0)
        lax.fori_loop(zero_from, n_tiles, zwait, 0)


def _moe_up_body(tm, tf, ts_ref, tc_ref, wg_ref, wu_ref, xs_hbm, h_hbm, wgb_ref, wub_ref, xbuf, hbuf, sem_in, sem_out):
    c = pl.program_id(1)
    wgb_ref[...] = wg_ref[...].astype(BF16)
    wub_ref[...] = wu_ref[...].astype(BF16)

    def compute(x):
        gate = jnp.dot(x, wgb_ref[...], preferred_element_type=F32)
        up = jnp.dot(x, wub_ref[...], preferred_element_type=F32)
        return jax.nn.silu(gate) * up

    _expert_tile_loop(tm, ts_ref, tc_ref, xs_hbm, h_hbm, pl.multiple_of(c * tf, tf), tf,
                      xbuf, hbuf, sem_in, sem_out, compute)


def _moe_down_body(tm, tn, ts_ref, tc_ref, wd_ref, h_hbm, y_hbm, wdb_ref, hbuf, ybuf, sem_in, sem_out):
    c = pl.program_id(1)
    wdb_ref[...] = wd_ref[...].astype(BF16)

    def compute(h):
        return jnp.dot(h, wdb_ref[...], preferred_element_type=F32)

    _expert_tile_loop(tm, ts_ref, tc_ref, h_hbm, y_hbm, pl.multiple_of(c * tn, tn), tn,
                      hbuf, ybuf, sem_in, sem_out, compute)


def moe_experts(xs, tile_start, tile_count, w_gate_up, w_down, layer, tm, tf=512, tnd=2048):
    p, d = xs.shape
    f = MOE_HIDDEN
    nfc = f // tf
    dma2 = pltpu.SemaphoreType.DMA((2,))
    h = pl.pallas_call(
        functools.partial(_moe_up_body, tm, tf),
        out_shape=jax.ShapeDtypeStruct((p, f), BF16),
        grid_spec=pltpu.PrefetchScalarGridSpec(
            num_scalar_prefetch=2,
            grid=(MOE_EXPERTS, nfc),
            in_specs=[pl.BlockSpec((None, None, d, tf), lambda e, c, ts, tc: (layer, e, 0, c)),
                      pl.BlockSpec((None, None, d, tf), lambda e, c, ts, tc: (layer, e, 0, nfc + c)),
                      pl.BlockSpec(memory_space=pl.ANY)],
            out_specs=pl.BlockSpec(memory_space=pl.ANY),
            scratch_shapes=[pltpu.VMEM((d, tf), BF16), pltpu.VMEM((d, tf), BF16),
                            pltpu.VMEM((2, tm, d), BF16), pltpu.VMEM((2, tm, tf), BF16), dma2, dma2],
        ),
        compiler_params=_cparams(("arbitrary", "arbitrary")),
        name="moe_up",
    )(tile_start, tile_count, w_gate_up, w_gate_up, xs)
    return pl.pallas_call(
        functools.partial(_moe_down_body, tm, tnd),
        out_shape=jax.ShapeDtypeStruct((p, d), F32),
        grid_spec=pltpu.PrefetchScalarGridSpec(
            num_scalar_prefetch=2,
            grid=(MOE_EXPERTS, d // tnd),
            in_specs=[pl.BlockSpec((None, None, f, tnd), lambda e, c, ts, tc: (layer, e, 0, c)),
                      pl.BlockSpec(memory_space=pl.ANY)],
            out_specs=pl.BlockSpec(memory_space=pl.ANY),
            scratch_shapes=[pltpu.VMEM((f, tnd), BF16),
                            pltpu.VMEM((2, tm, f), BF16), pltpu.VMEM((2, tm, tnd), F32), dma2, dma2],
        ),
        compiler_params=_cparams(("arbitrary", "arbitrary")),
        name="moe_down",
    )(tile_start, tile_count, w_down, h)


def _combine_body(tt, pos_ref, x_ref, g_ref, y_hbm, o_ref, buf_ref, sem):
    i = pl.program_id(0)

    def copy(t, k):
        return pltpu.make_async_copy(y_hbm.at[pl.ds(pos_ref[(i * tt + t) * MOE_TOP_K + k], 1), :],
                                     buf_ref.at[k, pl.ds(t, 1), :], sem)

    def start(t, _):
        for k in range(MOE_TOP_K):
            copy(t, k).start()
        return 0

    def wait(t, _):
        for k in range(MOE_TOP_K):
            copy(t, k).wait()
        return 0

    lax.fori_loop(0, tt, start, 0)
    lax.fori_loop(0, tt, wait, 0)
    g = g_ref[...]
    y = g[:, 0:1] * buf_ref[0] + g[:, 1:2] * buf_ref[1]
    o_ref[...] = x_ref[...] + y


def moe_combine(x, gates, pos, y_sorted, tt=256):
    m, d = x.shape
    return pl.pallas_call(
        functools.partial(_combine_body, tt),
        out_shape=jax.ShapeDtypeStruct((m, d), F32),
        grid_spec=pltpu.PrefetchScalarGridSpec(
            num_scalar_prefetch=1,
            grid=(m // tt,),
            in_specs=[pl.BlockSpec((tt, d), lambda i, pos: (i, 0)),
                      pl.BlockSpec((tt, ROUTER_LANES), lambda i, pos: (i, 0)),
                      pl.BlockSpec(memory_space=pl.ANY)],
            out_specs=pl.BlockSpec((tt, d), lambda i, pos: (i, 0)),
            scratch_shapes=[pltpu.VMEM((MOE_TOP_K, tt, d), F32), pltpu.SemaphoreType.DMA(())],
        ),
        compiler_params=_cparams(("arbitrary",)),
        name="moe_combine",
    )(pos, x, gates, y_sorted)


def _moe_plan(ids, tm):
    t = ids.shape[0]
    n = t * MOE_TOP_K
    n_tiles = -(-(n + MOE_EXPERTS * (tm - 1)) // tm)
    i32 = jnp.int32
    flat_e = ids.reshape(n)
    order = jnp.argsort(flat_e, stable=True).astype(i32)
    inv = jnp.argsort(order).astype(i32)
    se = flat_e[order]
    experts = jnp.arange(MOE_EXPERTS, dtype=i32)
    starts = jnp.searchsorted(se, experts, side="left").astype(i32)
    counts = jnp.searchsorted(se, experts, side="right").astype(i32) - starts
    tile_count = (counts + tm - 1) // tm
    tile_end = jnp.cumsum(tile_count).astype(i32)
    tile_start = tile_end - tile_count
    pos = tile_start[flat_e] * tm + (inv - starts[flat_e])
    tiles = jnp.arange(n_tiles, dtype=i32)
    tile_e = jnp.minimum(jnp.searchsorted(tile_end, tiles, side="right"), MOE_EXPERTS - 1).astype(i32)
    tile_valid = jnp.clip(counts[tile_e] - (tiles - tile_start[tile_e]) * tm, 0, tm)
    tile_groups = (tile_valid + GATHER_GROUP - 1) // GATHER_GROUP
    within = jnp.arange(tm, dtype=i32)[None, :]
    slot = (starts[tile_e] + (tiles - tile_start[tile_e]) * tm)[:, None] + within
    src = order[jnp.clip(slot, 0, n - 1)] // MOE_TOP_K
    row_src = jnp.where(within < tile_valid[:, None], src, 0).reshape(n_tiles * tm)
    return row_src, pos.astype(i32), tile_start, tile_count, tile_groups.astype(i32), n_tiles


def hier_moe(x, g, layer, w_group, b_group, w_expert, b_expert, w_gate_up, w_down):
    xn, ids, gates = moe_router(x, g, w_group, b_group, w_expert, b_expert)
    row_src, pos, tile_start, tile_count, tile_groups, n_tiles = _moe_plan(ids, MOE_TILE)
    xs = moe_gather(xn, row_src, tile_groups, n_tiles, MOE_TILE)
    y = moe_experts(xs, tile_start, tile_count, w_gate_up, w_down, layer, MOE_TILE)
    return moe_combine(x, gates, pos, y)


def kernel(x_prompt, x_sample, mem_prompt, state_ssm_re, state_ssm_im, state_gla, cache_mem_k, cache_mem_v, norm_mixer, norm_ca, norm_moe, norm_final, ssm_lambda_re, ssm_lambda_im, ssm_log_dt, ssm_b_re, ssm_b_im, ssm_c_re, ssm_c_im, ssm_d, ssm_w_glu, gla_w_in, gla_w_gate_up, gla_b_gate, gla_norm, gla_w_out, ca_mem_norm, ca_w_q, ca_w_kv, ca_w_o, moe_w_group, moe_b_group, moe_w_expert, moe_b_expert, moe_w_gate_up, moe_w_down):
    bp, lp, d = x_prompt.shape
    bs, ls, _ = x_sample.shape
    np_ = bp * lp
    ns_ = bs * ls

    mem = mem_prompt.reshape(bp * N_MEM, d)
    pk, pv = [], []
    for i in range(DEPTH):
        mn = rmsnorm(mem, ca_mem_norm[i])
        kv = matmul(mn, ca_w_kv, i, out_dtype=F32)
        pk.append(kv[:, :d].reshape(bp, N_MEM, d))
        pv.append(kv[:, d:].reshape(bp, N_MEM, d))
    prompt_mem_k = jnp.stack(pk).reshape(DEPTH, bp, N_MEM, CA_HEADS, CA_HEAD_DIM)
    prompt_mem_v = jnp.stack(pv).reshape(DEPTH, bp, N_MEM, CA_HEADS, CA_HEAD_DIM)

    x = jnp.concatenate([x_prompt.reshape(np_, d), x_sample.reshape(ns_, d)], axis=0)
    zero_ssm = jnp.zeros((bp, SSM_GROUPS, SSM_STATE), F32)
    zero_gla = jnp.zeros((bp, GLA_HEADS, GLA_DK_HEAD, GLA_DV_HEAD), F32)
    zero_act = jnp.zeros((np_ + ns_, d), BF16)
    p_re, p_im, s_re, s_im, p_gla, s_gla = [], [], [], [], [], []

    for i in range(DEPTH):
        j = i // 2
        if i % 2 == 0:
            xn = rmsnorm(x, norm_mixer[i], out_dtype=F32)
            prm = (ssm_lambda_re[j], ssm_lambda_im[j], ssm_log_dt[j], ssm_b_re[j], ssm_b_im[j],
                   ssm_c_re[j], ssm_c_im[j], ssm_d[j])
            zp, hr, hi = s5_layer(xn[:np_].reshape(bp, lp, d), zero_ssm, zero_ssm, prm)
            p_re.append(hr)
            p_im.append(hi)
            zs, hr, hi = s5_layer(xn[np_:].reshape(bs, ls, d), state_ssm_re[j], state_ssm_im[j], prm)
            s_re.append(hr)
            s_im.append(hi)
            z = jnp.concatenate([zp, zs], axis=0)
            x = matmul(z, ssm_w_glu, j, mode="glu_res", res=x, n_out=d, tn=256)
        else:
            xn = rmsnorm(x, norm_mixer[i])
            proj = matmul(xn, gla_w_in, j, out_dtype=BF16, n_out=GLA_MAIN)
            la = gla_gate(xn, gla_w_in[j][:, GLA_MAIN:], gla_w_gate_up[j], gla_b_gate[j])
            o, st = gla_recurrence(proj, la, 0, bp, lp, zero_gla, gla_norm[j], prev=zero_act)
            p_gla.append(st)
            o, st = gla_recurrence(proj, la, np_, bs, ls, state_gla[j], gla_norm[j], prev=o)
            s_gla.append(st)
            x = matmul(o, gla_w_out, j, mode="res", res=x)

        xn = rmsnorm(x, norm_ca[i])
        q = matmul(xn, ca_w_q, i, out_dtype=BF16)
        att = mem_attention(q, 0, bp, lp, pk[i], pv[i], prev=zero_act)
        att = mem_attention(q, np_, bs, ls, cache_mem_k, cache_mem_v, layer=i, prev=att)
        x = matmul(att, ca_w_o, i, mode="res", res=x)

        x = hier_moe(x, norm_moe[i], i, moe_w_group[i], moe_b_group[i], moe_w_expert[i], moe_b_expert[i],
                     moe_w_gate_up, moe_w_down)

    y_prompt = rmsnorm(x, norm_final, out_dtype=F32, row0=0, nrows=np_)
    y_sample = rmsnorm(x, norm_final, out_dtype=F32, row0=np_, nrows=ns_)
    return (y_prompt.reshape(bp, lp, d), y_sample.reshape(bs, ls, d),
            jnp.stack(p_re), jnp.stack(p_im), jnp.stack(p_gla), prompt_mem_k, prompt_mem_v,
            jnp.stack(s_re), jnp.stack(s_im), jnp.stack(s_gla))
```

```python
import functools
import math

import jax
import jax.numpy as jnp
import numpy as np
from jax import lax
from jax.experimental import pallas as pl
from jax.experimental.pallas import tpu as pltpu

F32 = jnp.float32
BF16 = jnp.bfloat16

EPS = 1e-6
D_MODEL = 4096
DEPTH = 2

SSM_GROUP_CH = 16
SSM_GROUPS = D_MODEL // SSM_GROUP_CH
SSM_STATE = 64
SSM_SEQS = 32
SSM_BLK_CH = 128
SSM_BLK_GROUPS = SSM_BLK_CH // SSM_GROUP_CH
SSM_NBLK = D_MODEL // SSM_BLK_CH
SSM_BLK_STATE = SSM_BLK_GROUPS * SSM_STATE

GLA_HEADS = 4
GLA_DK = D_MODEL // 2
GLA_DV = D_MODEL
GLA_DK_HEAD = GLA_DK // GLA_HEADS
GLA_DV_HEAD = GLA_DV // GLA_HEADS
GLA_GATE_RANK = 16
GLA_TAU = 16.0
GLA_MAIN = 2 * GLA_DK + 2 * GLA_DV
GLA_CHUNK = 64

N_MEM = 256
CA_HEADS = 4
CA_HEAD_DIM = D_MODEL // CA_HEADS

MOE_GROUPS = 4
MOE_EPG = 8
MOE_EXPERTS = MOE_GROUPS * MOE_EPG
MOE_TOP_K = 2
MOE_HIDDEN = D_MODEL // 4
MOE_TILE = 256
GATHER_GROUP = 8
ROUTER_LANES = 128

SUBLANES = 8
VMEM_LIMIT = 56 * 1024 * 1024


def _cparams(sem):
    return pltpu.CompilerParams(dimension_semantics=sem, vmem_limit_bytes=VMEM_LIMIT)


def _rms_body(n_add, emit_sum, x_ref, *refs):
    adds = refs[:n_add]
    g_ref = refs[n_add]
    outs = refs[n_add + 1:]
    x = x_ref[...].astype(F32)
    for a in adds:
        x = x + a[...].astype(F32)
    ms = jnp.mean(x * x, axis=-1, keepdims=True)
    xn = (x * lax.rsqrt(ms + EPS)) * g_ref[...]
    if emit_sum:
        outs[0][...] = x
        outs[1][...] = xn.astype(outs[1].dtype)
    else:
        outs[0][...] = xn.astype(outs[0].dtype)


def rmsnorm(x, g, adds=(), out_dtype=BF16, emit_sum=False, tm=256, row0=0, nrows=None):
    d = x.shape[1]
    m = x.shape[0] if nrows is None else nrows
    assert m % tm == 0 and row0 % tm == 0
    b0 = row0 // tm
    src = pl.BlockSpec((tm, d), lambda i: (b0 + i, 0))
    row = pl.BlockSpec((tm, d), lambda i: (i, 0))
    out_shape = [jax.ShapeDtypeStruct((m, d), out_dtype)]
    out_specs = [row]
    if emit_sum:
        out_shape.insert(0, jax.ShapeDtypeStruct((m, d), F32))
        out_specs.insert(0, row)
    res = pl.pallas_call(
        functools.partial(_rms_body, len(adds), emit_sum),
        out_shape=out_shape,
        grid=(m // tm,),
        in_specs=[src] * (1 + len(adds)) + [pl.BlockSpec((1, d), lambda i: (0, 0))],
        out_specs=out_specs,
        compiler_params=_cparams(("arbitrary",)),
        name="rmsnorm",
    )(x, *adds, g.reshape(1, d).astype(F32))
    return res if emit_sum else res[0]


def _mm_body(mode, x_ref, *refs):
    if mode == "glu_res":
        w_ref, w2_ref, res_ref, o_ref, wb_ref, wb2_ref = refs
    elif mode == "res":
        w_ref, res_ref, o_ref, wb_ref = refs
    else:
        w_ref, o_ref, wb_ref = refs

    @pl.when(pl.program_id(1) == 0)
    def _():
        wb_ref[...] = w_ref[...].astype(BF16)
        if mode == "glu_res":
            wb2_ref[...] = w2_ref[...].astype(BF16)

    x = x_ref[...]
    acc = jnp.dot(x, wb_ref[...], preferred_element_type=F32)
    if mode == "glu_res":
        gate = jnp.dot(x, wb2_ref[...], preferred_element_type=F32)
        o_ref[...] = res_ref[...] + acc * jax.nn.sigmoid(gate)
    elif mode == "res":
        o_ref[...] = res_ref[...] + acc
    else:
        o_ref[...] = acc.astype(o_ref.dtype)


def matmul(x, w, layer, mode="plain", res=None, out_dtype=F32, n_out=None, tm=512, tn=512):
    m, k = x.shape
    n = n_out if n_out is not None else w.shape[2]
    if m % tm:
        tm = m
    assert m % tm == 0 and n % tn == 0
    nb = n // tn
    in_specs = [pl.BlockSpec((tm, k), lambda j, i: (i, 0)),
                pl.BlockSpec((None, k, tn), lambda j, i: (layer, 0, j))]
    args = [x, w]
    scratch = [pltpu.VMEM((k, tn), BF16)]
    if mode == "glu_res":
        in_specs.append(pl.BlockSpec((None, k, tn), lambda j, i: (layer, 0, nb + j)))
        args.append(w)
        scratch.append(pltpu.VMEM((k, tn), BF16))
    if mode in ("res", "glu_res"):
        in_specs.append(pl.BlockSpec((tm, tn), lambda j, i: (i, j)))
        args.append(res)
        out_dtype = F32
    return pl.pallas_call(
        functools.partial(_mm_body, mode),
        out_shape=jax.ShapeDtypeStruct((m, n), out_dtype),
        grid=(nb, m // tm),
        in_specs=in_specs,
        out_specs=pl.BlockSpec((tm, tn), lambda j, i: (i, j)),
        scratch_shapes=scratch,
        compiler_params=_cparams(("arbitrary", "arbitrary")),
        name="matmul_" + mode,
    )(*args)


def _s5_body(lt, emit_y, u_ref, h0_ref, ar_ref, ai_ref, bd_ref, *refs):
    if emit_y:
        cbd_ref, d_ref, z_ref, ht_ref, h_ref, bu_ref, hs_ref = refs
    else:
        ht_ref, h_ref, bu_ref = refs
    tb = pl.program_id(2)
    ns = SSM_BLK_STATE

    @pl.when(tb == 0)
    def _():
        h_ref[...] = h0_ref[...]

    u = u_ref[...].reshape(lt * SUBLANES, SSM_BLK_CH)
    bu_ref[...] = jnp.dot(u.astype(BF16), bd_ref[...], preferred_element_type=F32)
    ar = jnp.broadcast_to(ar_ref[...], (SUBLANES, ns))
    ai = jnp.broadcast_to(ai_ref[...], (SUBLANES, ns))

    def step(t, carry):
        hr, hi = carry
        r0 = pl.multiple_of(t * SUBLANES, SUBLANES)
        nr = ar * hr - ai * hi + bu_ref[pl.ds(r0, SUBLANES), 0:ns]
        ni = ar * hi + ai * hr + bu_ref[pl.ds(r0, SUBLANES), ns:2 * ns]
        if emit_y:
            hs_ref[pl.ds(r0, SUBLANES), 0:ns] = nr
            hs_ref[pl.ds(r0, SUBLANES), ns:2 * ns] = ni
        return nr, ni

    hr, hi = lax.fori_loop(0, lt, step, (h_ref[:, 0:ns], h_ref[:, ns:2 * ns]), unroll=4)
    h_ref[:, 0:ns] = hr
    h_ref[:, ns:2 * ns] = hi
    ht_ref[:, 0:ns] = hr
    ht_ref[:, ns:2 * ns] = hi
    if emit_y:
        y = jnp.dot(hs_ref[...].astype(BF16), cbd_ref[...], preferred_element_type=F32)
        y = y + d_ref[...] * u
        z_ref[...] = jax.nn.gelu(y, approximate=True).reshape(lt, SUBLANES, SSM_BLK_CH)


def _s5_scan(ut, h0, ar, ai, bd, cbd, dsk, emit_y):
    ltot = ut.shape[0]
    lt = min(ltot, 64)
    assert ltot % lt == 0
    ns2 = 2 * SSM_BLK_STATE
    nsg = SSM_SEQS // SUBLANES
    in_specs = [
        pl.BlockSpec((lt, SUBLANES, SSM_BLK_CH), lambda c, s, t: (t, s, c)),
        pl.BlockSpec((None, SUBLANES, ns2), lambda c, s, t: (c, s, 0)),
        pl.BlockSpec((None, 1, SSM_BLK_STATE), lambda c, s, t: (c, 0, 0)),
        pl.BlockSpec((None, 1, SSM_BLK_STATE), lambda c, s, t: (c, 0, 0)),
        pl.BlockSpec((None, SSM_BLK_CH, ns2), lambda c, s, t: (c, 0, 0)),
    ]
    args = [ut, h0, ar, ai, bd]
    ht_shape = jax.ShapeDtypeStruct((SSM_NBLK, SSM_SEQS, ns2), F32)
    ht_spec = pl.BlockSpec((None, SUBLANES, ns2), lambda c, s, t: (c, s, 0))
    scratch = [pltpu.VMEM((SUBLANES, ns2), F32), pltpu.VMEM((lt * SUBLANES, ns2), F32)]
    if emit_y:
        in_specs += [pl.BlockSpec((None, ns2, SSM_BLK_CH), lambda c, s, t: (c, 0, 0)),
                     pl.BlockSpec((None, 1, SSM_BLK_CH), lambda c, s, t: (c, 0, 0))]
        args += [cbd, dsk]
        out_shape = [jax.ShapeDtypeStruct(ut.shape, F32), ht_shape]
        out_specs = [pl.BlockSpec((lt, SUBLANES, SSM_BLK_CH), lambda c, s, t: (t, s, c)), ht_spec]
        scratch.append(pltpu.VMEM((lt * SUBLANES, ns2), F32))
    else:
        out_shape = [ht_shape]
        out_specs = [ht_spec]
    res = pl.pallas_call(
        functools.partial(_s5_body, lt, emit_y),
        out_shape=out_shape,
        grid=(SSM_NBLK, nsg, ltot // lt),
        in_specs=in_specs,
        out_specs=out_specs,
        scratch_shapes=scratch,
        compiler_params=_cparams(("arbitrary", "arbitrary", "arbitrary")),
        name="s5_scan_y" if emit_y else "s5_scan_state",
    )(*args)
    return (res[0], res[1]) if emit_y else (None, res[0])


def _s5_carry_body(nb, nseg, e_ref, h0_ref, ar_ref, ai_ref, hs_ref, he_ref):
    ns = SSM_BLK_STATE
    ar = ar_ref[...]
    ai = ai_ref[...]
    for b in range(nb):
        hr = h0_ref[b:b + 1, 0:ns]
        hi = h0_ref[b:b + 1, ns:2 * ns]
        for j in range(nseg):
            s = b * nseg + j
            hs_ref[s:s + 1, 0:ns] = hr
            hs_ref[s:s + 1, ns:2 * ns] = hi
            er = e_ref[s:s + 1, 0:ns]
            ei = e_ref[s:s + 1, ns:2 * ns]
            hr, hi = ar * hr - ai * hi + er, ar * hi + ai * hr + ei
        he_ref[b:b + 1, 0:ns] = hr
        he_ref[b:b + 1, ns:2 * ns] = hi


def _s5_carry(e0, h0, ar_seg, ai_seg, nb, nseg):
    ns2 = 2 * SSM_BLK_STATE
    return pl.pallas_call(
        functools.partial(_s5_carry_body, nb, nseg),
        out_shape=[jax.ShapeDtypeStruct((SSM_NBLK, SSM_SEQS, ns2), F32),
                   jax.ShapeDtypeStruct((SSM_NBLK, nb, ns2), F32)],
        grid=(SSM_NBLK,),
        in_specs=[pl.BlockSpec((None, SSM_SEQS, ns2), lambda c: (c, 0, 0)),
                  pl.BlockSpec((None, nb, ns2), lambda c: (c, 0, 0)),
                  pl.BlockSpec((None, 1, SSM_BLK_STATE), lambda c: (c, 0, 0)),
                  pl.BlockSpec((None, 1, SSM_BLK_STATE), lambda c: (c, 0, 0))],
        out_specs=[pl.BlockSpec((None, SSM_SEQS, ns2), lambda c: (c, 0, 0)),
                   pl.BlockSpec((None, nb, ns2), lambda c: (c, 0, 0))],
        compiler_params=_cparams(("arbitrary",)),
        name="s5_carry",
    )(e0, h0, ar_seg, ai_seg)


def _s5_params(lam_re, lam_im, log_dt, b_re, b_im, c_re, c_im, d_skip, seg_len):
    dt = jnp.exp(log_dt.astype(F32))[:, None]
    lr = lam_re.astype(F32)
    li = lam_im.astype(F32)
    mag = jnp.exp(lr * dt)
    ab_re = mag * jnp.cos(li * dt)
    ab_im = mag * jnp.sin(li * dt)
    nr = ab_re - 1.0
    ni = ab_im
    den = lr * lr + li * li
    f_re = (nr * lr + ni * li) / den
    f_im = (ni * lr - nr * li) / den
    br = b_re.astype(F32)
    bi = b_im.astype(F32)
    bb_re = f_re[..., None] * br - f_im[..., None] * bi
    bb_im = f_re[..., None] * bi + f_im[..., None] * br
    mag_s = jnp.exp(lr * dt * seg_len)
    as_re = mag_s * jnp.cos(li * dt * seg_len)
    as_im = mag_s * jnp.sin(li * dt * seg_len)
    eye = jnp.eye(SSM_BLK_GROUPS, dtype=F32)

    def in_proj(w):
        w = w.reshape(SSM_NBLK, SSM_BLK_GROUPS, SSM_STATE, SSM_GROUP_CH).transpose(0, 1, 3, 2)
        return jnp.einsum("bjcp,jk->bjckp", w, eye).reshape(SSM_NBLK, SSM_BLK_CH, SSM_BLK_STATE)

    def out_proj(w):
        w = w.reshape(SSM_NBLK, SSM_BLK_GROUPS, SSM_GROUP_CH, SSM_STATE).transpose(0, 1, 3, 2)
        return jnp.einsum("bjpc,jk->bjpkc", w, eye).reshape(SSM_NBLK, SSM_BLK_STATE, SSM_BLK_CH)

    bd = jnp.concatenate([in_proj(bb_re), in_proj(bb_im)], axis=-1).astype(BF16)
    cbd = jnp.concatenate([out_proj(c_re.astype(F32)), -out_proj(c_im.astype(F32))], axis=1).astype(BF16)
    blk = lambda v: v.reshape(SSM_NBLK, 1, SSM_BLK_STATE)
    return (blk(ab_re), blk(ab_im), blk(as_re), blk(as_im), bd, cbd,
            d_skip.astype(F32).reshape(SSM_NBLK, 1, SSM_BLK_CH))


def _state_to_blocks(h_re, h_im):
    nb = h_re.shape[0]
    f = lambda h: h.astype(F32).reshape(nb, SSM_NBLK, SSM_BLK_STATE).transpose(1, 0, 2)
    return jnp.concatenate([f(h_re), f(h_im)], axis=-1)


def _blocks_to_state(hb):
    nb = hb.shape[1]
    f = lambda h: h.transpose(1, 0, 2).reshape(nb, SSM_GROUPS, SSM_STATE)
    return f(hb[..., :SSM_BLK_STATE]), f(hb[..., SSM_BLK_STATE:])


def s5_layer(u, h0_re, h0_im, prm):
    nb, L, d = u.shape
    nseg = SSM_SEQS // nb
    seg = L // nseg
    lam_re, lam_im, log_dt, b_re, b_im, c_re, c_im, d_skip = prm
    ar, ai, as_re, as_im, bd, cbd, dsk = _s5_params(lam_re, lam_im, log_dt, b_re, b_im, c_re, c_im, d_skip, seg)
    ut = u.reshape(nb * nseg, seg, d).transpose(1, 0, 2)
    h0b = _state_to_blocks(h0_re, h0_im)
    if nseg == 1:
        zt, hT = _s5_scan(ut, h0b, ar, ai, bd, cbd, dsk, True)
    else:
        zero = jnp.zeros((SSM_NBLK, SSM_SEQS, 2 * SSM_BLK_STATE), F32)
        _, e0 = _s5_scan(ut, zero, ar, ai, bd, None, None, False)
        hstart, hT = _s5_carry(e0, h0b, as_re, as_im, nb, nseg)
        zt, _ = _s5_scan(ut, hstart, ar, ai, bd, cbd, dsk, True)
    z = zt.astype(BF16).transpose(1, 0, 2).reshape(nb * L, d)
    hr, hi = _blocks_to_state(hT)
    return z, hr, hi


def _gla_gate_body(x_ref, w1_ref, w2_ref, b_ref, o_ref):
    glr = jnp.dot(x_ref[...], w1_ref[...].astype(BF16), preferred_element_type=F32)
    logits = jnp.dot(glr.astype(BF16), w2_ref[...].astype(BF16), preferred_element_type=F32) + b_ref[...]
    o_ref[...] = jax.nn.log_sigmoid(logits) / GLA_TAU


def gla_gate(xn, w_glr, w_gate_up, b_gate, tm=512):
    m, k = xn.shape
    rp = 128
    w1 = jnp.pad(w_glr.astype(F32), ((0, 0), (0, rp - GLA_GATE_RANK)))
    w2 = jnp.pad(w_gate_up.astype(F32), ((0, rp - GLA_GATE_RANK), (0, 0)))
    return pl.pallas_call(
        _gla_gate_body,
        out_shape=jax.ShapeDtypeStruct((m, GLA_DK), F32),
        grid=(m // tm,),
        in_specs=[pl.BlockSpec((tm, k), lambda i: (i, 0)),
                  pl.BlockSpec((k, rp), lambda i: (0, 0)),
                  pl.BlockSpec((rp, GLA_DK), lambda i: (0, 0)),
                  pl.BlockSpec((1, GLA_DK), lambda i: (0, 0))],
        out_specs=pl.BlockSpec((tm, GLA_DK), lambda i: (i, 0)),
        compiler_params=_cparams(("arbitrary",)),
        name="gla_gate",
    )(xn, w1, w2, b_gate.reshape(1, GLA_DK).astype(F32))


def _gla_body(lc, q_ref, k_ref, v_ref, r_ref, la_ref, s0_ref, g_ref, o_ref, st_ref, s_ref):
    c = pl.program_id(2)

    @pl.when(c == 0)
    def _():
        s_ref[...] = s0_ref[...].astype(F32)

    la = la_ref[...]
    la_hi = la.astype(BF16)
    la_lo = (la - la_hi.astype(F32)).astype(BF16)
    row = lax.broadcasted_iota(jnp.int32, (lc, lc), 0)
    col = lax.broadcasted_iota(jnp.int32, (lc, lc), 1)
    tri = (col <= row).astype(BF16)
    cum = (jnp.dot(tri, la_hi, preferred_element_type=F32)
           + jnp.dot(tri, la_lo, preferred_element_type=F32))
    end = cum[lc - 1:lc, :]
    kd = (k_ref[...].astype(F32) * jnp.exp(end - cum)).astype(BF16)
    ones = jnp.ones((lc, 128), BF16)
    tn = (((0,), (0,)), ((), ()))
    end_col = (lax.dot_general(la_hi, ones, tn, preferred_element_type=F32)
               + lax.dot_general(la_lo, ones, tn, preferred_element_type=F32))
    decay = jnp.exp(end_col[:, 0:1])
    s_new = decay * s_ref[...] + lax.dot_general(kd, v_ref[...].astype(BF16), tn, preferred_element_type=F32)
    s_ref[...] = s_new
    qs = (q_ref[...].astype(F32) * (GLA_DK_HEAD ** -0.5)).astype(BF16)
    o = jnp.dot(qs, s_new.astype(BF16), preferred_element_type=F32)
    o = o * lax.rsqrt(jnp.mean(o * o, axis=-1, keepdims=True) + EPS)
    o = o * g_ref[...]
    o_ref[...] = (o * jax.nn.silu(r_ref[...].astype(F32))).astype(o_ref.dtype)

    @pl.when(c == pl.num_programs(2) - 1)
    def _():
        st_ref[...] = s_new


def _drop_alias_ref(body, n_in, *refs):
    return body(*refs[:n_in], *refs[n_in + 1:])


def gla_recurrence(proj, la, row0, nb, L, s0, norm_g, prev=None):
    ntok = proj.shape[0]
    lc = min(GLA_CHUNK, L)
    nc = L // lc
    base = row0 // lc
    kq = GLA_DK // GLA_DK_HEAD
    kv = 2 * GLA_DK // GLA_DV_HEAD
    kr = kv + GLA_HEADS
    rows = lambda b, h, c: base + b * nc + c
    in_specs = [pl.BlockSpec((lc, GLA_DK_HEAD), lambda b, h, c: (rows(b, h, c), h)),
                pl.BlockSpec((lc, GLA_DK_HEAD), lambda b, h, c: (rows(b, h, c), kq + h)),
                pl.BlockSpec((lc, GLA_DV_HEAD), lambda b, h, c: (rows(b, h, c), kv + h)),
                pl.BlockSpec((lc, GLA_DV_HEAD), lambda b, h, c: (rows(b, h, c), kr + h)),
                pl.BlockSpec((lc, GLA_DK_HEAD), lambda b, h, c: (rows(b, h, c), h)),
                pl.BlockSpec((None, None, GLA_DK_HEAD, GLA_DV_HEAD), lambda b, h, c: (b, h, 0, 0)),
                pl.BlockSpec((1, GLA_DV_HEAD), lambda b, h, c: (0, h))]
    args = [proj, proj, proj, proj, la, s0, norm_g.reshape(1, GLA_DV).astype(F32)]
    body = functools.partial(_gla_body, lc)
    aliases = {}
    if prev is not None:
        in_specs.append(pl.BlockSpec(memory_space=pl.ANY))
        args.append(prev)
        aliases = {len(args) - 1: 0}
        body = functools.partial(_drop_alias_ref, body, len(args) - 1)
    o, st = pl.pallas_call(
        body,
        out_shape=[jax.ShapeDtypeStruct((ntok, GLA_DV), BF16),
                   jax.ShapeDtypeStruct((nb, GLA_HEADS, GLA_DK_HEAD, GLA_DV_HEAD), F32)],
        grid=(nb, GLA_HEADS, nc),
        in_specs=in_specs,
        out_specs=[pl.BlockSpec((lc, GLA_DV_HEAD), lambda b, h, c: (rows(b, h, c), h)),
                   pl.BlockSpec((None, None, GLA_DK_HEAD, GLA_DV_HEAD), lambda b, h, c: (b, h, 0, 0))],
        scratch_shapes=[pltpu.VMEM((GLA_DK_HEAD, GLA_DV_HEAD), F32)],
        input_output_aliases=aliases,
        compiler_params=_cparams(("arbitrary", "arbitrary", "arbitrary")),
        name="gla_recurrence",
    )(*args)
    return o, st


def _attn_body(per_head, q_ref, k_ref, v_ref, o_ref):
    nt = (((1,), (1,)), ((), ()))
    scale = CA_HEAD_DIM ** -0.5
    heads = lambda ref: [ref[:, h * CA_HEAD_DIM:(h + 1) * CA_HEAD_DIM] for h in range(CA_HEADS)]

    def softmax(s):
        p = jnp.exp(s - jnp.max(s, axis=-1, keepdims=True))
        return p / jnp.sum(p, axis=-1, keepdims=True)

    if per_head:
        tl = q_ref.shape[0]
        k2 = k_ref[...].reshape(N_MEM * CA_HEADS, CA_HEAD_DIM).astype(BF16)
        v2 = v_ref[...].reshape(N_MEM * CA_HEADS, CA_HEAD_DIM).astype(BF16)
        q4 = jnp.concatenate(heads(q_ref), axis=0)
        s = lax.dot_general(q4, k2, nt, preferred_element_type=F32) * scale
        q_head = lax.broadcasted_iota(jnp.int32, s.shape, 0) // tl
        m_head = lax.broadcasted_iota(jnp.int32, s.shape, 1) % CA_HEADS
        p = softmax(jnp.where(q_head == m_head, s, -jnp.inf))
        o4 = jnp.dot(p.astype(BF16), v2, preferred_element_type=F32).astype(o_ref.dtype)
        for h in range(CA_HEADS):
            o_ref[:, h * CA_HEAD_DIM:(h + 1) * CA_HEAD_DIM] = o4[h * tl:(h + 1) * tl]
    else:
        for h, (qh, kh, vh) in enumerate(zip(heads(q_ref), heads(k_ref), heads(v_ref))):
            s = lax.dot_general(qh, kh.astype(BF16), nt, preferred_element_type=F32) * scale
            o = jnp.dot(softmax(s).astype(BF16), vh.astype(BF16), preferred_element_type=F32)
            o_ref[:, h * CA_HEAD_DIM:(h + 1) * CA_HEAD_DIM] = o.astype(o_ref.dtype)


def mem_attention(q, row0, nb, L, mk, mv, layer=None, prev=None):
    ntok = q.shape[0]
    tl = min(L, 512)
    nl = L // tl
    base = row0 // tl
    rows = lambda b, i: (base + b * nl + i, 0)
    if layer is None:
        mem_spec = pl.BlockSpec((None, N_MEM, D_MODEL), lambda b, i: (b, 0, 0))
    else:
        mem_spec = pl.BlockSpec((None, None, N_MEM, CA_HEADS, CA_HEAD_DIM), lambda b, i: (layer, b, 0, 0, 0))
    in_specs = [pl.BlockSpec((tl, D_MODEL), rows), mem_spec, mem_spec]
    args = [q, mk, mv]
    body = functools.partial(_attn_body, layer is not None)
    aliases = {}
    if prev is not None:
        in_specs.append(pl.BlockSpec(memory_space=pl.ANY))
        args.append(prev)
        aliases = {len(args) - 1: 0}
        body = functools.partial(_drop_alias_ref, body, len(args) - 1)
    return pl.pallas_call(
        body,
        out_shape=jax.ShapeDtypeStruct((ntok, D_MODEL), BF16),
        grid=(nb, nl),
        in_specs=in_specs,
        out_specs=pl.BlockSpec((tl, D_MODEL), rows),
        input_output_aliases=aliases,
        compiler_params=_cparams(("arbitrary", "arbitrary")),
        name="mem_attention",
    )(*args)


def _router_body(x_ref, g_ref, w_ref, b_ref, xn_ref, ids_ref, gates_ref):
    x = x_ref[...]
    ms = jnp.mean(x * x, axis=-1, keepdims=True)
    xn = (x * lax.rsqrt(ms + EPS)) * g_ref[...]
    xn_ref[...] = xn.astype(xn_ref.dtype)
    logits = jnp.dot(xn.astype(BF16), w_ref[...].astype(BF16), preferred_element_type=F32) + b_ref[...]
    tm = logits.shape[0]
    col = lax.broadcasted_iota(jnp.int32, (tm, ROUTER_LANES), 1).astype(F32)
    neg = jnp.float32(-jnp.inf)
    first = lambda mask: jnp.min(jnp.where(mask, col, float(ROUTER_LANES)), axis=-1, keepdims=True)
    gl = jnp.where(col < MOE_GROUPS, logits, neg)
    gmax = jnp.max(gl, axis=-1, keepdims=True)
    gidx = first(gl == gmax)
    g_w = 1.0 / jnp.sum(jnp.exp(gl - gmax), axis=-1, keepdims=True)
    lo = MOE_GROUPS + gidx * MOE_EPG
    el = jnp.where((col >= lo) & (col < lo + MOE_EPG), logits, neg)
    m1 = jnp.max(el, axis=-1, keepdims=True)
    i1 = first(el == m1)
    z = jnp.sum(jnp.exp(el - m1), axis=-1, keepdims=True)
    p1 = 1.0 / z
    el2 = jnp.where(col == i1, neg, el)
    m2 = jnp.max(el2, axis=-1, keepdims=True)
    i2 = first(el2 == m2)
    p2 = jnp.exp(m2 - m1) / z
    tot = p1 + p2
    ids = jnp.where(col == 0, i1 - MOE_GROUPS, jnp.where(col == 1, i2 - MOE_GROUPS, 0.0))
    ids_ref[...] = ids.astype(jnp.int32)
    gates_ref[...] = jnp.where(col == 0, g_w * p1 / tot, jnp.where(col == 1, g_w * p2 / tot, 0.0))


def moe_router(x, g, w_group, b_group, w_expert, b_expert, tm=256):
    m, d = x.shape
    npad = ROUTER_LANES - MOE_GROUPS - MOE_EXPERTS
    w = jnp.pad(jnp.concatenate([w_group, w_expert], axis=1).astype(F32), ((0, 0), (0, npad)))
    b = jnp.pad(jnp.concatenate([b_group, b_expert]).astype(F32), (0, npad)).reshape(1, ROUTER_LANES)
    row = lambda n: pl.BlockSpec((tm, n), lambda i: (i, 0))
    xn, ids, gates = pl.pallas_call(
        _router_body,
        out_shape=[jax.ShapeDtypeStruct((m, d), F32),
                   jax.ShapeDtypeStruct((m, ROUTER_LANES), jnp.int32),
                   jax.ShapeDtypeStruct((m, ROUTER_LANES), F32)],
        grid=(m // tm,),
        in_specs=[row(d), pl.BlockSpec((1, d), lambda i: (0, 0)),
                  pl.BlockSpec((d, ROUTER_LANES), lambda i: (0, 0)),
                  pl.BlockSpec((1, ROUTER_LANES), lambda i: (0, 0))],
        out_specs=[row(d), row(ROUTER_LANES), row(ROUTER_LANES)],
        compiler_params=_cparams(("arbitrary",)),
        name="moe_router",
    )(x, g.reshape(1, d).astype(F32), w, b)
    return xn, ids[:, :MOE_TOP_K], gates


def _gather_body(tm, src_ref, cnt_ref, x_hbm, o_ref, buf_ref, sem):
    i = pl.program_id(0)
    n_groups = cnt_ref[i]

    @pl.when(n_groups < tm // GATHER_GROUP)
    def _():
        buf_ref[...] = jnp.zeros_like(buf_ref)

    def copy(r):
        return pltpu.make_async_copy(x_hbm.at[pl.ds(src_ref[i * tm + r], 1), :],
                                     buf_ref.at[pl.ds(r, 1), :], sem)

    def start(g, _):
        for u in range(GATHER_GROUP):
            copy(g * GATHER_GROUP + u).start()
        return 0

    def wait(g, _):
        for u in range(GATHER_GROUP):
            copy(g * GATHER_GROUP + u).wait()
        return 0

    lax.fori_loop(0, n_groups, start, 0)
    lax.fori_loop(0, n_groups, wait, 0)
    o_ref[...] = buf_ref[...].astype(o_ref.dtype)


def moe_gather(x, row_src, tile_groups, n_tiles, tm):
    d = x.shape[1]
    return pl.pallas_call(
        functools.partial(_gather_body, tm),
        out_shape=jax.ShapeDtypeStruct((n_tiles * tm, d), BF16),
        grid_spec=pltpu.PrefetchScalarGridSpec(
            num_scalar_prefetch=2,
            grid=(n_tiles,),
            in_specs=[pl.BlockSpec(memory_space=pl.ANY)],
            out_specs=pl.BlockSpec((tm, d), lambda i, src, cnt: (i, 0)),
            scratch_shapes=[pltpu.VMEM((tm, d), x.dtype), pltpu.SemaphoreType.DMA(())],
        ),
        compiler_params=_cparams(("arbitrary",)),
        name="moe_gather",
    )(row_src, tile_groups, x)


def _segment_weights(te_ref, seg_ref, meta_ref, copies, cast):
    c = pl.program_id(0)
    r = pl.program_id(1)
    nt_used = meta_ref[0]
    nseg = meta_ref[1]
    used = r < nt_used
    seg = seg_ref[MOE_EXPERTS + r]
    first = jnp.logical_and(used, jnp.logical_or(r == 0, te_ref[r] != te_ref[jnp.maximum(r - 1, 0)]))
    g = c * nseg + seg
    slot = lax.rem(g, 2)

    @pl.when(first)
    def _():
        @pl.when(g == 0)
        def _():
            for cp in copies(te_ref[r], c, 0):
                cp.start()

        for cp in copies(te_ref[r], c, slot):
            cp.wait()
        wraps = seg + 1 >= nseg
        nxt_seg = jnp.where(wraps, 0, seg + 1)
        nxt_c = jnp.where(wraps, c + 1, c)

        @pl.when(nxt_c < pl.num_programs(0))
        def _():
            for cp in copies(seg_ref[nxt_seg], nxt_c, 1 - slot):
                cp.start()

        cast(slot)

    return used


def _moe_up_body(layer, tf, te_ref, seg_ref, meta_ref, x_ref, w_hbm, h_ref, wbuf, wgb_ref, wub_ref, sem):
    f = MOE_HIDDEN

    def copies(e, c, slot):
        col = pl.multiple_of(c * tf, tf)
        return [pltpu.make_async_copy(w_hbm.at[layer, e, :, pl.ds(half * f + col, tf)], wbuf.at[slot, half],
                                      sem.at[slot, half]) for half in range(2)]

    def cast(slot):
        wgb_ref[...] = wbuf[slot, 0].astype(BF16)
        wub_ref[...] = wbuf[slot, 1].astype(BF16)

    used = _segment_weights(te_ref, seg_ref, meta_ref, copies, cast)

    @pl.when(used)
    def _():
        x = x_ref[...]
        gate = jnp.dot(x, wgb_ref[...], preferred_element_type=F32)
        up = jnp.dot(x, wub_ref[...], preferred_element_type=F32)
        h_ref[...] = (jax.nn.silu(gate) * up).astype(h_ref.dtype)

    @pl.when(jnp.logical_not(used))
    def _():
        h_ref[...] = jnp.zeros_like(h_ref)


def _moe_down_body(layer, tn, te_ref, seg_ref, meta_ref, h_ref, w_hbm, y_ref, wbuf, wdb_ref, sem):
    def copies(e, c, slot):
        col = pl.multiple_of(c * tn, tn)
        return [pltpu.make_async_copy(w_hbm.at[layer, e, :, pl.ds(col, tn)], wbuf.at[slot], sem.at[slot])]

    def cast(slot):
        wdb_ref[...] = wbuf[slot].astype(BF16)

    used = _segment_weights(te_ref, seg_ref, meta_ref, copies, cast)

    @pl.when(used)
    def _():
        y_ref[...] = jnp.dot(h_ref[...], wdb_ref[...], preferred_element_type=F32)

    @pl.when(jnp.logical_not(used))
    def _():
        y_ref[...] = jnp.zeros_like(y_ref)


def moe_experts(xs, tile_e, seg_info, meta, w_gate_up, w_down, layer, tm, tf=512, tnd=D_MODEL):
    p, d = xs.shape
    n_tiles = p // tm
    f = MOE_HIDDEN
    h = pl.pallas_call(
        functools.partial(_moe_up_body, layer, tf),
        out_shape=jax.ShapeDtypeStruct((p, f), BF16),
        grid_spec=pltpu.PrefetchScalarGridSpec(
            num_scalar_prefetch=3,
            grid=(f // tf, n_tiles),
            in_specs=[pl.BlockSpec((tm, d), lambda c, r, te, sg, mt: (r, 0)),
                      pl.BlockSpec(memory_space=pl.ANY)],
            out_specs=pl.BlockSpec((tm, tf), lambda c, r, te, sg, mt: (r, c)),
            scratch_shapes=[pltpu.VMEM((2, 2, d, tf), F32), pltpu.VMEM((d, tf), BF16), pltpu.VMEM((d, tf), BF16),
                            pltpu.SemaphoreType.DMA((2, 2))],
        ),
        compiler_params=_cparams(("arbitrary", "arbitrary")),
        name="moe_up",
    )(tile_e, seg_info, meta, xs, w_gate_up)
    return pl.pallas_call(
        functools.partial(_moe_down_body, layer, tnd),
        out_shape=jax.ShapeDtypeStruct((p, d), F32),
        grid_spec=pltpu.PrefetchScalarGridSpec(
            num_scalar_prefetch=3,
            grid=(d // tnd, n_tiles),
            in_specs=[pl.BlockSpec((tm, f), lambda c, r, te, sg, mt: (r, 0)),
                      pl.BlockSpec(memory_space=pl.ANY)],
            out_specs=pl.BlockSpec((tm, tnd), lambda c, r, te, sg, mt: (r, c)),
            scratch_shapes=[pltpu.VMEM((2, f, tnd), F32), pltpu.VMEM((f, tnd), BF16),
                            pltpu.SemaphoreType.DMA((2,))],
        ),
        compiler_params=_cparams(("arbitrary", "arbitrary")),
        name="moe_down",
    )(tile_e, seg_info, meta, h, w_down)


def _combine_body(tt, pos_ref, x_ref, g_ref, y_hbm, o_ref, buf_ref, sem):
    i = pl.program_id(0)

    def copy(t, k):
        return pltpu.make_async_copy(y_hbm.at[pl.ds(pos_ref[(i * tt + t) * MOE_TOP_K + k], 1), :],
                                     buf_ref.at[k, pl.ds(t, 1), :], sem)

    def start(t, _):
        for k in range(MOE_TOP_K):
            copy(t, k).start()
        return 0

    def wait(t, _):
        for k in range(MOE_TOP_K):
            copy(t, k).wait()
        return 0

    lax.fori_loop(0, tt, start, 0)
    lax.fori_loop(0, tt, wait, 0)
    g = g_ref[...]
    y = g[:, 0:1] * buf_ref[0] + g[:, 1:2] * buf_ref[1]
    o_ref[...] = x_ref[...] + y


def moe_combine(x, gates, pos, y_sorted, tt=256):
    m, d = x.shape
    return pl.pallas_call(
        functools.partial(_combine_body, tt),
        out_shape=jax.ShapeDtypeStruct((m, d), F32),
        grid_spec=pltpu.PrefetchScalarGridSpec(
            num_scalar_prefetch=1,
            grid=(m // tt,),
            in_specs=[pl.BlockSpec((tt, d), lambda i, pos: (i, 0)),
                      pl.BlockSpec((tt, ROUTER_LANES), lambda i, pos: (i, 0)),
                      pl.BlockSpec(memory_space=pl.ANY)],
            out_specs=pl.BlockSpec((tt, d), lambda i, pos: (i, 0)),
            scratch_shapes=[pltpu.VMEM((MOE_TOP_K, tt, d), F32), pltpu.SemaphoreType.DMA(())],
        ),
        compiler_params=_cparams(("arbitrary",)),
        name="moe_combine",
    )(pos, x, gates, y_sorted)


def _moe_plan(ids, tm):
    t = ids.shape[0]
    n = t * MOE_TOP_K
    n_tiles = -(-(n + MOE_EXPERTS * (tm - 1)) // tm)
    i32 = jnp.int32
    flat_e = ids.reshape(n)
    order = jnp.argsort(flat_e, stable=True).astype(i32)
    inv = jnp.argsort(order).astype(i32)
    se = flat_e[order]
    experts = jnp.arange(MOE_EXPERTS, dtype=i32)
    starts = jnp.searchsorted(se, experts, side="left").astype(i32)
    counts = jnp.searchsorted(se, experts, side="right").astype(i32) - starts
    tile_count = (counts + tm - 1) // tm
    tile_end = jnp.cumsum(tile_count).astype(i32)
    tile_start = tile_end - tile_count
    pos = tile_start[flat_e] * tm + (inv - starts[flat_e])
    tiles = jnp.arange(n_tiles, dtype=i32)
    tile_e = jnp.minimum(jnp.searchsorted(tile_end, tiles, side="right"), MOE_EXPERTS - 1).astype(i32)
    tile_valid = jnp.clip(counts[tile_e] - (tiles - tile_start[tile_e]) * tm, 0, tm)
    tile_groups = (tile_valid + GATHER_GROUP - 1) // GATHER_GROUP
    within = jnp.arange(tm, dtype=i32)[None, :]
    slot = (starts[tile_e] + (tiles - tile_start[tile_e]) * tm)[:, None] + within
    src = order[jnp.clip(slot, 0, n - 1)] // MOE_TOP_K
    row_src = jnp.where(within < tile_valid[:, None], src, 0).reshape(n_tiles * tm)
    nonempty = tile_count > 0
    seg_of_expert = jnp.cumsum(nonempty.astype(i32)).astype(i32) - 1
    seg_expert = jnp.argsort(jnp.where(nonempty, 0, 1).astype(i32), stable=True).astype(i32)
    seg_info = jnp.concatenate([seg_expert, seg_of_expert[tile_e]])
    meta = jnp.stack([tile_end[-1], jnp.sum(nonempty.astype(i32))]).astype(i32)
    return row_src, pos.astype(i32), tile_e, seg_info, meta, tile_groups.astype(i32), n_tiles


def hier_moe(x, g, layer, w_group, b_group, w_expert, b_expert, w_gate_up, w_down):
    xn, ids, gates = moe_router(x, g, w_group, b_group, w_expert, b_expert)
    row_src, pos, tile_e, seg_info, meta, tile_groups, n_tiles = _moe_plan(ids, MOE_TILE)
    xs = moe_gather(xn, row_src, tile_groups, n_tiles, MOE_TILE)
    y = moe_experts(xs, tile_e, seg_info, meta, w_gate_up, w_down, layer, MOE_TILE)
    return moe_combine(x, gates, pos, y)


def kernel(x_prompt, x_sample, mem_prompt, state_ssm_re, state_ssm_im, state_gla, cache_mem_k, cache_mem_v, norm_mixer, norm_ca, norm_moe, norm_final, ssm_lambda_re, ssm_lambda_im, ssm_log_dt, ssm_b_re, ssm_b_im, ssm_c_re, ssm_c_im, ssm_d, ssm_w_glu, gla_w_in, gla_w_gate_up, gla_b_gate, gla_norm, gla_w_out, ca_mem_norm, ca_w_q, ca_w_kv, ca_w_o, moe_w_group, moe_b_group, moe_w_expert, moe_b_expert, moe_w_gate_up, moe_w_down):
    bp, lp, d = x_prompt.shape
    bs, ls, _ = x_sample.shape
    np_ = bp * lp
    ns_ = bs * ls

    mem = mem_prompt.reshape(bp * N_MEM, d)
    pk, pv = [], []
    for i in range(DEPTH):
        mn = rmsnorm(mem, ca_mem_norm[i])
        kv = matmul(mn, ca_w_kv, i, out_dtype=F32)
        pk.append(kv[:, :d].reshape(bp, N_MEM, d))
        pv.append(kv[:, d:].reshape(bp, N_MEM, d))
    prompt_mem_k = jnp.stack(pk).reshape(DEPTH, bp, N_MEM, CA_HEADS, CA_HEAD_DIM)
    prompt_mem_v = jnp.stack(pv).reshape(DEPTH, bp, N_MEM, CA_HEADS, CA_HEAD_DIM)

    x = jnp.concatenate([x_prompt.reshape(np_, d), x_sample.reshape(ns_, d)], axis=0)
    zero_ssm = jnp.zeros((bp, SSM_GROUPS, SSM_STATE), F32)
    zero_gla = jnp.zeros((bp, GLA_HEADS, GLA_DK_HEAD, GLA_DV_HEAD), F32)
    zero_act = jnp.zeros((np_ + ns_, d), BF16)
    p_re, p_im, s_re, s_im, p_gla, s_gla = [], [], [], [], [], []

    for i in range(DEPTH):
        j = i // 2
        if i % 2 == 0:
            xn = rmsnorm(x, norm_mixer[i], out_dtype=F32)
            prm = (ssm_lambda_re[j], ssm_lambda_im[j], ssm_log_dt[j], ssm_b_re[j], ssm_b_im[j],
                   ssm_c_re[j], ssm_c_im[j], ssm_d[j])
            zp, hr, hi = s5_layer(xn[:np_].reshape(bp, lp, d), zero_ssm, zero_ssm, prm)
            p_re.append(hr)
            p_im.append(hi)
            zs, hr, hi = s5_layer(xn[np_:].reshape(bs, ls, d), state_ssm_re[j], state_ssm_im[j], prm)
            s_re.append(hr)
            s_im.append(hi)
            z = jnp.concatenate([zp, zs], axis=0)
            x = matmul(z, ssm_w_glu, j, mode="glu_res", res=x, n_out=d, tn=256)
        else:
            xn = rmsnorm(x, norm_mixer[i])
            proj = matmul(xn, gla_w_in, j, out_dtype=F32, n_out=GLA_MAIN)
            la = gla_gate(xn, gla_w_in[j][:, GLA_MAIN:], gla_w_gate_up[j], gla_b_gate[j])
            o, st = gla_recurrence(proj, la, 0, bp, lp, zero_gla, gla_norm[j], prev=zero_act)
            p_gla.append(st)
            o, st = gla_recurrence(proj, la, np_, bs, ls, state_gla[j], gla_norm[j], prev=o)
            s_gla.append(st)
            x = matmul(o, gla_w_out, j, mode="res", res=x)

        xn = rmsnorm(x, norm_ca[i])
        q = matmul(xn, ca_w_q, i, out_dtype=BF16)
        att = mem_attention(q, 0, bp, lp, pk[i], pv[i], prev=zero_act)
        att = mem_attention(q, np_, bs, ls, cache_mem_k, cache_mem_v, layer=i, prev=att)
        x = matmul(att, ca_w_o, i, mode="res", res=x)

        x = hier_moe(x, norm_moe[i], i, moe_w_group[i], moe_b_group[i], moe_w_expert[i], moe_b_expert[i],
                     moe_w_gate_up, moe_w_down)

    y_prompt = rmsnorm(x, norm_final, out_dtype=F32, row0=0, nrows=np_)
    y_sample = rmsnorm(x, norm_final, out_dtype=F32, row0=np_, nrows=ns_)
    return (y_prompt.reshape(bp, lp, d), y_sample.reshape(bs, ls, d),
            jnp.stack(p_re), jnp.stack(p_im), jnp.stack(p_gla), prompt_mem_k, prompt_mem_v,
            jnp.stack(s_re), jnp.stack(s_im), jnp.stack(s_gla))
```

```python
import functools
import math

import jax
import jax.numpy as jnp
import numpy as np
from jax import lax
from jax.experimental import pallas as pl
from jax.experimental.pallas import tpu as pltpu

F32 = jnp.float32
BF16 = jnp.bfloat16

EPS = 1e-6
D_MODEL = 4096
DEPTH = 2

SSM_GROUP_CH = 16
SSM_GROUPS = D_MODEL // SSM_GROUP_CH
SSM_STATE = 64
SSM_SEQS = 32
SSM_BLK_CH = 128
SSM_BLK_GROUPS = SSM_BLK_CH // SSM_GROUP_CH
SSM_NBLK = D_MODEL // SSM_BLK_CH
SSM_BLK_STATE = SSM_BLK_GROUPS * SSM_STATE

GLA_HEADS = 4
GLA_DK = D_MODEL // 2
GLA_DV = D_MODEL
GLA_DK_HEAD = GLA_DK // GLA_HEADS
GLA_DV_HEAD = GLA_DV // GLA_HEADS
GLA_GATE_RANK = 16
GLA_TAU = 16.0
GLA_MAIN = 2 * GLA_DK + 2 * GLA_DV
GLA_CHUNK = 64

N_MEM = 256
CA_HEADS = 4
CA_HEAD_DIM = D_MODEL // CA_HEADS

MOE_GROUPS = 4
MOE_EPG = 8
MOE_EXPERTS = MOE_GROUPS * MOE_EPG
MOE_TOP_K = 2
MOE_HIDDEN = D_MODEL // 4
MOE_TILE = 256
GATHER_GROUP = 8
DMA_PRIORITIES = 2
ROUTER_LANES = 128

DENSE_TM = 1088
SUBLANES = 8
VMEM_LIMIT = 56 * 1024 * 1024


def _cparams(sem):
    return pltpu.CompilerParams(dimension_semantics=sem, vmem_limit_bytes=VMEM_LIMIT)


def _rms_body(n_add, emit_sum, x_ref, *refs):
    adds = refs[:n_add]
    g_ref = refs[n_add]
    outs = refs[n_add + 1:]
    x = x_ref[...].astype(F32)
    for a in adds:
        x = x + a[...].astype(F32)
    ms = jnp.mean(x * x, axis=-1, keepdims=True)
    xn = (x * lax.rsqrt(ms + EPS)) * g_ref[...]
    if emit_sum:
        outs[0][...] = x
        outs[1][...] = xn.astype(outs[1].dtype)
    else:
        outs[0][...] = xn.astype(outs[0].dtype)


def rmsnorm(x, g, adds=(), out_dtype=BF16, emit_sum=False, tm=256, row0=0, nrows=None):
    d = x.shape[1]
    m = x.shape[0] if nrows is None else nrows
    assert m % tm == 0 and row0 % tm == 0
    b0 = row0 // tm
    src = pl.BlockSpec((tm, d), lambda i: (b0 + i, 0))
    row = pl.BlockSpec((tm, d), lambda i: (i, 0))
    out_shape = [jax.ShapeDtypeStruct((m, d), out_dtype)]
    out_specs = [row]
    if emit_sum:
        out_shape.insert(0, jax.ShapeDtypeStruct((m, d), F32))
        out_specs.insert(0, row)
    res = pl.pallas_call(
        functools.partial(_rms_body, len(adds), emit_sum),
        out_shape=out_shape,
        grid=(m // tm,),
        in_specs=[src] * (1 + len(adds)) + [pl.BlockSpec((1, d), lambda i: (0, 0))],
        out_specs=out_specs,
        compiler_params=_cparams(("arbitrary",)),
        name="rmsnorm",
    )(x, *adds, g.reshape(1, d).astype(F32))
    return res if emit_sum else res[0]


def _mm_body(mode, w_is_nk, x_ref, *refs):
    if mode == "glu_res":
        w_ref, w2_ref, res_ref, o_ref, wb_ref, wb2_ref = refs
    elif mode == "res":
        w_ref, res_ref, o_ref, wb_ref = refs
    else:
        w_ref, o_ref, wb_ref = refs

    @pl.when(pl.program_id(1) == 0)
    def _():
        wb_ref[...] = w_ref[...].astype(BF16)
        if mode == "glu_res":
            wb2_ref[...] = w2_ref[...].astype(BF16)

    x = x_ref[...]
    if w_is_nk:
        acc = lax.dot_general(x, wb_ref[...], (((1,), (1,)), ((), ())), preferred_element_type=F32)
    else:
        acc = jnp.dot(x, wb_ref[...], preferred_element_type=F32)
    if mode == "glu_res":
        gate = jnp.dot(x, wb2_ref[...], preferred_element_type=F32)
        o_ref[...] = res_ref[...] + acc * jax.nn.sigmoid(gate)
    elif mode == "res":
        o_ref[...] = res_ref[...] + acc
    else:
        o_ref[...] = acc.astype(o_ref.dtype)


def matmul(x, w, layer, mode="plain", res=None, out_dtype=F32, n_out=None, tm=DENSE_TM, tn=512, w_is_nk=False):
    m, k = x.shape
    n = n_out if n_out is not None else w.shape[1 if w_is_nk else 2]
    if m % tm:
        tm = m
    assert m % tm == 0 and n % tn == 0 and not (w_is_nk and mode != "plain")
    nb = n // tn
    if w_is_nk:
        w_spec, w_tile = pl.BlockSpec((None, tn, k), lambda j, i: (layer, j, 0)), (tn, k)
    else:
        w_spec, w_tile = pl.BlockSpec((None, k, tn), lambda j, i: (layer, 0, j)), (k, tn)
    in_specs = [pl.BlockSpec((tm, k), lambda j, i: (i, 0)), w_spec]
    args = [x, w]
    scratch = [pltpu.VMEM(w_tile, BF16)]
    if mode == "glu_res":
        in_specs.append(pl.BlockSpec((None, k, tn), lambda j, i: (layer, 0, nb + j)))
        args.append(w)
        scratch.append(pltpu.VMEM((k, tn), BF16))
    if mode in ("res", "glu_res"):
        in_specs.append(pl.BlockSpec((tm, tn), lambda j, i: (i, j)))
        args.append(res)
        out_dtype = F32
    return pl.pallas_call(
        functools.partial(_mm_body, mode, w_is_nk),
        out_shape=jax.ShapeDtypeStruct((m, n), out_dtype),
        grid=(nb, m // tm),
        in_specs=in_specs,
        out_specs=pl.BlockSpec((tm, tn), lambda j, i: (i, j)),
        scratch_shapes=scratch,
        compiler_params=_cparams(("arbitrary", "arbitrary")),
        name="matmul_" + mode,
    )(*args)


def _s5_body(lt, emit_y, u_ref, h0_ref, ar_ref, ai_ref, bd_ref, *refs):
    if emit_y:
        cbd_ref, d_ref, z_ref, ht_ref, h_ref, bu_ref, hs_ref = refs
    else:
        ht_ref, h_ref, bu_ref = refs
    tb = pl.program_id(2)
    ns = SSM_BLK_STATE

    @pl.when(tb == 0)
    def _():
        h_ref[...] = h0_ref[...]

    u = u_ref[...].reshape(lt * SUBLANES, SSM_BLK_CH)
    bu_ref[...] = jnp.dot(u.astype(BF16), bd_ref[...], preferred_element_type=F32)
    ar = jnp.broadcast_to(ar_ref[...], (SUBLANES, ns))
    ai = jnp.broadcast_to(ai_ref[...], (SUBLANES, ns))

    def step(t, carry):
        hr, hi = carry
        r0 = pl.multiple_of(t * SUBLANES, SUBLANES)
        nr = ar * hr - ai * hi + bu_ref[pl.ds(r0, SUBLANES), 0:ns]
        ni = ar * hi + ai * hr + bu_ref[pl.ds(r0, SUBLANES), ns:2 * ns]
        if emit_y:
            hs_ref[pl.ds(r0, SUBLANES), 0:ns] = nr
            hs_ref[pl.ds(r0, SUBLANES), ns:2 * ns] = ni
        return nr, ni

    hr, hi = lax.fori_loop(0, lt, step, (h_ref[:, 0:ns], h_ref[:, ns:2 * ns]), unroll=4)
    h_ref[:, 0:ns] = hr
    h_ref[:, ns:2 * ns] = hi
    ht_ref[:, 0:ns] = hr
    ht_ref[:, ns:2 * ns] = hi
    if emit_y:
        y = jnp.dot(hs_ref[...].astype(BF16), cbd_ref[...], preferred_element_type=F32)
        y = y + d_ref[...] * u
        z_ref[...] = jax.nn.gelu(y, approximate=True).reshape(lt, SUBLANES, SSM_BLK_CH)


def _s5_scan(ut, h0, ar, ai, bd, cbd, dsk, emit_y):
    ltot = ut.shape[0]
    lt = min(ltot, 64)
    assert ltot % lt == 0
    ns2 = 2 * SSM_BLK_STATE
    nsg = SSM_SEQS // SUBLANES
    in_specs = [
        pl.BlockSpec((lt, SUBLANES, SSM_BLK_CH), lambda c, s, t: (t, s, c)),
        pl.BlockSpec((None, SUBLANES, ns2), lambda c, s, t: (c, s, 0)),
        pl.BlockSpec((None, 1, SSM_BLK_STATE), lambda c, s, t: (c, 0, 0)),
        pl.BlockSpec((None, 1, SSM_BLK_STATE), lambda c, s, t: (c, 0, 0)),
        pl.BlockSpec((None, SSM_BLK_CH, ns2), lambda c, s, t: (c, 0, 0)),
    ]
    args = [ut, h0, ar, ai, bd]
    ht_shape = jax.ShapeDtypeStruct((SSM_NBLK, SSM_SEQS, ns2), F32)
    ht_spec = pl.BlockSpec((None, SUBLANES, ns2), lambda c, s, t: (c, s, 0))
    scratch = [pltpu.VMEM((SUBLANES, ns2), F32), pltpu.VMEM((lt * SUBLANES, ns2), F32)]
    if emit_y:
        in_specs += [pl.BlockSpec((None, ns2, SSM_BLK_CH), lambda c, s, t: (c, 0, 0)),
                     pl.BlockSpec((None, 1, SSM_BLK_CH), lambda c, s, t: (c, 0, 0))]
        args += [cbd, dsk]
        out_shape = [jax.ShapeDtypeStruct(ut.shape, F32), ht_shape]
        out_specs = [pl.BlockSpec((lt, SUBLANES, SSM_BLK_CH), lambda c, s, t: (t, s, c)), ht_spec]
        scratch.append(pltpu.VMEM((lt * SUBLANES, ns2), F32))
    else:
        out_shape = [ht_shape]
        out_specs = [ht_spec]
    res = pl.pallas_call(
        functools.partial(_s5_body, lt, emit_y),
        out_shape=out_shape,
        grid=(SSM_NBLK, nsg, ltot // lt),
        in_specs=in_specs,
        out_specs=out_specs,
        scratch_shapes=scratch,
        compiler_params=_cparams(("arbitrary", "arbitrary", "arbitrary")),
        name="s5_scan_y" if emit_y else "s5_scan_state",
    )(*args)
    return (res[0], res[1]) if emit_y else (None, res[0])


def _s5_carry_body(nb, nseg, e_ref, h0_ref, ar_ref, ai_ref, hs_ref, he_ref):
    ns = SSM_BLK_STATE
    ar = ar_ref[...]
    ai = ai_ref[...]
    for b in range(nb):
        hr = h0_ref[b:b + 1, 0:ns]
        hi = h0_ref[b:b + 1, ns:2 * ns]
        for j in range(nseg):
            s = b * nseg + j
            hs_ref[s:s + 1, 0:ns] = hr
            hs_ref[s:s + 1, ns:2 * ns] = hi
            er = e_ref[s:s + 1, 0:ns]
            ei = e_ref[s:s + 1, ns:2 * ns]
            hr, hi = ar * hr - ai * hi + er, ar * hi + ai * hr + ei
        he_ref[b:b + 1, 0:ns] = hr
        he_ref[b:b + 1, ns:2 * ns] = hi


def _s5_carry(e0, h0, ar_seg, ai_seg, nb, nseg):
    ns2 = 2 * SSM_BLK_STATE
    return pl.pallas_call(
        functools.partial(_s5_carry_body, nb, nseg),
        out_shape=[jax.ShapeDtypeStruct((SSM_NBLK, SSM_SEQS, ns2), F32),
                   jax.ShapeDtypeStruct((SSM_NBLK, nb, ns2), F32)],
        grid=(SSM_NBLK,),
        in_specs=[pl.BlockSpec((None, SSM_SEQS, ns2), lambda c: (c, 0, 0)),
                  pl.BlockSpec((None, nb, ns2), lambda c: (c, 0, 0)),
                  pl.BlockSpec((None, 1, SSM_BLK_STATE), lambda c: (c, 0, 0)),
                  pl.BlockSpec((None, 1, SSM_BLK_STATE), lambda c: (c, 0, 0))],
        out_specs=[pl.BlockSpec((None, SSM_SEQS, ns2), lambda c: (c, 0, 0)),
                   pl.BlockSpec((None, nb, ns2), lambda c: (c, 0, 0))],
        compiler_params=_cparams(("arbitrary",)),
        name="s5_carry",
    )(e0, h0, ar_seg, ai_seg)


def _s5_params(lam_re, lam_im, log_dt, b_re, b_im, c_re, c_im, d_skip, seg_len):
    dt = jnp.exp(log_dt.astype(F32))[:, None]
    lr = lam_re.astype(F32)
    li = lam_im.astype(F32)
    mag = jnp.exp(lr * dt)
    ab_re = mag * jnp.cos(li * dt)
    ab_im = mag * jnp.sin(li * dt)
    nr = ab_re - 1.0
    ni = ab_im
    den = lr * lr + li * li
    f_re = (nr * lr + ni * li) / den
    f_im = (ni * lr - nr * li) / den
    br = b_re.astype(F32)
    bi = b_im.astype(F32)
    bb_re = f_re[..., None] * br - f_im[..., None] * bi
    bb_im = f_re[..., None] * bi + f_im[..., None] * br
    mag_s = jnp.exp(lr * dt * seg_len)
    as_re = mag_s * jnp.cos(li * dt * seg_len)
    as_im = mag_s * jnp.sin(li * dt * seg_len)
    eye = jnp.eye(SSM_BLK_GROUPS, dtype=F32)

    def in_proj(w):
        w = w.reshape(SSM_NBLK, SSM_BLK_GROUPS, SSM_STATE, SSM_GROUP_CH).transpose(0, 1, 3, 2)
        return jnp.einsum("bjcp,jk->bjckp", w, eye).reshape(SSM_NBLK, SSM_BLK_CH, SSM_BLK_STATE)

    def out_proj(w):
        w = w.reshape(SSM_NBLK, SSM_BLK_GROUPS, SSM_GROUP_CH, SSM_STATE).transpose(0, 1, 3, 2)
        return jnp.einsum("bjpc,jk->bjpkc", w, eye).reshape(SSM_NBLK, SSM_BLK_STATE, SSM_BLK_CH)

    bd = jnp.concatenate([in_proj(bb_re), in_proj(bb_im)], axis=-1).astype(BF16)
    cbd = jnp.concatenate([out_proj(c_re.astype(F32)), -out_proj(c_im.astype(F32))], axis=1).astype(BF16)
    blk = lambda v: v.reshape(SSM_NBLK, 1, SSM_BLK_STATE)
    return (blk(ab_re), blk(ab_im), blk(as_re), blk(as_im), bd, cbd,
            d_skip.astype(F32).reshape(SSM_NBLK, 1, SSM_BLK_CH))


def _state_to_blocks(h_re, h_im):
    nb = h_re.shape[0]
    f = lambda h: h.astype(F32).reshape(nb, SSM_NBLK, SSM_BLK_STATE).transpose(1, 0, 2)
    return jnp.concatenate([f(h_re), f(h_im)], axis=-1)


def _blocks_to_state(hb):
    nb = hb.shape[1]
    f = lambda h: h.transpose(1, 0, 2).reshape(nb, SSM_GROUPS, SSM_STATE)
    return f(hb[..., :SSM_BLK_STATE]), f(hb[..., SSM_BLK_STATE:])


def s5_layer(u, h0_re, h0_im, prm):
    nb, L, d = u.shape
    nseg = SSM_SEQS // nb
    seg = L // nseg
    lam_re, lam_im, log_dt, b_re, b_im, c_re, c_im, d_skip = prm
    ar, ai, as_re, as_im, bd, cbd, dsk = _s5_params(lam_re, lam_im, log_dt, b_re, b_im, c_re, c_im, d_skip, seg)
    ut = u.reshape(nb * nseg, seg, d).transpose(1, 0, 2)
    h0b = _state_to_blocks(h0_re, h0_im)
    if nseg == 1:
        zt, hT = _s5_scan(ut, h0b, ar, ai, bd, cbd, dsk, True)
    else:
        zero = jnp.zeros((SSM_NBLK, SSM_SEQS, 2 * SSM_BLK_STATE), F32)
        _, e0 = _s5_scan(ut, zero, ar, ai, bd, None, None, False)
        hstart, hT = _s5_carry(e0, h0b, as_re, as_im, nb, nseg)
        zt, _ = _s5_scan(ut, hstart, ar, ai, bd, cbd, dsk, True)
    z = zt.astype(BF16).transpose(1, 0, 2).reshape(nb * L, d)
    hr, hi = _blocks_to_state(hT)
    return z, hr, hi


def _gla_gate_body(x_ref, w1_ref, w2_ref, b_ref, o_ref):
    glr = lax.dot_general(x_ref[...], w1_ref[...].astype(BF16), (((1,), (1,)), ((), ())),
                          preferred_element_type=F32)
    logits = jnp.dot(glr.astype(BF16), w2_ref[...].astype(BF16), preferred_element_type=F32) + b_ref[...]
    o_ref[...] = jax.nn.log_sigmoid(logits) / GLA_TAU


def gla_gate(xn, w_glr_t, w_gate_up, b_gate, tm=512):
    m, k = xn.shape
    rp = 128
    w1 = jnp.pad(w_glr_t.astype(F32), ((0, rp - GLA_GATE_RANK), (0, 0)))
    w2 = jnp.pad(w_gate_up.astype(F32), ((0, rp - GLA_GATE_RANK), (0, 0)))
    return pl.pallas_call(
        _gla_gate_body,
        out_shape=jax.ShapeDtypeStruct((m, GLA_DK), F32),
        grid=(m // tm,),
        in_specs=[pl.BlockSpec((tm, k), lambda i: (i, 0)),
                  pl.BlockSpec((rp, k), lambda i: (0, 0)),
                  pl.BlockSpec((rp, GLA_DK), lambda i: (0, 0)),
                  pl.BlockSpec((1, GLA_DK), lambda i: (0, 0))],
        out_specs=pl.BlockSpec((tm, GLA_DK), lambda i: (i, 0)),
        compiler_params=_cparams(("arbitrary",)),
        name="gla_gate",
    )(xn, w1, w2, b_gate.reshape(1, GLA_DK).astype(F32))


def _gla_body(lc, q_ref, k_ref, v_ref, r_ref, la_ref, s0_ref, g_ref, o_ref, st_ref, s_ref):
    c = pl.program_id(2)

    @pl.when(c == 0)
    def _():
        s_ref[...] = s0_ref[...].astype(F32)

    la = la_ref[...]
    la_hi = la.astype(BF16)
    la_lo = (la - la_hi.astype(F32)).astype(BF16)
    row = lax.broadcasted_iota(jnp.int32, (lc, lc), 0)
    col = lax.broadcasted_iota(jnp.int32, (lc, lc), 1)
    tri = (col <= row).astype(BF16)
    cum = (jnp.dot(tri, la_hi, preferred_element_type=F32)
           + jnp.dot(tri, la_lo, preferred_element_type=F32))
    end = cum[lc - 1:lc, :]
    kd = (k_ref[...].astype(F32) * jnp.exp(end - cum)).astype(BF16)
    ones = jnp.ones((lc, 128), BF16)
    tn = (((0,), (0,)), ((), ()))
    end_col = (lax.dot_general(la_hi, ones, tn, preferred_element_type=F32)
               + lax.dot_general(la_lo, ones, tn, preferred_element_type=F32))
    decay = jnp.exp(end_col[:, 0:1])
    s_new = decay * s_ref[...] + lax.dot_general(kd, v_ref[...].astype(BF16), tn, preferred_element_type=F32)
    s_ref[...] = s_new
    qs = (q_ref[...].astype(F32) * (GLA_DK_HEAD ** -0.5)).astype(BF16)
    o = jnp.dot(qs, s_new.astype(BF16), preferred_element_type=F32)
    o = o * lax.rsqrt(jnp.mean(o * o, axis=-1, keepdims=True) + EPS)
    o = o * g_ref[...]
    o_ref[...] = (o * jax.nn.silu(r_ref[...].astype(F32))).astype(o_ref.dtype)

    @pl.when(c == pl.num_programs(2) - 1)
    def _():
        st_ref[...] = s_new


def _drop_alias_ref(body, n_in, *refs):
    return body(*refs[:n_in], *refs[n_in + 1:])


def gla_recurrence(proj, la, row0, nb, L, s0, norm_g, prev=None):
    ntok = proj.shape[0]
    lc = min(GLA_CHUNK, L)
    nc = L // lc
    base = row0 // lc
    kq = GLA_DK // GLA_DK_HEAD
    kv = 2 * GLA_DK // GLA_DV_HEAD
    kr = kv + GLA_HEADS
    rows = lambda b, h, c: base + b * nc + c
    in_specs = [pl.BlockSpec((lc, GLA_DK_HEAD), lambda b, h, c: (rows(b, h, c), h)),
                pl.BlockSpec((lc, GLA_DK_HEAD), lambda b, h, c: (rows(b, h, c), kq + h)),
                pl.BlockSpec((lc, GLA_DV_HEAD), lambda b, h, c: (rows(b, h, c), kv + h)),
                pl.BlockSpec((lc, GLA_DV_HEAD), lambda b, h, c: (rows(b, h, c), kr + h)),
                pl.BlockSpec((lc, GLA_DK_HEAD), lambda b, h, c: (rows(b, h, c), h)),
                pl.BlockSpec((None, None, GLA_DK_HEAD, GLA_DV_HEAD), lambda b, h, c: (b, h, 0, 0)),
                pl.BlockSpec((1, GLA_DV_HEAD), lambda b, h, c: (0, h))]
    args = [proj, proj, proj, proj, la, s0, norm_g.reshape(1, GLA_DV).astype(F32)]
    body = functools.partial(_gla_body, lc)
    aliases = {}
    if prev is not None:
        in_specs.append(pl.BlockSpec(memory_space=pl.ANY))
        args.append(prev)
        aliases = {len(args) - 1: 0}
        body = functools.partial(_drop_alias_ref, body, len(args) - 1)
    o, st = pl.pallas_call(
        body,
        out_shape=[jax.ShapeDtypeStruct((ntok, GLA_DV), BF16),
                   jax.ShapeDtypeStruct((nb, GLA_HEADS, GLA_DK_HEAD, GLA_DV_HEAD), F32)],
        grid=(nb, GLA_HEADS, nc),
        in_specs=in_specs,
        out_specs=[pl.BlockSpec((lc, GLA_DV_HEAD), lambda b, h, c: (rows(b, h, c), h)),
                   pl.BlockSpec((None, None, GLA_DK_HEAD, GLA_DV_HEAD), lambda b, h, c: (b, h, 0, 0))],
        scratch_shapes=[pltpu.VMEM((GLA_DK_HEAD, GLA_DV_HEAD), F32)],
        input_output_aliases=aliases,
        compiler_params=_cparams(("arbitrary", "arbitrary", "arbitrary")),
        name="gla_recurrence",
    )(*args)
    return o, st


def _attn_body(per_head, q_ref, k_ref, v_ref, o_ref):
    nt = (((1,), (1,)), ((), ()))
    scale = CA_HEAD_DIM ** -0.5
    heads = lambda ref: [ref[:, h * CA_HEAD_DIM:(h + 1) * CA_HEAD_DIM] for h in range(CA_HEADS)]

    def softmax(s):
        p = jnp.exp(s - jnp.max(s, axis=-1, keepdims=True))
        return p / jnp.sum(p, axis=-1, keepdims=True)

    if per_head:
        tl = q_ref.shape[0]
        k2 = k_ref[...].reshape(N_MEM * CA_HEADS, CA_HEAD_DIM).astype(BF16)
        v2 = v_ref[...].reshape(N_MEM * CA_HEADS, CA_HEAD_DIM).astype(BF16)
        q4 = jnp.concatenate(heads(q_ref), axis=0)
        s = lax.dot_general(q4, k2, nt, preferred_element_type=F32) * scale
        q_head = lax.broadcasted_iota(jnp.int32, s.shape, 0) // tl
        m_head = lax.broadcasted_iota(jnp.int32, s.shape, 1) % CA_HEADS
        p = softmax(jnp.where(q_head == m_head, s, -jnp.inf))
        o4 = jnp.dot(p.astype(BF16), v2, preferred_element_type=F32).astype(o_ref.dtype)
        for h in range(CA_HEADS):
            o_ref[:, h * CA_HEAD_DIM:(h + 1) * CA_HEAD_DIM] = o4[h * tl:(h + 1) * tl]
    else:
        for h, (qh, kh, vh) in enumerate(zip(heads(q_ref), heads(k_ref), heads(v_ref))):
            s = lax.dot_general(qh, kh.astype(BF16), nt, preferred_element_type=F32) * scale
            o = jnp.dot(softmax(s).astype(BF16), vh.astype(BF16), preferred_element_type=F32)
            o_ref[:, h * CA_HEAD_DIM:(h + 1) * CA_HEAD_DIM] = o.astype(o_ref.dtype)


def mem_attention(q, row0, nb, L, mk, mv, layer=None, prev=None):
    ntok = q.shape[0]
    tl = min(L, 512)
    nl = L // tl
    base = row0 // tl
    rows = lambda b, i: (base + b * nl + i, 0)
    if layer is None:
        mem_spec = pl.BlockSpec((None, N_MEM, D_MODEL), lambda b, i: (b, 0, 0))
    else:
        mem_spec = pl.BlockSpec((None, None, N_MEM, CA_HEADS, CA_HEAD_DIM), lambda b, i: (layer, b, 0, 0, 0))
    in_specs = [pl.BlockSpec((tl, D_MODEL), rows), mem_spec, mem_spec]
    args = [q, mk, mv]
    body = functools.partial(_attn_body, layer is not None)
    aliases = {}
    if prev is not None:
        in_specs.append(pl.BlockSpec(memory_space=pl.ANY))
        args.append(prev)
        aliases = {len(args) - 1: 0}
        body = functools.partial(_drop_alias_ref, body, len(args) - 1)
    return pl.pallas_call(
        body,
        out_shape=jax.ShapeDtypeStruct((ntok, D_MODEL), BF16),
        grid=(nb, nl),
        in_specs=in_specs,
        out_specs=pl.BlockSpec((tl, D_MODEL), rows),
        input_output_aliases=aliases,
        compiler_params=_cparams(("arbitrary", "arbitrary")),
        name="mem_attention",
    )(*args)


def _router_body(x_ref, g_ref, w_ref, b_ref, xn_ref, ids_ref, gates_ref):
    x = x_ref[...]
    ms = jnp.mean(x * x, axis=-1, keepdims=True)
    xn = (x * lax.rsqrt(ms + EPS)) * g_ref[...]
    xn_ref[...] = xn.astype(xn_ref.dtype)
    logits = jnp.dot(xn.astype(BF16), w_ref[...].astype(BF16), preferred_element_type=F32) + b_ref[...]
    tm = logits.shape[0]
    col = lax.broadcasted_iota(jnp.int32, (tm, ROUTER_LANES), 1).astype(F32)
    neg = jnp.float32(-jnp.inf)
    first = lambda mask: jnp.min(jnp.where(mask, col, float(ROUTER_LANES)), axis=-1, keepdims=True)
    gl = jnp.where(col < MOE_GROUPS, logits, neg)
    gmax = jnp.max(gl, axis=-1, keepdims=True)
    gidx = first(gl == gmax)
    g_w = 1.0 / jnp.sum(jnp.exp(gl - gmax), axis=-1, keepdims=True)
    lo = MOE_GROUPS + gidx * MOE_EPG
    el = jnp.where((col >= lo) & (col < lo + MOE_EPG), logits, neg)
    m1 = jnp.max(el, axis=-1, keepdims=True)
    i1 = first(el == m1)
    z = jnp.sum(jnp.exp(el - m1), axis=-1, keepdims=True)
    p1 = 1.0 / z
    el2 = jnp.where(col == i1, neg, el)
    m2 = jnp.max(el2, axis=-1, keepdims=True)
    i2 = first(el2 == m2)
    p2 = jnp.exp(m2 - m1) / z
    tot = p1 + p2
    ids = jnp.where(col == 0, i1 - MOE_GROUPS, jnp.where(col == 1, i2 - MOE_GROUPS, 0.0))
    ids_ref[...] = ids.astype(jnp.int32)
    gates_ref[...] = jnp.where(col == 0, g_w * p1 / tot, jnp.where(col == 1, g_w * p2 / tot, 0.0))


def moe_router(x, g, w_group, b_group, w_expert, b_expert, tm=256):
    m, d = x.shape
    npad = ROUTER_LANES - MOE_GROUPS - MOE_EXPERTS
    w = jnp.pad(jnp.concatenate([w_group, w_expert], axis=1).astype(F32), ((0, 0), (0, npad)))
    b = jnp.pad(jnp.concatenate([b_group, b_expert]).astype(F32), (0, npad)).reshape(1, ROUTER_LANES)
    row = lambda n: pl.BlockSpec((tm, n), lambda i: (i, 0))
    xn, ids, gates = pl.pallas_call(
        _router_body,
        out_shape=[jax.ShapeDtypeStruct((m, d), F32),
                   jax.ShapeDtypeStruct((m, ROUTER_LANES), jnp.int32),
                   jax.ShapeDtypeStruct((m, ROUTER_LANES), F32)],
        grid=(m // tm,),
        in_specs=[row(d), pl.BlockSpec((1, d), lambda i: (0, 0)),
                  pl.BlockSpec((d, ROUTER_LANES), lambda i: (0, 0)),
                  pl.BlockSpec((1, ROUTER_LANES), lambda i: (0, 0))],
        out_specs=[row(d), row(ROUTER_LANES), row(ROUTER_LANES)],
        compiler_params=_cparams(("arbitrary",)),
        name="moe_router",
    )(x, g.reshape(1, d).astype(F32), w, b)
    return xn, ids[:, :MOE_TOP_K], gates


def _gather_body(tm, src_ref, cnt_ref, x_hbm, o_ref, buf_ref, sem):
    i = pl.program_id(0)
    n_groups = cnt_ref[i]

    @pl.when(n_groups < tm // GATHER_GROUP)
    def _():
        buf_ref[...] = jnp.zeros_like(buf_ref)

    def copy(r):
        return pltpu.make_async_copy(x_hbm.at[pl.ds(src_ref[i * tm + r], 1), :],
                                     buf_ref.at[pl.ds(r, 1), :], sem)

    def start(g, _):
        for u in range(GATHER_GROUP):
            copy(g * GATHER_GROUP + u).start(priority=u % DMA_PRIORITIES)
        return 0

    def wait(g, _):
        for u in range(GATHER_GROUP):
            copy(g * GATHER_GROUP + u).wait()
        return 0

    lax.fori_loop(0, n_groups, start, 0)
    lax.fori_loop(0, n_groups, wait, 0)
    o_ref[...] = buf_ref[...].astype(o_ref.dtype)


def moe_gather(x, row_src, tile_groups, n_tiles, tm):
    d = x.shape[1]
    return pl.pallas_call(
        functools.partial(_gather_body, tm),
        out_shape=jax.ShapeDtypeStruct((n_tiles * tm, d), BF16),
        grid_spec=pltpu.PrefetchScalarGridSpec(
            num_scalar_prefetch=2,
            grid=(n_tiles,),
            in_specs=[pl.BlockSpec(memory_space=pl.ANY)],
            out_specs=pl.BlockSpec((tm, d), lambda i, src, cnt: (i, 0)),
            scratch_shapes=[pltpu.VMEM((tm, d), x.dtype), pltpu.SemaphoreType.DMA(())],
        ),
        compiler_params=_cparams(("arbitrary",)),
        name="moe_gather",
    )(row_src, tile_groups, x)


def _segment_weights(te_ref, seg_ref, meta_ref, copies, cast):
    c = pl.program_id(0)
    r = pl.program_id(1)
    nt_used = meta_ref[0]
    nseg = meta_ref[1]
    used = r < nt_used
    seg = seg_ref[MOE_EXPERTS + r]
    first = jnp.logical_and(used, jnp.logical_or(r == 0, te_ref[r] != te_ref[jnp.maximum(r - 1, 0)]))
    g = c * nseg + seg
    slot = lax.rem(g, 2)

    @pl.when(first)
    def _():
        @pl.when(g == 0)
        def _():
            for cp in copies(te_ref[r], c, 0):
                cp.start()

        for cp in copies(te_ref[r], c, slot):
            cp.wait()
        wraps = seg + 1 >= nseg
        nxt_seg = jnp.where(wraps, 0, seg + 1)
        nxt_c = jnp.where(wraps, c + 1, c)

        @pl.when(nxt_c < pl.num_programs(0))
        def _():
            for cp in copies(seg_ref[nxt_seg], nxt_c, 1 - slot):
                cp.start()

        cast(slot)

    return used


def _moe_up_body(layer, tf, te_ref, seg_ref, meta_ref, x_ref, w_hbm, h_ref, wbuf, wgb_ref, wub_ref, sem):
    f = MOE_HIDDEN

    def copies(e, c, slot):
        col = pl.multiple_of(c * tf, tf)
        return [pltpu.make_async_copy(w_hbm.at[layer, e, :, pl.ds(half * f + col, tf)], wbuf.at[slot, half],
                                      sem.at[slot, half]) for half in range(2)]

    def cast(slot):
        wgb_ref[...] = wbuf[slot, 0].astype(BF16)
        wub_ref[...] = wbuf[slot, 1].astype(BF16)

    used = _segment_weights(te_ref, seg_ref, meta_ref, copies, cast)

    @pl.when(used)
    def _():
        x = x_ref[...]
        gate = jnp.dot(x, wgb_ref[...], preferred_element_type=F32)
        up = jnp.dot(x, wub_ref[...], preferred_element_type=F32)
        h_ref[...] = (jax.nn.silu(gate) * up).astype(h_ref.dtype)

    @pl.when(jnp.logical_not(used))
    def _():
        h_ref[...] = jnp.zeros_like(h_ref)


def _moe_down_body(layer, tn, te_ref, seg_ref, meta_ref, h_ref, w_hbm, y_ref, wbuf, wdb_ref, sem):
    def copies(e, c, slot):
        col = pl.multiple_of(c * tn, tn)
        return [pltpu.make_async_copy(w_hbm.at[layer, e, :, pl.ds(col, tn)], wbuf.at[slot], sem.at[slot])]

    def cast(slot):
        wdb_ref[...] = wbuf[slot].astype(BF16)

    used = _segment_weights(te_ref, seg_ref, meta_ref, copies, cast)

    @pl.when(used)
    def _():
        y_ref[...] = jnp.dot(h_ref[...], wdb_ref[...], preferred_element_type=F32)

    @pl.when(jnp.logical_not(used))
    def _():
        y_ref[...] = jnp.zeros_like(y_ref)


def moe_experts(xs, tile_e, seg_info, meta, w_gate_up, w_down, layer, tm, tf=512, tnd=D_MODEL):
    p, d = xs.shape
    n_tiles = p // tm
    f = MOE_HIDDEN
    h = pl.pallas_call(
        functools.partial(_moe_up_body, layer, tf),
        out_shape=jax.ShapeDtypeStruct((p, f), BF16),
        grid_spec=pltpu.PrefetchScalarGridSpec(
            num_scalar_prefetch=3,
            grid=(f // tf, n_tiles),
            in_specs=[pl.BlockSpec((tm, d), lambda c, r, te, sg, mt: (r, 0)),
                      pl.BlockSpec(memory_space=pl.ANY)],
            out_specs=pl.BlockSpec((tm, tf), lambda c, r, te, sg, mt: (r, c)),
            scratch_shapes=[pltpu.VMEM((2, 2, d, tf), F32), pltpu.VMEM((d, tf), BF16), pltpu.VMEM((d, tf), BF16),
                            pltpu.SemaphoreType.DMA((2, 2))],
        ),
        compiler_params=_cparams(("arbitrary", "arbitrary")),
        name="moe_up",
    )(tile_e, seg_info, meta, xs, w_gate_up)
    return pl.pallas_call(
        functools.partial(_moe_down_body, layer, tnd),
        out_shape=jax.ShapeDtypeStruct((p, d), F32),
        grid_spec=pltpu.PrefetchScalarGridSpec(
            num_scalar_prefetch=3,
            grid=(d // tnd, n_tiles),
            in_specs=[pl.BlockSpec((tm, f), lambda c, r, te, sg, mt: (r, 0)),
                      pl.BlockSpec(memory_space=pl.ANY)],
            out_specs=pl.BlockSpec((tm, tnd), lambda c, r, te, sg, mt: (r, c)),
            scratch_shapes=[pltpu.VMEM((2, f, tnd), F32), pltpu.VMEM((f, tnd), BF16),
                            pltpu.SemaphoreType.DMA((2,))],
        ),
        compiler_params=_cparams(("arbitrary", "arbitrary")),
        name="moe_down",
    )(tile_e, seg_info, meta, h, w_down)


def _combine_body(tt, pos_ref, x_ref, g_ref, y_hbm, o_ref, buf_ref, sem):
    i = pl.program_id(0)

    def copy(t, k):
        return pltpu.make_async_copy(y_hbm.at[pl.ds(pos_ref[(i * tt + t) * MOE_TOP_K + k], 1), :],
                                     buf_ref.at[k, pl.ds(t, 1), :], sem)

    def start(t, _):
        for k in range(MOE_TOP_K):
            copy(t, k).start(priority=k % DMA_PRIORITIES)
        return 0

    def wait(t, _):
        for k in range(MOE_TOP_K):
            copy(t, k).wait()
        return 0

    lax.fori_loop(0, tt, start, 0)
    lax.fori_loop(0, tt, wait, 0)
    g = g_ref[...]
    y = g[:, 0:1] * buf_ref[0] + g[:, 1:2] * buf_ref[1]
    o_ref[...] = x_ref[...] + y


def moe_combine(x, gates, pos, y_sorted, tt=256):
    m, d = x.shape
    return pl.pallas_call(
        functools.partial(_combine_body, tt),
        out_shape=jax.ShapeDtypeStruct((m, d), F32),
        grid_spec=pltpu.PrefetchScalarGridSpec(
            num_scalar_prefetch=1,
            grid=(m // tt,),
            in_specs=[pl.BlockSpec((tt, d), lambda i, pos: (i, 0)),
                      pl.BlockSpec((tt, ROUTER_LANES), lambda i, pos: (i, 0)),
                      pl.BlockSpec(memory_space=pl.ANY)],
            out_specs=pl.BlockSpec((tt, d), lambda i, pos: (i, 0)),
            scratch_shapes=[pltpu.VMEM((MOE_TOP_K, tt, d), F32), pltpu.SemaphoreType.DMA(())],
        ),
        compiler_params=_cparams(("arbitrary",)),
        name="moe_combine",
    )(pos, x, gates, y_sorted)


def _moe_plan(ids, tm):
    t = ids.shape[0]
    n = t * MOE_TOP_K
    n_tiles = -(-(n + MOE_EXPERTS * (tm - 1)) // tm)
    i32 = jnp.int32
    flat_e = ids.reshape(n)
    order = jnp.argsort(flat_e, stable=True).astype(i32)
    inv = jnp.argsort(order).astype(i32)
    se = flat_e[order]
    experts = jnp.arange(MOE_EXPERTS, dtype=i32)
    starts = jnp.searchsorted(se, experts, side="left", method="compare_all").astype(i32)
    counts = jnp.searchsorted(se, experts, side="right", method="compare_all").astype(i32) - starts
    tile_count = (counts + tm - 1) // tm
    tile_end = jnp.cumsum(tile_count).astype(i32)
    tile_start = tile_end - tile_count
    pos = tile_start[flat_e] * tm + (inv - starts[flat_e])
    tiles = jnp.arange(n_tiles, dtype=i32)
    tile_e = jnp.minimum(jnp.searchsorted(tile_end, tiles, side="right", method="compare_all"),
                         MOE_EXPERTS - 1).astype(i32)
    tile_valid = jnp.clip(counts[tile_e] - (tiles - tile_start[tile_e]) * tm, 0, tm)
    tile_groups = (tile_valid + GATHER_GROUP - 1) // GATHER_GROUP
    within = jnp.arange(tm, dtype=i32)[None, :]
    slot = (starts[tile_e] + (tiles - tile_start[tile_e]) * tm)[:, None] + within
    src = order[jnp.clip(slot, 0, n - 1)] // MOE_TOP_K
    row_src = jnp.where(within < tile_valid[:, None], src, 0).reshape(n_tiles * tm)
    nonempty = tile_count > 0
    seg_of_expert = jnp.cumsum(nonempty.astype(i32)).astype(i32) - 1
    seg_expert = jnp.argsort(jnp.where(nonempty, 0, 1).astype(i32), stable=True).astype(i32)
    seg_info = jnp.concatenate([seg_expert, seg_of_expert[tile_e]])
    meta = jnp.stack([tile_end[-1], jnp.sum(nonempty.astype(i32))]).astype(i32)
    return row_src, pos.astype(i32), tile_e, seg_info, meta, tile_groups.astype(i32), n_tiles


def hier_moe(x, g, layer, w_group, b_group, w_expert, b_expert, w_gate_up, w_down):
    xn, ids, gates = moe_router(x, g, w_group, b_group, w_expert, b_expert)
    row_src, pos, tile_e, seg_info, meta, tile_groups, n_tiles = _moe_plan(ids, MOE_TILE)
    xs = moe_gather(xn, row_src, tile_groups, n_tiles, MOE_TILE)
    y = moe_experts(xs, tile_e, seg_info, meta, w_gate_up, w_down, layer, MOE_TILE)
    return moe_combine(x, gates, pos, y)


def kernel(x_prompt, x_sample, mem_prompt, state_ssm_re, state_ssm_im, state_gla, cache_mem_k, cache_mem_v, norm_mixer, norm_ca, norm_moe, norm_final, ssm_lambda_re, ssm_lambda_im, ssm_log_dt, ssm_b_re, ssm_b_im, ssm_c_re, ssm_c_im, ssm_d, ssm_w_glu, gla_w_in, gla_w_gate_up, gla_b_gate, gla_norm, gla_w_out, ca_mem_norm, ca_w_q, ca_w_kv, ca_w_o, moe_w_group, moe_b_group, moe_w_expert, moe_b_expert, moe_w_gate_up, moe_w_down):
    bp, lp, d = x_prompt.shape
    bs, ls, _ = x_sample.shape
    np_ = bp * lp
    ns_ = bs * ls

    mem = mem_prompt.reshape(bp * N_MEM, d)
    pk, pv = [], []
    for i in range(DEPTH):
        mn = rmsnorm(mem, ca_mem_norm[i])
        kv = matmul(mn, ca_w_kv, i, out_dtype=F32)
        pk.append(kv[:, :d].reshape(bp, N_MEM, d))
        pv.append(kv[:, d:].reshape(bp, N_MEM, d))
    prompt_mem_k = jnp.stack(pk).reshape(DEPTH, bp, N_MEM, CA_HEADS, CA_HEAD_DIM)
    prompt_mem_v = jnp.stack(pv).reshape(DEPTH, bp, N_MEM, CA_HEADS, CA_HEAD_DIM)

    x = jnp.concatenate([x_prompt.reshape(np_, d), x_sample.reshape(ns_, d)], axis=0)
    zero_ssm = jnp.zeros((bp, SSM_GROUPS, SSM_STATE), F32)
    zero_gla = jnp.zeros((bp, GLA_HEADS, GLA_DK_HEAD, GLA_DV_HEAD), F32)
    zero_act = jnp.zeros((np_ + ns_, d), BF16)
    p_re, p_im, s_re, s_im, p_gla, s_gla = [], [], [], [], [], []

    for i in range(DEPTH):
        j = i // 2
        if i % 2 == 0:
            xn_p = rmsnorm(x, norm_mixer[i], out_dtype=F32, row0=0, nrows=np_)
            xn_s = rmsnorm(x, norm_mixer[i], out_dtype=F32, row0=np_, nrows=ns_)
            prm = (ssm_lambda_re[j], ssm_lambda_im[j], ssm_log_dt[j], ssm_b_re[j], ssm_b_im[j],
                   ssm_c_re[j], ssm_c_im[j], ssm_d[j])
            zp, hr, hi = s5_layer(xn_p.reshape(bp, lp, d), zero_ssm, zero_ssm, prm)
            p_re.append(hr)
            p_im.append(hi)
            zs, hr, hi = s5_layer(xn_s.reshape(bs, ls, d), state_ssm_re[j], state_ssm_im[j], prm)
            s_re.append(hr)
            s_im.append(hi)
            z = jnp.concatenate([zp, zs], axis=0)
            x = matmul(z, ssm_w_glu, j, mode="glu_res", res=x, n_out=d, tn=256)
        else:
            xn = rmsnorm(x, norm_mixer[i])
            w_in_t = jnp.swapaxes(gla_w_in, 1, 2)
            proj = matmul(xn, w_in_t, j, out_dtype=F32, n_out=GLA_MAIN, w_is_nk=True)
            la = gla_gate(xn, w_in_t[j, GLA_MAIN:, :], gla_w_gate_up[j], gla_b_gate[j])
            o, st = gla_recurrence(proj, la, 0, bp, lp, zero_gla, gla_norm[j], prev=zero_act)
            p_gla.append(st)
            o, st = gla_recurrence(proj, la, np_, bs, ls, state_gla[j], gla_norm[j], prev=o)
            s_gla.append(st)
            x = matmul(o, gla_w_out, j, mode="res", res=x)

        xn = rmsnorm(x, norm_ca[i])
        q = matmul(xn, ca_w_q, i, out_dtype=BF16)
        att = mem_attention(q, 0, bp, lp, pk[i], pv[i], prev=zero_act)
        att = mem_attention(q, np_, bs, ls, cache_mem_k, cache_mem_v, layer=i, prev=att)
        x = matmul(att, ca_w_o, i, mode="res", res=x)

        x = hier_moe(x, norm_moe[i], i, moe_w_group[i], moe_b_group[i], moe_w_expert[i], moe_b_expert[i],
                     moe_w_gate_up, moe_w_down)

    y_prompt = rmsnorm(x, norm_final, out_dtype=F32, row0=0, nrows=np_)
    y_sample = rmsnorm(x, norm_final, out_dtype=F32, row0=np_, nrows=ns_)
    return (y_prompt.reshape(bp, lp, d), y_sample.reshape(bs, ls, d),
            jnp.stack(p_re), jnp.stack(p_im), jnp.stack(p_gla), prompt_mem_k, prompt_mem_v,
            jnp.stack(s_re), jnp.stack(s_im), jnp.stack(s_gla))
```

```python
import functools
import math

import jax
import jax.numpy as jnp
import numpy as np
from jax import lax
from jax.experimental import pallas as pl
from jax.experimental.pallas import tpu as pltpu

F32 = jnp.float32
BF16 = jnp.bfloat16

EPS = 1e-6
D_MODEL = 4096
DEPTH = 2

SSM_GROUP_CH = 16
SSM_GROUPS = D_MODEL // SSM_GROUP_CH
SSM_STATE = 64
SSM_SEQS = 32
SSM_SEQ_TILE = 32
SSM_SCAN_UNROLL = True
SSM_BLK_CH = 128
SSM_BLK_GROUPS = SSM_BLK_CH // SSM_GROUP_CH
SSM_NBLK = D_MODEL // SSM_BLK_CH
SSM_BLK_STATE = SSM_BLK_GROUPS * SSM_STATE

GLA_HEADS = 4
GLA_DK = D_MODEL // 2
GLA_DV = D_MODEL
GLA_DK_HEAD = GLA_DK // GLA_HEADS
GLA_DV_HEAD = GLA_DV // GLA_HEADS
GLA_GATE_RANK = 16
GLA_TAU = 16.0
GLA_MAIN = 2 * GLA_DK + 2 * GLA_DV
GLA_CHUNK = 64

N_MEM = 256
CA_HEADS = 4
CA_HEAD_DIM = D_MODEL // CA_HEADS

MOE_GROUPS = 4
MOE_EPG = 8
MOE_EXPERTS = MOE_GROUPS * MOE_EPG
MOE_TOP_K = 2
MOE_HIDDEN = D_MODEL // 4
MOE_TILE = 256
GATHER_GROUP = 8
DMA_PRIORITIES = 2
ROUTER_LANES = 128

DENSE_TM = 1088
SUBLANES = 8
VMEM_LIMIT = 56 * 1024 * 1024


def _cparams(sem):
    return pltpu.CompilerParams(dimension_semantics=sem, vmem_limit_bytes=VMEM_LIMIT)


def _rms_body(n_add, emit_sum, x_ref, *refs):
    adds = refs[:n_add]
    g_ref = refs[n_add]
    outs = refs[n_add + 1:]
    x = x_ref[...].astype(F32)
    for a in adds:
        x = x + a[...].astype(F32)
    ms = jnp.mean(x * x, axis=-1, keepdims=True)
    xn = (x * lax.rsqrt(ms + EPS)) * g_ref[...]
    if emit_sum:
        outs[0][...] = x
        outs[1][...] = xn.astype(outs[1].dtype)
    else:
        outs[0][...] = xn.astype(outs[0].dtype)


def rmsnorm(x, g, adds=(), out_dtype=BF16, emit_sum=False, tm=256, row0=0, nrows=None):
    d = x.shape[1]
    m = x.shape[0] if nrows is None else nrows
    assert m % tm == 0 and row0 % tm == 0
    b0 = row0 // tm
    src = pl.BlockSpec((tm, d), lambda i: (b0 + i, 0))
    row = pl.BlockSpec((tm, d), lambda i: (i, 0))
    out_shape = [jax.ShapeDtypeStruct((m, d), out_dtype)]
    out_specs = [row]
    if emit_sum:
        out_shape.insert(0, jax.ShapeDtypeStruct((m, d), F32))
        out_specs.insert(0, row)
    res = pl.pallas_call(
        functools.partial(_rms_body, len(adds), emit_sum),
        out_shape=out_shape,
        grid=(m // tm,),
        in_specs=[src] * (1 + len(adds)) + [pl.BlockSpec((1, d), lambda i: (0, 0))],
        out_specs=out_specs,
        compiler_params=_cparams(("arbitrary",)),
        name="rmsnorm",
    )(x, *adds, g.reshape(1, d).astype(F32))
    return res if emit_sum else res[0]


def _mm_body(mode, w_is_nk, x_ref, *refs):
    if mode == "glu_res":
        w_ref, w2_ref, res_ref, o_ref, wb_ref, wb2_ref = refs
    elif mode == "res":
        w_ref, res_ref, o_ref, wb_ref = refs
    else:
        w_ref, o_ref, wb_ref = refs

    @pl.when(pl.program_id(1) == 0)
    def _():
        wb_ref[...] = w_ref[...].astype(BF16)
        if mode == "glu_res":
            wb2_ref[...] = w2_ref[...].astype(BF16)

    x = x_ref[...]
    if w_is_nk:
        acc = lax.dot_general(x, wb_ref[...], (((1,), (1,)), ((), ())), preferred_element_type=F32)
    else:
        acc = jnp.dot(x, wb_ref[...], preferred_element_type=F32)
    if mode == "glu_res":
        gate = jnp.dot(x, wb2_ref[...], preferred_element_type=F32)
        o_ref[...] = res_ref[...] + acc * jax.nn.sigmoid(gate)
    elif mode == "res":
        o_ref[...] = res_ref[...] + acc
    else:
        o_ref[...] = acc.astype(o_ref.dtype)


def matmul(x, w, layer, mode="plain", res=None, out_dtype=F32, n_out=None, tm=DENSE_TM, tn=512, w_is_nk=False):
    m, k = x.shape
    n = n_out if n_out is not None else w.shape[1 if w_is_nk else 2]
    if m % tm:
        tm = m
    assert m % tm == 0 and n % tn == 0 and not (w_is_nk and mode != "plain")
    nb = n // tn
    if w_is_nk:
        w_spec, w_tile = pl.BlockSpec((None, tn, k), lambda j, i: (layer, j, 0)), (tn, k)
    else:
        w_spec, w_tile = pl.BlockSpec((None, k, tn), lambda j, i: (layer, 0, j)), (k, tn)
    in_specs = [pl.BlockSpec((tm, k), lambda j, i: (i, 0)), w_spec]
    args = [x, w]
    scratch = [pltpu.VMEM(w_tile, BF16)]
    if mode == "glu_res":
        in_specs.append(pl.BlockSpec((None, k, tn), lambda j, i: (layer, 0, nb + j)))
        args.append(w)
        scratch.append(pltpu.VMEM((k, tn), BF16))
    if mode in ("res", "glu_res"):
        in_specs.append(pl.BlockSpec((tm, tn), lambda j, i: (i, j)))
        args.append(res)
        out_dtype = F32
    return pl.pallas_call(
        functools.partial(_mm_body, mode, w_is_nk),
        out_shape=jax.ShapeDtypeStruct((m, n), out_dtype),
        grid=(nb, m // tm),
        in_specs=in_specs,
        out_specs=pl.BlockSpec((tm, tn), lambda j, i: (i, j)),
        scratch_shapes=scratch,
        compiler_params=_cparams(("arbitrary", "arbitrary")),
        name="matmul_" + mode,
    )(*args)


def _s5_body(lt, emit_y, u_ref, h0_ref, ar_ref, ai_ref, bd_ref, *refs):
    if emit_y:
        cbd_ref, d_ref, z_ref, ht_ref, h_ref, bu_ref, hs_ref = refs
    else:
        ht_ref, h_ref, bu_ref = refs
    tb = pl.program_id(2)
    ns = SSM_BLK_STATE

    @pl.when(tb == 0)
    def _():
        h_ref[...] = h0_ref[...]

    st = SSM_SEQ_TILE
    groups = range(st // SUBLANES)
    u = u_ref[...].reshape(lt * st, SSM_BLK_CH)
    bu_ref[...] = jnp.dot(u.astype(BF16), bd_ref[...], preferred_element_type=F32)
    ar = jnp.broadcast_to(ar_ref[...], (SUBLANES, ns))
    ai = jnp.broadcast_to(ai_ref[...], (SUBLANES, ns))

    def step(t, carry):
        r0 = t * st
        out = []
        for g in groups:
            hr, hi = carry[g]
            rows = pl.ds(pl.multiple_of(r0 + g * SUBLANES, SUBLANES), SUBLANES)
            nr = ar * hr - ai * hi + bu_ref[rows, 0:ns]
            ni = ar * hi + ai * hr + bu_ref[rows, ns:2 * ns]
            if emit_y:
                hs_ref[rows, 0:ns] = nr
                hs_ref[rows, ns:2 * ns] = ni
            out.append((nr, ni))
        return tuple(out)

    grp = lambda g: slice(g * SUBLANES, (g + 1) * SUBLANES)
    init = tuple((h_ref[grp(g), 0:ns], h_ref[grp(g), ns:2 * ns]) for g in groups)
    final = lax.fori_loop(0, lt, step, init, unroll=SSM_SCAN_UNROLL)
    for g in groups:
        hr, hi = final[g]
        h_ref[grp(g), 0:ns] = hr
        h_ref[grp(g), ns:2 * ns] = hi
        ht_ref[grp(g), 0:ns] = hr
        ht_ref[grp(g), ns:2 * ns] = hi
    if emit_y:
        y = jnp.dot(hs_ref[...].astype(BF16), cbd_ref[...], preferred_element_type=F32)
        y = y + d_ref[...] * u
        z_ref[...] = jax.nn.gelu(y, approximate=True).reshape(lt, st, SSM_BLK_CH)


def _s5_scan(ut, h0, ar, ai, bd, cbd, dsk, emit_y):
    ltot = ut.shape[0]
    lt = min(ltot, 64)
    assert ltot % lt == 0
    ns2 = 2 * SSM_BLK_STATE
    st = SSM_SEQ_TILE
    nsg = SSM_SEQS // st
    in_specs = [
        pl.BlockSpec((lt, st, SSM_BLK_CH), lambda c, s, t: (t, s, c)),
        pl.BlockSpec((None, st, ns2), lambda c, s, t: (c, s, 0)),
        pl.BlockSpec((None, 1, SSM_BLK_STATE), lambda c, s, t: (c, 0, 0)),
        pl.BlockSpec((None, 1, SSM_BLK_STATE), lambda c, s, t: (c, 0, 0)),
        pl.BlockSpec((None, SSM_BLK_CH, ns2), lambda c, s, t: (c, 0, 0)),
    ]
    args = [ut, h0, ar, ai, bd]
    ht_shape = jax.ShapeDtypeStruct((SSM_NBLK, SSM_SEQS, ns2), F32)
    ht_spec = pl.BlockSpec((None, st, ns2), lambda c, s, t: (c, s, 0))
    scratch = [pltpu.VMEM((st, ns2), F32), pltpu.VMEM((lt * st, ns2), F32)]
    if emit_y:
        in_specs += [pl.BlockSpec((None, ns2, SSM_BLK_CH), lambda c, s, t: (c, 0, 0)),
                     pl.BlockSpec((None, 1, SSM_BLK_CH), lambda c, s, t: (c, 0, 0))]
        args += [cbd, dsk]
        out_shape = [jax.ShapeDtypeStruct(ut.shape, F32), ht_shape]
        out_specs = [pl.BlockSpec((lt, st, SSM_BLK_CH), lambda c, s, t: (t, s, c)), ht_spec]
        scratch.append(pltpu.VMEM((lt * st, ns2), F32))
    else:
        out_shape = [ht_shape]
        out_specs = [ht_spec]
    res = pl.pallas_call(
        functools.partial(_s5_body, lt, emit_y),
        out_shape=out_shape,
        grid=(SSM_NBLK, nsg, ltot // lt),
        in_specs=in_specs,
        out_specs=out_specs,
        scratch_shapes=scratch,
        compiler_params=_cparams(("arbitrary", "arbitrary", "arbitrary")),
        name="s5_scan_y" if emit_y else "s5_scan_state",
    )(*args)
    return (res[0], res[1]) if emit_y else (None, res[0])


def _s5_carry_body(nb, nseg, e_ref, h0_ref, ar_ref, ai_ref, hs_ref, he_ref):
    ns = SSM_BLK_STATE
    ar = ar_ref[...]
    ai = ai_ref[...]
    for b in range(nb):
        hr = h0_ref[b:b + 1, 0:ns]
        hi = h0_ref[b:b + 1, ns:2 * ns]
        for j in range(nseg):
            s = b * nseg + j
            hs_ref[s:s + 1, 0:ns] = hr
            hs_ref[s:s + 1, ns:2 * ns] = hi
            er = e_ref[s:s + 1, 0:ns]
            ei = e_ref[s:s + 1, ns:2 * ns]
            hr, hi = ar * hr - ai * hi + er, ar * hi + ai * hr + ei
        he_ref[b:b + 1, 0:ns] = hr
        he_ref[b:b + 1, ns:2 * ns] = hi


def _s5_carry(e0, h0, ar_seg, ai_seg, nb, nseg):
    ns2 = 2 * SSM_BLK_STATE
    return pl.pallas_call(
        functools.partial(_s5_carry_body, nb, nseg),
        out_shape=[jax.ShapeDtypeStruct((SSM_NBLK, SSM_SEQS, ns2), F32),
                   jax.ShapeDtypeStruct((SSM_NBLK, nb, ns2), F32)],
        grid=(SSM_NBLK,),
        in_specs=[pl.BlockSpec((None, SSM_SEQS, ns2), lambda c: (c, 0, 0)),
                  pl.BlockSpec((None, nb, ns2), lambda c: (c, 0, 0)),
                  pl.BlockSpec((None, 1, SSM_BLK_STATE), lambda c: (c, 0, 0)),
                  pl.BlockSpec((None, 1, SSM_BLK_STATE), lambda c: (c, 0, 0))],
        out_specs=[pl.BlockSpec((None, SSM_SEQS, ns2), lambda c: (c, 0, 0)),
                   pl.BlockSpec((None, nb, ns2), lambda c: (c, 0, 0))],
        compiler_params=_cparams(("arbitrary",)),
        name="s5_carry",
    )(e0, h0, ar_seg, ai_seg)


def _s5_params(lam_re, lam_im, log_dt, b_re, b_im, c_re, c_im, d_skip, seg_len):
    dt = jnp.exp(log_dt.astype(F32))[:, None]
    lr = lam_re.astype(F32)
    li = lam_im.astype(F32)
    mag = jnp.exp(lr * dt)
    ab_re = mag * jnp.cos(li * dt)
    ab_im = mag * jnp.sin(li * dt)
    nr = ab_re - 1.0
    ni = ab_im
    den = lr * lr + li * li
    f_re = (nr * lr + ni * li) / den
    f_im = (ni * lr - nr * li) / den
    br = b_re.astype(F32)
    bi = b_im.astype(F32)
    bb_re = f_re[..., None] * br - f_im[..., None] * bi
    bb_im = f_re[..., None] * bi + f_im[..., None] * br
    mag_s = jnp.exp(lr * dt * seg_len)
    as_re = mag_s * jnp.cos(li * dt * seg_len)
    as_im = mag_s * jnp.sin(li * dt * seg_len)
    eye = jnp.eye(SSM_BLK_GROUPS, dtype=F32)

    def in_proj(w):
        w = w.reshape(SSM_NBLK, SSM_BLK_GROUPS, SSM_STATE, SSM_GROUP_CH).transpose(0, 1, 3, 2)
        return jnp.einsum("bjcp,jk->bjckp", w, eye).reshape(SSM_NBLK, SSM_BLK_CH, SSM_BLK_STATE)

    def out_proj(w):
        w = w.reshape(SSM_NBLK, SSM_BLK_GROUPS, SSM_GROUP_CH, SSM_STATE).transpose(0, 1, 3, 2)
        return jnp.einsum("bjpc,jk->bjpkc", w, eye).reshape(SSM_NBLK, SSM_BLK_STATE, SSM_BLK_CH)

    bd = jnp.concatenate([in_proj(bb_re), in_proj(bb_im)], axis=-1).astype(BF16)
    cbd = jnp.concatenate([out_proj(c_re.astype(F32)), -out_proj(c_im.astype(F32))], axis=1).astype(BF16)
    blk = lambda v: v.reshape(SSM_NBLK, 1, SSM_BLK_STATE)
    return (blk(ab_re), blk(ab_im), blk(as_re), blk(as_im), bd, cbd,
            d_skip.astype(F32).reshape(SSM_NBLK, 1, SSM_BLK_CH))


def _state_to_blocks(h_re, h_im):
    nb = h_re.shape[0]
    f = lambda h: h.astype(F32).reshape(nb, SSM_NBLK, SSM_BLK_STATE).transpose(1, 0, 2)
    return jnp.concatenate([f(h_re), f(h_im)], axis=-1)


def _blocks_to_state(hb):
    nb = hb.shape[1]
    f = lambda h: h.transpose(1, 0, 2).reshape(nb, SSM_GROUPS, SSM_STATE)
    return f(hb[..., :SSM_BLK_STATE]), f(hb[..., SSM_BLK_STATE:])


def s5_layer(u, h0_re, h0_im, prm):
    nb, L, d = u.shape
    nseg = SSM_SEQS // nb
    seg = L // nseg
    lam_re, lam_im, log_dt, b_re, b_im, c_re, c_im, d_skip = prm
    ar, ai, as_re, as_im, bd, cbd, dsk = _s5_params(lam_re, lam_im, log_dt, b_re, b_im, c_re, c_im, d_skip, seg)
    ut = u.reshape(nb * nseg, seg, d).transpose(1, 0, 2)
    h0b = _state_to_blocks(h0_re, h0_im)
    if nseg == 1:
        zt, hT = _s5_scan(ut, h0b, ar, ai, bd, cbd, dsk, True)
    else:
        zero = jnp.zeros((SSM_NBLK, SSM_SEQS, 2 * SSM_BLK_STATE), F32)
        _, e0 = _s5_scan(ut, zero, ar, ai, bd, None, None, False)
        hstart, hT = _s5_carry(e0, h0b, as_re, as_im, nb, nseg)
        zt, _ = _s5_scan(ut, hstart, ar, ai, bd, cbd, dsk, True)
    z = zt.astype(BF16).transpose(1, 0, 2).reshape(nb * L, d)
    hr, hi = _blocks_to_state(hT)
    return z, hr, hi


def _gla_gate_body(x_ref, w1_ref, w2_ref, b_ref, o_ref):
    glr = lax.dot_general(x_ref[...], w1_ref[...].astype(BF16), (((1,), (1,)), ((), ())),
                          preferred_element_type=F32)
    logits = jnp.dot(glr.astype(BF16), w2_ref[...].astype(BF16), preferred_element_type=F32) + b_ref[...]
    o_ref[...] = jax.nn.log_sigmoid(logits) / GLA_TAU


def gla_gate(xn, w_glr_t, w_gate_up, b_gate, tm=512):
    m, k = xn.shape
    rp = 128
    w1 = jnp.pad(w_glr_t.astype(F32), ((0, rp - GLA_GATE_RANK), (0, 0)))
    w2 = jnp.pad(w_gate_up.astype(F32), ((0, rp - GLA_GATE_RANK), (0, 0)))
    return pl.pallas_call(
        _gla_gate_body,
        out_shape=jax.ShapeDtypeStruct((m, GLA_DK), F32),
        grid=(m // tm,),
        in_specs=[pl.BlockSpec((tm, k), lambda i: (i, 0)),
                  pl.BlockSpec((rp, k), lambda i: (0, 0)),
                  pl.BlockSpec((rp, GLA_DK), lambda i: (0, 0)),
                  pl.BlockSpec((1, GLA_DK), lambda i: (0, 0))],
        out_specs=pl.BlockSpec((tm, GLA_DK), lambda i: (i, 0)),
        compiler_params=_cparams(("arbitrary",)),
        name="gla_gate",
    )(xn, w1, w2, b_gate.reshape(1, GLA_DK).astype(F32))


def _gla_body(lc, q_ref, k_ref, v_ref, r_ref, la_ref, s0_ref, g_ref, o_ref, st_ref, s_ref):
    c = pl.program_id(2)

    @pl.when(c == 0)
    def _():
        s_ref[...] = s0_ref[...].astype(F32)

    la = la_ref[...]
    la_hi = la.astype(BF16)
    la_lo = (la - la_hi.astype(F32)).astype(BF16)
    row = lax.broadcasted_iota(jnp.int32, (lc, lc), 0)
    col = lax.broadcasted_iota(jnp.int32, (lc, lc), 1)
    tri = (col <= row).astype(BF16)
    cum = (jnp.dot(tri, la_hi, preferred_element_type=F32)
           + jnp.dot(tri, la_lo, preferred_element_type=F32))
    end = cum[lc - 1:lc, :]
    kd = (k_ref[...].astype(F32) * jnp.exp(end - cum)).astype(BF16)
    ones = jnp.ones((lc, 128), BF16)
    tn = (((0,), (0,)), ((), ()))
    end_col = (lax.dot_general(la_hi, ones, tn, preferred_element_type=F32)
               + lax.dot_general(la_lo, ones, tn, preferred_element_type=F32))
    decay = jnp.exp(end_col[:, 0:1])
    s_new = decay * s_ref[...] + lax.dot_general(kd, v_ref[...].astype(BF16), tn, preferred_element_type=F32)
    s_ref[...] = s_new
    qs = (q_ref[...].astype(F32) * (GLA_DK_HEAD ** -0.5)).astype(BF16)
    o = jnp.dot(qs, s_new.astype(BF16), preferred_element_type=F32)
    o = o * lax.rsqrt(jnp.mean(o * o, axis=-1, keepdims=True) + EPS)
    o = o * g_ref[...]
    o_ref[...] = (o * jax.nn.silu(r_ref[...].astype(F32))).astype(o_ref.dtype)

    @pl.when(c == pl.num_programs(2) - 1)
    def _():
        st_ref[...] = s_new


def _drop_alias_ref(body, n_in, *refs):
    return body(*refs[:n_in], *refs[n_in + 1:])


def gla_recurrence(proj, la, row0, nb, L, s0, norm_g, prev=None):
    ntok = proj.shape[0]
    lc = min(GLA_CHUNK, L)
    nc = L // lc
    base = row0 // lc
    kq = GLA_DK // GLA_DK_HEAD
    kv = 2 * GLA_DK // GLA_DV_HEAD
    kr = kv + GLA_HEADS
    rows = lambda b, h, c: base + b * nc + c
    in_specs = [pl.BlockSpec((lc, GLA_DK_HEAD), lambda b, h, c: (rows(b, h, c), h)),
                pl.BlockSpec((lc, GLA_DK_HEAD), lambda b, h, c: (rows(b, h, c), kq + h)),
                pl.BlockSpec((lc, GLA_DV_HEAD), lambda b, h, c: (rows(b, h, c), kv + h)),
                pl.BlockSpec((lc, GLA_DV_HEAD), lambda b, h, c: (rows(b, h, c), kr + h)),
                pl.BlockSpec((lc, GLA_DK_HEAD), lambda b, h, c: (rows(b, h, c), h)),
                pl.BlockSpec((None, None, GLA_DK_HEAD, GLA_DV_HEAD), lambda b, h, c: (b, h, 0, 0)),
                pl.BlockSpec((1, GLA_DV_HEAD), lambda b, h, c: (0, h))]
    args = [proj, proj, proj, proj, la, s0, norm_g.reshape(1, GLA_DV).astype(F32)]
    body = functools.partial(_gla_body, lc)
    aliases = {}
    if prev is not None:
        in_specs.append(pl.BlockSpec(memory_space=pl.ANY))
        args.append(prev)
        aliases = {len(args) - 1: 0}
        body = functools.partial(_drop_alias_ref, body, len(args) - 1)
    o, st = pl.pallas_call(
        body,
        out_shape=[jax.ShapeDtypeStruct((ntok, GLA_DV), BF16),
                   jax.ShapeDtypeStruct((nb, GLA_HEADS, GLA_DK_HEAD, GLA_DV_HEAD), F32)],
        grid=(nb, GLA_HEADS, nc),
        in_specs=in_specs,
        out_specs=[pl.BlockSpec((lc, GLA_DV_HEAD), lambda b, h, c: (rows(b, h, c), h)),
                   pl.BlockSpec((None, None, GLA_DK_HEAD, GLA_DV_HEAD), lambda b, h, c: (b, h, 0, 0))],
        scratch_shapes=[pltpu.VMEM((GLA_DK_HEAD, GLA_DV_HEAD), F32)],
        input_output_aliases=aliases,
        compiler_params=_cparams(("arbitrary", "arbitrary", "arbitrary")),
        name="gla_recurrence",
    )(*args)
    return o, st


def _attn_body(per_head, q_ref, k_ref, v_ref, o_ref):
    nt = (((1,), (1,)), ((), ()))
    scale = CA_HEAD_DIM ** -0.5
    heads = lambda ref: [ref[:, h * CA_HEAD_DIM:(h + 1) * CA_HEAD_DIM] for h in range(CA_HEADS)]

    def softmax(s):
        p = jnp.exp(s - jnp.max(s, axis=-1, keepdims=True))
        return p / jnp.sum(p, axis=-1, keepdims=True)

    if per_head:
        tl = q_ref.shape[0]
        k2 = k_ref[...].reshape(N_MEM * CA_HEADS, CA_HEAD_DIM).astype(BF16)
        v2 = v_ref[...].reshape(N_MEM * CA_HEADS, CA_HEAD_DIM).astype(BF16)
        q4 = jnp.concatenate(heads(q_ref), axis=0)
        s = lax.dot_general(q4, k2, nt, preferred_element_type=F32) * scale
        q_head = lax.broadcasted_iota(jnp.int32, s.shape, 0) // tl
        m_head = lax.broadcasted_iota(jnp.int32, s.shape, 1) % CA_HEADS
        p = softmax(jnp.where(q_head == m_head, s, -jnp.inf))
        o4 = jnp.dot(p.astype(BF16), v2, preferred_element_type=F32).astype(o_ref.dtype)
        for h in range(CA_HEADS):
            o_ref[:, h * CA_HEAD_DIM:(h + 1) * CA_HEAD_DIM] = o4[h * tl:(h + 1) * tl]
    else:
        for h, (qh, kh, vh) in enumerate(zip(heads(q_ref), heads(k_ref), heads(v_ref))):
            s = lax.dot_general(qh, kh.astype(BF16), nt, preferred_element_type=F32) * scale
            o = jnp.dot(softmax(s).astype(BF16), vh.astype(BF16), preferred_element_type=F32)
            o_ref[:, h * CA_HEAD_DIM:(h + 1) * CA_HEAD_DIM] = o.astype(o_ref.dtype)


def mem_attention(q, row0, nb, L, mk, mv, layer=None, prev=None):
    ntok = q.shape[0]
    tl = min(L, 512)
    nl = L // tl
    base = row0 // tl
    rows = lambda b, i: (base + b * nl + i, 0)
    if layer is None:
        mem_spec = pl.BlockSpec((None, N_MEM, D_MODEL), lambda b, i: (b, 0, 0))
    else:
        mem_spec = pl.BlockSpec((None, None, N_MEM, CA_HEADS, CA_HEAD_DIM), lambda b, i: (layer, b, 0, 0, 0))
    in_specs = [pl.BlockSpec((tl, D_MODEL), rows), mem_spec, mem_spec]
    args = [q, mk, mv]
    body = functools.partial(_attn_body, layer is not None)
    aliases = {}
    if prev is not None:
        in_specs.append(pl.BlockSpec(memory_space=pl.ANY))
        args.append(prev)
        aliases = {len(args) - 1: 0}
        body = functools.partial(_drop_alias_ref, body, len(args) - 1)
    return pl.pallas_call(
        body,
        out_shape=jax.ShapeDtypeStruct((ntok, D_MODEL), BF16),
        grid=(nb, nl),
        in_specs=in_specs,
        out_specs=pl.BlockSpec((tl, D_MODEL), rows),
        input_output_aliases=aliases,
        compiler_params=_cparams(("arbitrary", "arbitrary")),
        name="mem_attention",
    )(*args)


def _router_body(x_ref, g_ref, w_ref, b_ref, xn_ref, ids_ref, gates_ref):
    x = x_ref[...]
    ms = jnp.mean(x * x, axis=-1, keepdims=True)
    xn = (x * lax.rsqrt(ms + EPS)) * g_ref[...]
    xn_ref[...] = xn.astype(xn_ref.dtype)
    logits = jnp.dot(xn.astype(BF16), w_ref[...].astype(BF16), preferred_element_type=F32) + b_ref[...]
    tm = logits.shape[0]
    col = lax.broadcasted_iota(jnp.int32, (tm, ROUTER_LANES), 1).astype(F32)
    neg = jnp.float32(-jnp.inf)
    first = lambda mask: jnp.min(jnp.where(mask, col, float(ROUTER_LANES)), axis=-1, keepdims=True)
    gl = jnp.where(col < MOE_GROUPS, logits, neg)
    gmax = jnp.max(gl, axis=-1, keepdims=True)
    gidx = first(gl == gmax)
    g_w = 1.0 / jnp.sum(jnp.exp(gl - gmax), axis=-1, keepdims=True)
    lo = MOE_GROUPS + gidx * MOE_EPG
    el = jnp.where((col >= lo) & (col < lo + MOE_EPG), logits, neg)
    m1 = jnp.max(el, axis=-1, keepdims=True)
    i1 = first(el == m1)
    z = jnp.sum(jnp.exp(el - m1), axis=-1, keepdims=True)
    p1 = 1.0 / z
    el2 = jnp.where(col == i1, neg, el)
    m2 = jnp.max(el2, axis=-1, keepdims=True)
    i2 = first(el2 == m2)
    p2 = jnp.exp(m2 - m1) / z
    tot = p1 + p2
    ids = jnp.where(col == 0, i1 - MOE_GROUPS, jnp.where(col == 1, i2 - MOE_GROUPS, 0.0))
    ids_ref[...] = ids.astype(jnp.int32)
    gates_ref[...] = jnp.where(col == 0, g_w * p1 / tot, jnp.where(col == 1, g_w * p2 / tot, 0.0))


def moe_router(x, g, w_group, b_group, w_expert, b_expert, tm=256):
    m, d = x.shape
    npad = ROUTER_LANES - MOE_GROUPS - MOE_EXPERTS
    w = jnp.pad(jnp.concatenate([w_group, w_expert], axis=1).astype(F32), ((0, 0), (0, npad)))
    b = jnp.pad(jnp.concatenate([b_group, b_expert]).astype(F32), (0, npad)).reshape(1, ROUTER_LANES)
    row = lambda n: pl.BlockSpec((tm, n), lambda i: (i, 0))
    xn, ids, gates = pl.pallas_call(
        _router_body,
        out_shape=[jax.ShapeDtypeStruct((m, d), F32),
                   jax.ShapeDtypeStruct((m, ROUTER_LANES), jnp.int32),
                   jax.ShapeDtypeStruct((m, ROUTER_LANES), F32)],
        grid=(m // tm,),
        in_specs=[row(d), pl.BlockSpec((1, d), lambda i: (0, 0)),
                  pl.BlockSpec((d, ROUTER_LANES), lambda i: (0, 0)),
                  pl.BlockSpec((1, ROUTER_LANES), lambda i: (0, 0))],
        out_specs=[row(d), row(ROUTER_LANES), row(ROUTER_LANES)],
        compiler_params=_cparams(("arbitrary",)),
        name="moe_router",
    )(x, g.reshape(1, d).astype(F32), w, b)
    return xn, ids[:, :MOE_TOP_K], gates


def _gather_body(tm, src_ref, cnt_ref, x_hbm, o_ref, buf_ref, sem):
    i = pl.program_id(0)
    n_groups = cnt_ref[i]

    @pl.when(n_groups < tm // GATHER_GROUP)
    def _():
        buf_ref[...] = jnp.zeros_like(buf_ref)

    def copy(r):
        return pltpu.make_async_copy(x_hbm.at[pl.ds(src_ref[i * tm + r], 1), :],
                                     buf_ref.at[pl.ds(r, 1), :], sem)

    def start(g, _):
        for u in range(GATHER_GROUP):
            copy(g * GATHER_GROUP + u).start(priority=u % DMA_PRIORITIES)
        return 0

    def wait(g, _):
        for u in range(GATHER_GROUP):
            copy(g * GATHER_GROUP + u).wait()
        return 0

    lax.fori_loop(0, n_groups, start, 0)
    lax.fori_loop(0, n_groups, wait, 0)
    o_ref[...] = buf_ref[...].astype(o_ref.dtype)


def moe_gather(x, row_src, tile_groups, n_tiles, tm):
    d = x.shape[1]
    return pl.pallas_call(
        functools.partial(_gather_body, tm),
        out_shape=jax.ShapeDtypeStruct((n_tiles * tm, d), BF16),
        grid_spec=pltpu.PrefetchScalarGridSpec(
            num_scalar_prefetch=2,
            grid=(n_tiles,),
            in_specs=[pl.BlockSpec(memory_space=pl.ANY)],
            out_specs=pl.BlockSpec((tm, d), lambda i, src, cnt: (i, 0)),
            scratch_shapes=[pltpu.VMEM((tm, d), x.dtype), pltpu.SemaphoreType.DMA(())],
        ),
        compiler_params=_cparams(("arbitrary",)),
        name="moe_gather",
    )(row_src, tile_groups, x)


def _segment_weights(te_ref, seg_ref, meta_ref, copies, cast):
    c = pl.program_id(0)
    r = pl.program_id(1)
    nt_used = meta_ref[0]
    nseg = meta_ref[1]
    used = r < nt_used
    seg = seg_ref[MOE_EXPERTS + r]
    first = jnp.logical_and(used, jnp.logical_or(r == 0, te_ref[r] != te_ref[jnp.maximum(r - 1, 0)]))
    g = c * nseg + seg
    slot = lax.rem(g, 2)

    @pl.when(first)
    def _():
        @pl.when(g == 0)
        def _():
            for cp in copies(te_ref[r], c, 0):
                cp.start()

        for cp in copies(te_ref[r], c, slot):
            cp.wait()
        wraps = seg + 1 >= nseg
        nxt_seg = jnp.where(wraps, 0, seg + 1)
        nxt_c = jnp.where(wraps, c + 1, c)

        @pl.when(nxt_c < pl.num_programs(0))
        def _():
            for cp in copies(seg_ref[nxt_seg], nxt_c, 1 - slot):
                cp.start()

        cast(slot)

    return used


def _moe_up_body(layer, tf, te_ref, seg_ref, meta_ref, x_ref, w_hbm, h_ref, wbuf, wgb_ref, wub_ref, sem):
    f = MOE_HIDDEN

    def copies(e, c, slot):
        col = pl.multiple_of(c * tf, tf)
        return [pltpu.make_async_copy(w_hbm.at[layer, e, :, pl.ds(half * f + col, tf)], wbuf.at[slot, half],
                                      sem.at[slot, half]) for half in range(2)]

    def cast(slot):
        wgb_ref[...] = wbuf[slot, 0].astype(BF16)
        wub_ref[...] = wbuf[slot, 1].astype(BF16)

    used = _segment_weights(te_ref, seg_ref, meta_ref, copies, cast)

    @pl.when(used)
    def _():
        x = x_ref[...]
        gate = jnp.dot(x, wgb_ref[...], preferred_element_type=F32)
        up = jnp.dot(x, wub_ref[...], preferred_element_type=F32)
        h_ref[...] = (jax.nn.silu(gate) * up).astype(h_ref.dtype)

    @pl.when(jnp.logical_not(used))
    def _():
        h_ref[...] = jnp.zeros_like(h_ref)


def _moe_down_body(layer, tn, te_ref, seg_ref, meta_ref, h_ref, w_hbm, y_ref, wbuf, wdb_ref, sem):
    def copies(e, c, slot):
        col = pl.multiple_of(c * tn, tn)
        return [pltpu.make_async_copy(w_hbm.at[layer, e, :, pl.ds(col, tn)], wbuf.at[slot], sem.at[slot])]

    def cast(slot):
        wdb_ref[...] = wbuf[slot].astype(BF16)

    used = _segment_weights(te_ref, seg_ref, meta_ref, copies, cast)

    @pl.when(used)
    def _():
        y_ref[...] = jnp.dot(h_ref[...], wdb_ref[...], preferred_element_type=F32)

    @pl.when(jnp.logical_not(used))
    def _():
        y_ref[...] = jnp.zeros_like(y_ref)


def moe_experts(xs, tile_e, seg_info, meta, w_gate_up, w_down, layer, tm, tf=512, tnd=D_MODEL):
    p, d = xs.shape
    n_tiles = p // tm
    f = MOE_HIDDEN
    h = pl.pallas_call(
        functools.partial(_moe_up_body, layer, tf),
        out_shape=jax.ShapeDtypeStruct((p, f), BF16),
        grid_spec=pltpu.PrefetchScalarGridSpec(
            num_scalar_prefetch=3,
            grid=(f // tf, n_tiles),
            in_specs=[pl.BlockSpec((tm, d), lambda c, r, te, sg, mt: (r, 0)),
                      pl.BlockSpec(memory_space=pl.ANY)],
            out_specs=pl.BlockSpec((tm, tf), lambda c, r, te, sg, mt: (r, c)),
            scratch_shapes=[pltpu.VMEM((2, 2, d, tf), F32), pltpu.VMEM((d, tf), BF16), pltpu.VMEM((d, tf), BF16),
                            pltpu.SemaphoreType.DMA((2, 2))],
        ),
        compiler_params=_cparams(("arbitrary", "arbitrary")),
        name="moe_up",
    )(tile_e, seg_info, meta, xs, w_gate_up)
    return pl.pallas_call(
        functools.partial(_moe_down_body, layer, tnd),
        out_shape=jax.ShapeDtypeStruct((p, d), F32),
        grid_spec=pltpu.PrefetchScalarGridSpec(
            num_scalar_prefetch=3,
            grid=(d // tnd, n_tiles),
            in_specs=[pl.BlockSpec((tm, f), lambda c, r, te, sg, mt: (r, 0)),
                      pl.BlockSpec(memory_space=pl.ANY)],
            out_specs=pl.BlockSpec((tm, tnd), lambda c, r, te, sg, mt: (r, c)),
            scratch_shapes=[pltpu.VMEM((2, f, tnd), F32), pltpu.VMEM((f, tnd), BF16),
                            pltpu.SemaphoreType.DMA((2,))],
        ),
        compiler_params=_cparams(("arbitrary", "arbitrary")),
        name="moe_down",
    )(tile_e, seg_info, meta, h, w_down)


def _combine_body(tt, pos_ref, x_ref, g_ref, y_hbm, o_ref, buf_ref, sem):
    i = pl.program_id(0)

    def copy(t, k):
        return pltpu.make_async_copy(y_hbm.at[pl.ds(pos_ref[(i * tt + t) * MOE_TOP_K + k], 1), :],
                                     buf_ref.at[k, pl.ds(t, 1), :], sem)

    def start(t, _):
        for k in range(MOE_TOP_K):
            copy(t, k).start(priority=k % DMA_PRIORITIES)
        return 0

    def wait(t, _):
        for k in range(MOE_TOP_K):
            copy(t, k).wait()
        return 0

    lax.fori_loop(0, tt, start, 0)
    lax.fori_loop(0, tt, wait, 0)
    g = g_ref[...]
    y = g[:, 0:1] * buf_ref[0] + g[:, 1:2] * buf_ref[1]
    o_ref[...] = x_ref[...] + y


def moe_combine(x, gates, pos, y_sorted, tt=256):
    m, d = x.shape
    return pl.pallas_call(
        functools.partial(_combine_body, tt),
        out_shape=jax.ShapeDtypeStruct((m, d), F32),
        grid_spec=pltpu.PrefetchScalarGridSpec(
            num_scalar_prefetch=1,
            grid=(m // tt,),
            in_specs=[pl.BlockSpec((tt, d), lambda i, pos: (i, 0)),
                      pl.BlockSpec((tt, ROUTER_LANES), lambda i, pos: (i, 0)),
                      pl.BlockSpec(memory_space=pl.ANY)],
            out_specs=pl.BlockSpec((tt, d), lambda i, pos: (i, 0)),
            scratch_shapes=[pltpu.VMEM((MOE_TOP_K, tt, d), F32), pltpu.SemaphoreType.DMA(())],
        ),
        compiler_params=_cparams(("arbitrary",)),
        name="moe_combine",
    )(pos, x, gates, y_sorted)


def _moe_plan(ids, tm):
    t = ids.shape[0]
    n = t * MOE_TOP_K
    n_tiles = -(-(n + MOE_EXPERTS * (tm - 1)) // tm)
    i32 = jnp.int32
    flat_e = ids.reshape(n)
    order = jnp.argsort(flat_e, stable=True).astype(i32)
    inv = jnp.argsort(order).astype(i32)
    se = flat_e[order]
    experts = jnp.arange(MOE_EXPERTS, dtype=i32)
    starts = jnp.searchsorted(se, experts, side="left", method="compare_all").astype(i32)
    counts = jnp.searchsorted(se, experts, side="right", method="compare_all").astype(i32) - starts
    tile_count = (counts + tm - 1) // tm
    tile_end = jnp.cumsum(tile_count).astype(i32)
    tile_start = tile_end - tile_count
    pos = tile_start[flat_e] * tm + (inv - starts[flat_e])
    tiles = jnp.arange(n_tiles, dtype=i32)
    tile_e = jnp.minimum(jnp.searchsorted(tile_end, tiles, side="right", method="compare_all"),
                         MOE_EXPERTS - 1).astype(i32)
    tile_valid = jnp.clip(counts[tile_e] - (tiles - tile_start[tile_e]) * tm, 0, tm)
    tile_groups = (tile_valid + GATHER_GROUP - 1) // GATHER_GROUP
    within = jnp.arange(tm, dtype=i32)[None, :]
    slot = (starts[tile_e] + (tiles - tile_start[tile_e]) * tm)[:, None] + within
    src = order[jnp.clip(slot, 0, n - 1)] // MOE_TOP_K
    row_src = jnp.where(within < tile_valid[:, None], src, 0).reshape(n_tiles * tm)
    nonempty = tile_count > 0
    seg_of_expert = jnp.cumsum(nonempty.astype(i32)).astype(i32) - 1
    seg_expert = jnp.argsort(jnp.where(nonempty, 0, 1).astype(i32), stable=True).astype(i32)
    seg_info = jnp.concatenate([seg_expert, seg_of_expert[tile_e]])
    meta = jnp.stack([tile_end[-1], jnp.sum(nonempty.astype(i32))]).astype(i32)
    return row_src, pos.astype(i32), tile_e, seg_info, meta, tile_groups.astype(i32), n_tiles


def hier_moe(x, g, layer, w_group, b_group, w_expert, b_expert, w_gate_up, w_down):
    xn, ids, gates = moe_router(x, g, w_group, b_group, w_expert, b_expert)
    row_src, pos, tile_e, seg_info, meta, tile_groups, n_tiles = _moe_plan(ids, MOE_TILE)
    xs = moe_gather(xn, row_src, tile_groups, n_tiles, MOE_TILE)
    y = moe_experts(xs, tile_e, seg_info, meta, w_gate_up, w_down, layer, MOE_TILE)
    return moe_combine(x, gates, pos, y)


def kernel(x_prompt, x_sample, mem_prompt, state_ssm_re, state_ssm_im, state_gla, cache_mem_k, cache_mem_v, norm_mixer, norm_ca, norm_moe, norm_final, ssm_lambda_re, ssm_lambda_im, ssm_log_dt, ssm_b_re, ssm_b_im, ssm_c_re, ssm_c_im, ssm_d, ssm_w_glu, gla_w_in, gla_w_gate_up, gla_b_gate, gla_norm, gla_w_out, ca_mem_norm, ca_w_q, ca_w_kv, ca_w_o, moe_w_group, moe_b_group, moe_w_expert, moe_b_expert, moe_w_gate_up, moe_w_down):
    bp, lp, d = x_prompt.shape
    bs, ls, _ = x_sample.shape
    np_ = bp * lp
    ns_ = bs * ls

    mem = mem_prompt.reshape(bp * N_MEM, d)
    pk, pv = [], []
    for i in range(DEPTH):
        mn = rmsnorm(mem, ca_mem_norm[i])
        kv = matmul(mn, ca_w_kv, i, out_dtype=F32)
        pk.append(kv[:, :d].reshape(bp, N_MEM, d))
        pv.append(kv[:, d:].reshape(bp, N_MEM, d))
    prompt_mem_k = jnp.stack(pk).reshape(DEPTH, bp, N_MEM, CA_HEADS, CA_HEAD_DIM)
    prompt_mem_v = jnp.stack(pv).reshape(DEPTH, bp, N_MEM, CA_HEADS, CA_HEAD_DIM)

    x = jnp.concatenate([x_prompt.reshape(np_, d), x_sample.reshape(ns_, d)], axis=0)
    zero_ssm = jnp.zeros((bp, SSM_GROUPS, SSM_STATE), F32)
    zero_gla = jnp.zeros((bp, GLA_HEADS, GLA_DK_HEAD, GLA_DV_HEAD), F32)
    zero_act = jnp.zeros((np_ + ns_, d), BF16)
    p_re, p_im, s_re, s_im, p_gla, s_gla = [], [], [], [], [], []

    for i in range(DEPTH):
        j = i // 2
        if i % 2 == 0:
            xn_p = rmsnorm(x, norm_mixer[i], out_dtype=F32, row0=0, nrows=np_)
            xn_s = rmsnorm(x, norm_mixer[i], out_dtype=F32, row0=np_, nrows=ns_)
            prm = (ssm_lambda_re[j], ssm_lambda_im[j], ssm_log_dt[j], ssm_b_re[j], ssm_b_im[j],
                   ssm_c_re[j], ssm_c_im[j], ssm_d[j])
            zp, hr, hi = s5_layer(xn_p.reshape(bp, lp, d), zero_ssm, zero_ssm, prm)
            p_re.append(hr)
            p_im.append(hi)
            zs, hr, hi = s5_layer(xn_s.reshape(bs, ls, d), state_ssm_re[j], state_ssm_im[j], prm)
            s_re.append(hr)
            s_im.append(hi)
            z = jnp.concatenate([zp, zs], axis=0)
            x = matmul(z, ssm_w_glu, j, mode="glu_res", res=x, n_out=d, tn=256)
        else:
            xn = rmsnorm(x, norm_mixer[i])
            w_in_t = jnp.swapaxes(gla_w_in, 1, 2)
            proj = matmul(xn, w_in_t, j, out_dtype=F32, n_out=GLA_MAIN, w_is_nk=True)
            la = gla_gate(xn, w_in_t[j, GLA_MAIN:, :], gla_w_gate_up[j], gla_b_gate[j])
            o, st = gla_recurrence(proj, la, 0, bp, lp, zero_gla, gla_norm[j], prev=zero_act)
            p_gla.append(st)
            o, st = gla_recurrence(proj, la, np_, bs, ls, state_gla[j], gla_norm[j], prev=o)
            s_gla.append(st)
            x = matmul(o, gla_w_out, j, mode="res", res=x)

        xn = rmsnorm(x, norm_ca[i])
        q = matmul(xn, ca_w_q, i, out_dtype=BF16)
        att = mem_attention(q, 0, bp, lp, pk[i], pv[i], prev=zero_act)
        att = mem_attention(q, np_, bs, ls, cache_mem_k, cache_mem_v, layer=i, prev=att)
        x = matmul(att, ca_w_o, i, mode="res", res=x)

        x = hier_moe(x, norm_moe[i], i, moe_w_group[i], moe_b_group[i], moe_w_expert[i], moe_b_expert[i],
                     moe_w_gate_up, moe_w_down)

    y_prompt = rmsnorm(x, norm_final, out_dtype=F32, row0=0, nrows=np_)
    y_sample = rmsnorm(x, norm_final, out_dtype=F32, row0=np_, nrows=ns_)
    return (y_prompt.reshape(bp, lp, d), y_sample.reshape(bs, ls, d),
            jnp.stack(p_re), jnp.stack(p_im), jnp.stack(p_gla), prompt_mem_k, prompt_mem_v,
            jnp.stack(s_re), jnp.stack(s_im), jnp.stack(s_gla))
```

```python
import functools
import math

import jax
import jax.numpy as jnp
import numpy as np
from jax import lax
from jax.experimental import pallas as pl
from jax.experimental.pallas import tpu as pltpu

F32 = jnp.float32
BF16 = jnp.bfloat16

EPS = 1e-6
D_MODEL = 4096
DEPTH = 2

SSM_GROUP_CH = 16
SSM_GROUPS = D_MODEL // SSM_GROUP_CH
SSM_STATE = 64
SSM_SEQS = 32
SSM_SEQ_TILE = 32
SSM_SCAN_UNROLL = True
SSM_BLK_CH = 128
SSM_BLK_GROUPS = SSM_BLK_CH // SSM_GROUP_CH
SSM_NBLK = D_MODEL // SSM_BLK_CH
SSM_BLK_STATE = SSM_BLK_GROUPS * SSM_STATE

GLA_HEADS = 4
GLA_DK = D_MODEL // 2
GLA_DV = D_MODEL
GLA_DK_HEAD = GLA_DK // GLA_HEADS
GLA_DV_HEAD = GLA_DV // GLA_HEADS
GLA_GATE_RANK = 16
GLA_TAU = 16.0
GLA_MAIN = 2 * GLA_DK + 2 * GLA_DV
GLA_CHUNK = 64
GLA_HEADS_PER_STEP = 4

N_MEM = 256
CA_HEADS = 4
CA_HEAD_DIM = D_MODEL // CA_HEADS

MOE_GROUPS = 4
MOE_EPG = 8
MOE_EXPERTS = MOE_GROUPS * MOE_EPG
MOE_TOP_K = 2
MOE_HIDDEN = D_MODEL // 4
MOE_TILE = 256
GATHER_GROUP = 8
DMA_PRIORITIES = 2
ROUTER_LANES = 128

DENSE_TM = 1088
SUBLANES = 8
VMEM_LIMIT = 56 * 1024 * 1024


def _cparams(sem):
    return pltpu.CompilerParams(dimension_semantics=sem, vmem_limit_bytes=VMEM_LIMIT)


def _rms_body(n_add, emit_sum, x_ref, *refs):
    adds = refs[:n_add]
    g_ref = refs[n_add]
    outs = refs[n_add + 1:]
    x = x_ref[...].astype(F32)
    for a in adds:
        x = x + a[...].astype(F32)
    ms = jnp.mean(x * x, axis=-1, keepdims=True)
    xn = (x * lax.rsqrt(ms + EPS)) * g_ref[...]
    if emit_sum:
        outs[0][...] = x
        outs[1][...] = xn.astype(outs[1].dtype)
    else:
        outs[0][...] = xn.astype(outs[0].dtype)


def rmsnorm(x, g, adds=(), out_dtype=BF16, emit_sum=False, tm=256, row0=0, nrows=None):
    d = x.shape[1]
    m = x.shape[0] if nrows is None else nrows
    assert m % tm == 0 and row0 % tm == 0
    b0 = row0 // tm
    src = pl.BlockSpec((tm, d), lambda i: (b0 + i, 0))
    row = pl.BlockSpec((tm, d), lambda i: (i, 0))
    out_shape = [jax.ShapeDtypeStruct((m, d), out_dtype)]
    out_specs = [row]
    if emit_sum:
        out_shape.insert(0, jax.ShapeDtypeStruct((m, d), F32))
        out_specs.insert(0, row)
    res = pl.pallas_call(
        functools.partial(_rms_body, len(adds), emit_sum),
        out_shape=out_shape,
        grid=(m // tm,),
        in_specs=[src] * (1 + len(adds)) + [pl.BlockSpec((1, d), lambda i: (0, 0))],
        out_specs=out_specs,
        compiler_params=_cparams(("arbitrary",)),
        name="rmsnorm",
    )(x, *adds, g.reshape(1, d).astype(F32))
    return res if emit_sum else res[0]


def _mm_body(mode, w_is_nk, x_ref, *refs):
    if mode == "glu_res":
        w_ref, w2_ref, res_ref, o_ref, wb_ref, wb2_ref = refs
    elif mode == "res":
        w_ref, res_ref, o_ref, wb_ref = refs
    else:
        w_ref, o_ref, wb_ref = refs

    @pl.when(pl.program_id(1) == 0)
    def _():
        wb_ref[...] = w_ref[...].astype(BF16)
        if mode == "glu_res":
            wb2_ref[...] = w2_ref[...].astype(BF16)

    x = x_ref[...]
    if w_is_nk:
        acc = lax.dot_general(x, wb_ref[...], (((1,), (1,)), ((), ())), preferred_element_type=F32)
    else:
        acc = jnp.dot(x, wb_ref[...], preferred_element_type=F32)
    if mode == "glu_res":
        gate = jnp.dot(x, wb2_ref[...], preferred_element_type=F32)
        o_ref[...] = res_ref[...] + acc * jax.nn.sigmoid(gate)
    elif mode == "res":
        o_ref[...] = res_ref[...] + acc
    else:
        o_ref[...] = acc.astype(o_ref.dtype)


def matmul(x, w, layer, mode="plain", res=None, out_dtype=F32, n_out=None, tm=DENSE_TM, tn=512, w_is_nk=False):
    m, k = x.shape
    n = n_out if n_out is not None else w.shape[1 if w_is_nk else 2]
    if m % tm:
        tm = m
    assert m % tm == 0 and n % tn == 0 and not (w_is_nk and mode != "plain")
    nb = n // tn
    if w_is_nk:
        w_spec, w_tile = pl.BlockSpec((None, tn, k), lambda j, i: (layer, j, 0)), (tn, k)
    else:
        w_spec, w_tile = pl.BlockSpec((None, k, tn), lambda j, i: (layer, 0, j)), (k, tn)
    in_specs = [pl.BlockSpec((tm, k), lambda j, i: (i, 0)), w_spec]
    args = [x, w]
    scratch = [pltpu.VMEM(w_tile, BF16)]
    if mode == "glu_res":
        in_specs.append(pl.BlockSpec((None, k, tn), lambda j, i: (layer, 0, nb + j)))
        args.append(w)
        scratch.append(pltpu.VMEM((k, tn), BF16))
    if mode in ("res", "glu_res"):
        in_specs.append(pl.BlockSpec((tm, tn), lambda j, i: (i, j)))
        args.append(res)
        out_dtype = F32
    return pl.pallas_call(
        functools.partial(_mm_body, mode, w_is_nk),
        out_shape=jax.ShapeDtypeStruct((m, n), out_dtype),
        grid=(nb, m // tm),
        in_specs=in_specs,
        out_specs=pl.BlockSpec((tm, tn), lambda j, i: (i, j)),
        scratch_shapes=scratch,
        compiler_params=_cparams(("arbitrary", "arbitrary")),
        name="matmul_" + mode,
    )(*args)


def _s5_body(lt, emit_y, u_ref, h0_ref, ar_ref, ai_ref, bd_ref, *refs):
    if emit_y:
        cbd_ref, d_ref, z_ref, ht_ref, h_ref, bu_ref, hs_ref = refs
    else:
        ht_ref, h_ref, bu_ref = refs
    tb = pl.program_id(2)
    ns = SSM_BLK_STATE

    @pl.when(tb == 0)
    def _():
        h_ref[...] = h0_ref[...]

    st = SSM_SEQ_TILE
    groups = range(st // SUBLANES)
    u = u_ref[...].reshape(lt * st, SSM_BLK_CH)
    bu_ref[...] = jnp.dot(u.astype(BF16), bd_ref[...], preferred_element_type=F32)
    ar = jnp.broadcast_to(ar_ref[...], (SUBLANES, ns))
    ai = jnp.broadcast_to(ai_ref[...], (SUBLANES, ns))

    def step(t, carry):
        r0 = t * st
        out = []
        for g in groups:
            hr, hi = carry[g]
            rows = pl.ds(pl.multiple_of(r0 + g * SUBLANES, SUBLANES), SUBLANES)
            nr = ar * hr - ai * hi + bu_ref[rows, 0:ns]
            ni = ar * hi + ai * hr + bu_ref[rows, ns:2 * ns]
            if emit_y:
                hs_ref[rows, 0:ns] = nr
                hs_ref[rows, ns:2 * ns] = ni
            out.append((nr, ni))
        return tuple(out)

    grp = lambda g: slice(g * SUBLANES, (g + 1) * SUBLANES)
    init = tuple((h_ref[grp(g), 0:ns], h_ref[grp(g), ns:2 * ns]) for g in groups)
    final = lax.fori_loop(0, lt, step, init, unroll=SSM_SCAN_UNROLL)
    for g in groups:
        hr, hi = final[g]
        h_ref[grp(g), 0:ns] = hr
        h_ref[grp(g), ns:2 * ns] = hi
        ht_ref[grp(g), 0:ns] = hr
        ht_ref[grp(g), ns:2 * ns] = hi
    if emit_y:
        y = jnp.dot(hs_ref[...].astype(BF16), cbd_ref[...], preferred_element_type=F32)
        y = y + d_ref[...] * u
        z_ref[...] = jax.nn.gelu(y, approximate=True).reshape(lt, st, SSM_BLK_CH)


def _s5_scan(ut, h0, ar, ai, bd, cbd, dsk, emit_y):
    ltot = ut.shape[0]
    lt = min(ltot, 64)
    assert ltot % lt == 0
    ns2 = 2 * SSM_BLK_STATE
    st = SSM_SEQ_TILE
    nsg = SSM_SEQS // st
    in_specs = [
        pl.BlockSpec((lt, st, SSM_BLK_CH), lambda c, s, t: (t, s, c)),
        pl.BlockSpec((None, st, ns2), lambda c, s, t: (c, s, 0)),
        pl.BlockSpec((None, 1, SSM_BLK_STATE), lambda c, s, t: (c, 0, 0)),
        pl.BlockSpec((None, 1, SSM_BLK_STATE), lambda c, s, t: (c, 0, 0)),
        pl.BlockSpec((None, SSM_BLK_CH, ns2), lambda c, s, t: (c, 0, 0)),
    ]
    args = [ut, h0, ar, ai, bd]
    ht_shape = jax.ShapeDtypeStruct((SSM_NBLK, SSM_SEQS, ns2), F32)
    ht_spec = pl.BlockSpec((None, st, ns2), lambda c, s, t: (c, s, 0))
    scratch = [pltpu.VMEM((st, ns2), F32), pltpu.VMEM((lt * st, ns2), F32)]
    if emit_y:
        in_specs += [pl.BlockSpec((None, ns2, SSM_BLK_CH), lambda c, s, t: (c, 0, 0)),
                     pl.BlockSpec((None, 1, SSM_BLK_CH), lambda c, s, t: (c, 0, 0))]
        args += [cbd, dsk]
        out_shape = [jax.ShapeDtypeStruct(ut.shape, F32), ht_shape]
        out_specs = [pl.BlockSpec((lt, st, SSM_BLK_CH), lambda c, s, t: (t, s, c)), ht_spec]
        scratch.append(pltpu.VMEM((lt * st, ns2), F32))
    else:
        out_shape = [ht_shape]
        out_specs = [ht_spec]
    res = pl.pallas_call(
        functools.partial(_s5_body, lt, emit_y),
        out_shape=out_shape,
        grid=(SSM_NBLK, nsg, ltot // lt),
        in_specs=in_specs,
        out_specs=out_specs,
        scratch_shapes=scratch,
        compiler_params=_cparams(("arbitrary", "arbitrary", "arbitrary")),
        name="s5_scan_y" if emit_y else "s5_scan_state",
    )(*args)
    return (res[0], res[1]) if emit_y else (None, res[0])


def _s5_carry_body(nb, nseg, e_ref, h0_ref, ar_ref, ai_ref, hs_ref, he_ref):
    ns = SSM_BLK_STATE
    ar = ar_ref[...]
    ai = ai_ref[...]
    for b in range(nb):
        hr = h0_ref[b:b + 1, 0:ns]
        hi = h0_ref[b:b + 1, ns:2 * ns]
        for j in range(nseg):
            s = b * nseg + j
            hs_ref[s:s + 1, 0:ns] = hr
            hs_ref[s:s + 1, ns:2 * ns] = hi
            er = e_ref[s:s + 1, 0:ns]
            ei = e_ref[s:s + 1, ns:2 * ns]
            hr, hi = ar * hr - ai * hi + er, ar * hi + ai * hr + ei
        he_ref[b:b + 1, 0:ns] = hr
        he_ref[b:b + 1, ns:2 * ns] = hi


def _s5_carry(e0, h0, ar_seg, ai_seg, nb, nseg):
    ns2 = 2 * SSM_BLK_STATE
    return pl.pallas_call(
        functools.partial(_s5_carry_body, nb, nseg),
        out_shape=[jax.ShapeDtypeStruct((SSM_NBLK, SSM_SEQS, ns2), F32),
                   jax.ShapeDtypeStruct((SSM_NBLK, nb, ns2), F32)],
        grid=(SSM_NBLK,),
        in_specs=[pl.BlockSpec((None, SSM_SEQS, ns2), lambda c: (c, 0, 0)),
                  pl.BlockSpec((None, nb, ns2), lambda c: (c, 0, 0)),
                  pl.BlockSpec((None, 1, SSM_BLK_STATE), lambda c: (c, 0, 0)),
                  pl.BlockSpec((None, 1, SSM_BLK_STATE), lambda c: (c, 0, 0))],
        out_specs=[pl.BlockSpec((None, SSM_SEQS, ns2), lambda c: (c, 0, 0)),
                   pl.BlockSpec((None, nb, ns2), lambda c: (c, 0, 0))],
        compiler_params=_cparams(("arbitrary",)),
        name="s5_carry",
    )(e0, h0, ar_seg, ai_seg)


def _s5_params(lam_re, lam_im, log_dt, b_re, b_im, c_re, c_im, d_skip, seg_len):
    dt = jnp.exp(log_dt.astype(F32))[:, None]
    lr = lam_re.astype(F32)
    li = lam_im.astype(F32)
    mag = jnp.exp(lr * dt)
    ab_re = mag * jnp.cos(li * dt)
    ab_im = mag * jnp.sin(li * dt)
    nr = ab_re - 1.0
    ni = ab_im
    den = lr * lr + li * li
    f_re = (nr * lr + ni * li) / den
    f_im = (ni * lr - nr * li) / den
    br = b_re.astype(F32)
    bi = b_im.astype(F32)
    bb_re = f_re[..., None] * br - f_im[..., None] * bi
    bb_im = f_re[..., None] * bi + f_im[..., None] * br
    mag_s = jnp.exp(lr * dt * seg_len)
    as_re = mag_s * jnp.cos(li * dt * seg_len)
    as_im = mag_s * jnp.sin(li * dt * seg_len)
    eye = jnp.eye(SSM_BLK_GROUPS, dtype=F32)

    def in_proj(w):
        w = w.reshape(SSM_NBLK, SSM_BLK_GROUPS, SSM_STATE, SSM_GROUP_CH).transpose(0, 1, 3, 2)
        return jnp.einsum("bjcp,jk->bjckp", w, eye).reshape(SSM_NBLK, SSM_BLK_CH, SSM_BLK_STATE)

    def out_proj(w):
        w = w.reshape(SSM_NBLK, SSM_BLK_GROUPS, SSM_GROUP_CH, SSM_STATE).transpose(0, 1, 3, 2)
        return jnp.einsum("bjpc,jk->bjpkc", w, eye).reshape(SSM_NBLK, SSM_BLK_STATE, SSM_BLK_CH)

    bd = jnp.concatenate([in_proj(bb_re), in_proj(bb_im)], axis=-1).astype(BF16)
    cbd = jnp.concatenate([out_proj(c_re.astype(F32)), -out_proj(c_im.astype(F32))], axis=1).astype(BF16)
    blk = lambda v: v.reshape(SSM_NBLK, 1, SSM_BLK_STATE)
    return (blk(ab_re), blk(ab_im), blk(as_re), blk(as_im), bd, cbd,
            d_skip.astype(F32).reshape(SSM_NBLK, 1, SSM_BLK_CH))


def _state_to_blocks(h_re, h_im):
    nb = h_re.shape[0]
    f = lambda h: h.astype(F32).reshape(nb, SSM_NBLK, SSM_BLK_STATE).transpose(1, 0, 2)
    return jnp.concatenate([f(h_re), f(h_im)], axis=-1)


def _blocks_to_state(hb):
    nb = hb.shape[1]
    f = lambda h: h.transpose(1, 0, 2).reshape(nb, SSM_GROUPS, SSM_STATE)
    return f(hb[..., :SSM_BLK_STATE]), f(hb[..., SSM_BLK_STATE:])


def s5_layer(u, h0_re, h0_im, prm):
    nb, L, d = u.shape
    nseg = SSM_SEQS // nb
    seg = L // nseg
    lam_re, lam_im, log_dt, b_re, b_im, c_re, c_im, d_skip = prm
    ar, ai, as_re, as_im, bd, cbd, dsk = _s5_params(lam_re, lam_im, log_dt, b_re, b_im, c_re, c_im, d_skip, seg)
    ut = u.reshape(nb * nseg, seg, d).transpose(1, 0, 2)
    h0b = _state_to_blocks(h0_re, h0_im)
    if nseg == 1:
        zt, hT = _s5_scan(ut, h0b, ar, ai, bd, cbd, dsk, True)
    else:
        zero = jnp.zeros((SSM_NBLK, SSM_SEQS, 2 * SSM_BLK_STATE), F32)
        _, e0 = _s5_scan(ut, zero, ar, ai, bd, None, None, False)
        hstart, hT = _s5_carry(e0, h0b, as_re, as_im, nb, nseg)
        zt, _ = _s5_scan(ut, hstart, ar, ai, bd, cbd, dsk, True)
    z = zt.astype(BF16).transpose(1, 0, 2).reshape(nb * L, d)
    hr, hi = _blocks_to_state(hT)
    return z, hr, hi


def _gla_gate_body(x_ref, w1_ref, w2_ref, b_ref, o_ref):
    glr = lax.dot_general(x_ref[...], w1_ref[...].astype(BF16), (((1,), (1,)), ((), ())),
                          preferred_element_type=F32)
    logits = jnp.dot(glr.astype(BF16), w2_ref[...].astype(BF16), preferred_element_type=F32) + b_ref[...]
    o_ref[...] = jax.nn.log_sigmoid(logits) / GLA_TAU


def gla_gate(xn, w_glr_t, w_gate_up, b_gate, tm=512):
    m, k = xn.shape
    rp = 128
    w1 = jnp.pad(w_glr_t.astype(F32), ((0, rp - GLA_GATE_RANK), (0, 0)))
    w2 = jnp.pad(w_gate_up.astype(F32), ((0, rp - GLA_GATE_RANK), (0, 0)))
    return pl.pallas_call(
        _gla_gate_body,
        out_shape=jax.ShapeDtypeStruct((m, GLA_DK), F32),
        grid=(m // tm,),
        in_specs=[pl.BlockSpec((tm, k), lambda i: (i, 0)),
                  pl.BlockSpec((rp, k), lambda i: (0, 0)),
                  pl.BlockSpec((rp, GLA_DK), lambda i: (0, 0)),
                  pl.BlockSpec((1, GLA_DK), lambda i: (0, 0))],
        out_specs=pl.BlockSpec((tm, GLA_DK), lambda i: (i, 0)),
        compiler_params=_cparams(("arbitrary",)),
        name="gla_gate",
    )(xn, w1, w2, b_gate.reshape(1, GLA_DK).astype(F32))


def _gla_body(lc, q_ref, k_ref, v_ref, r_ref, la_ref, s0_ref, g_ref, o_ref, st_ref, s_ref):
    c = pl.program_id(2)

    @pl.when(c == 0)
    def _():
        s_ref[...] = s0_ref[...].astype(F32)

    row = lax.broadcasted_iota(jnp.int32, (lc, lc), 0)
    col = lax.broadcasted_iota(jnp.int32, (lc, lc), 1)
    tri = (col <= row).astype(BF16)
    ones = jnp.ones((lc, 128), BF16)
    tn = (((0,), (0,)), ((), ()))
    for hh in range(GLA_HEADS_PER_STEP):
        ks = slice(hh * GLA_DK_HEAD, (hh + 1) * GLA_DK_HEAD)
        vs = slice(hh * GLA_DV_HEAD, (hh + 1) * GLA_DV_HEAD)
        la = la_ref[:, ks]
        la_hi = la.astype(BF16)
        la_lo = (la - la_hi.astype(F32)).astype(BF16)
        cum = (jnp.dot(tri, la_hi, preferred_element_type=F32)
               + jnp.dot(tri, la_lo, preferred_element_type=F32))
        end = cum[lc - 1:lc, :]
        kd = (k_ref[:, ks].astype(F32) * jnp.exp(end - cum)).astype(BF16)
        end_col = (lax.dot_general(la_hi, ones, tn, preferred_element_type=F32)
                   + lax.dot_general(la_lo, ones, tn, preferred_element_type=F32))
        decay = jnp.exp(end_col[:, 0:1])
        s_new = decay * s_ref[hh] + lax.dot_general(kd, v_ref[:, vs].astype(BF16), tn, preferred_element_type=F32)
        s_ref[hh] = s_new
        qs = (q_ref[:, ks].astype(F32) * (GLA_DK_HEAD ** -0.5)).astype(BF16)
        o = jnp.dot(qs, s_new.astype(BF16), preferred_element_type=F32)
        o = o * lax.rsqrt(jnp.mean(o * o, axis=-1, keepdims=True) + EPS)
        o = o * g_ref[:, vs]
        o_ref[:, vs] = (o * jax.nn.silu(r_ref[:, vs].astype(F32))).astype(o_ref.dtype)

    @pl.when(c == pl.num_programs(2) - 1)
    def _():
        st_ref[...] = s_ref[...]


def _drop_alias_ref(body, n_in, *refs):
    return body(*refs[:n_in], *refs[n_in + 1:])


def gla_recurrence(proj, la, row0, nb, L, s0, norm_g, prev=None):
    ntok = proj.shape[0]
    lc = min(GLA_CHUNK, L)
    nc = L // lc
    base = row0 // lc
    hp = GLA_HEADS_PER_STEP
    wk, wv = hp * GLA_DK_HEAD, hp * GLA_DV_HEAD
    kq = GLA_DK // wk
    kv = 2 * GLA_DK // wv
    kr = kv + GLA_DV // wv
    rows = lambda b, h, c: base + b * nc + c
    state_spec = pl.BlockSpec((None, hp, GLA_DK_HEAD, GLA_DV_HEAD), lambda b, h, c: (b, h, 0, 0))
    in_specs = [pl.BlockSpec((lc, wk), lambda b, h, c: (rows(b, h, c), h)),
                pl.BlockSpec((lc, wk), lambda b, h, c: (rows(b, h, c), kq + h)),
                pl.BlockSpec((lc, wv), lambda b, h, c: (rows(b, h, c), kv + h)),
                pl.BlockSpec((lc, wv), lambda b, h, c: (rows(b, h, c), kr + h)),
                pl.BlockSpec((lc, wk), lambda b, h, c: (rows(b, h, c), h)),
                state_spec,
                pl.BlockSpec((1, wv), lambda b, h, c: (0, h))]
    args = [proj, proj, proj, proj, la, s0, norm_g.reshape(1, GLA_DV).astype(F32)]
    body = functools.partial(_gla_body, lc)
    aliases = {}
    if prev is not None:
        in_specs.append(pl.BlockSpec(memory_space=pl.ANY))
        args.append(prev)
        aliases = {len(args) - 1: 0}
        body = functools.partial(_drop_alias_ref, body, len(args) - 1)
    o, st = pl.pallas_call(
        body,
        out_shape=[jax.ShapeDtypeStruct((ntok, GLA_DV), BF16),
                   jax.ShapeDtypeStruct((nb, GLA_HEADS, GLA_DK_HEAD, GLA_DV_HEAD), F32)],
        grid=(nb, GLA_HEADS // hp, nc),
        in_specs=in_specs,
        out_specs=[pl.BlockSpec((lc, wv), lambda b, h, c: (rows(b, h, c), h)), state_spec],
        scratch_shapes=[pltpu.VMEM((hp, GLA_DK_HEAD, GLA_DV_HEAD), F32)],
        input_output_aliases=aliases,
        compiler_params=_cparams(("arbitrary", "arbitrary", "arbitrary")),
        name="gla_recurrence",
    )(*args)
    return o, st


def _attn_body(per_head, q_ref, k_ref, v_ref, o_ref):
    nt = (((1,), (1,)), ((), ()))
    scale = CA_HEAD_DIM ** -0.5
    heads = lambda ref: [ref[:, h * CA_HEAD_DIM:(h + 1) * CA_HEAD_DIM] for h in range(CA_HEADS)]

    def softmax(s):
        p = jnp.exp(s - jnp.max(s, axis=-1, keepdims=True))
        return p / jnp.sum(p, axis=-1, keepdims=True)

    if per_head:
        tl = q_ref.shape[0]
        k2 = k_ref[...].reshape(N_MEM * CA_HEADS, CA_HEAD_DIM).astype(BF16)
        v2 = v_ref[...].reshape(N_MEM * CA_HEADS, CA_HEAD_DIM).astype(BF16)
        q4 = jnp.concatenate(heads(q_ref), axis=0)
        s = lax.dot_general(q4, k2, nt, preferred_element_type=F32) * scale
        q_head = lax.broadcasted_iota(jnp.int32, s.shape, 0) // tl
        m_head = lax.broadcasted_iota(jnp.int32, s.shape, 1) % CA_HEADS
        p = softmax(jnp.where(q_head == m_head, s, -jnp.inf))
        o4 = jnp.dot(p.astype(BF16), v2, preferred_element_type=F32).astype(o_ref.dtype)
        for h in range(CA_HEADS):
            o_ref[:, h * CA_HEAD_DIM:(h + 1) * CA_HEAD_DIM] = o4[h * tl:(h + 1) * tl]
    else:
        for h, (qh, kh, vh) in enumerate(zip(heads(q_ref), heads(k_ref), heads(v_ref))):
            s = lax.dot_general(qh, kh.astype(BF16), nt, preferred_element_type=F32) * scale
            o = jnp.dot(softmax(s).astype(BF16), vh.astype(BF16), preferred_element_type=F32)
            o_ref[:, h * CA_HEAD_DIM:(h + 1) * CA_HEAD_DIM] = o.astype(o_ref.dtype)


def mem_attention(q, row0, nb, L, mk, mv, layer=None, prev=None):
    ntok = q.shape[0]
    tl = min(L, 512)
    nl = L // tl
    base = row0 // tl
    rows = lambda b, i: (base + b * nl + i, 0)
    if layer is None:
        mem_spec = pl.BlockSpec((None, N_MEM, D_MODEL), lambda b, i: (b, 0, 0))
    else:
        mem_spec = pl.BlockSpec((None, None, N_MEM, CA_HEADS, CA_HEAD_DIM), lambda b, i: (layer, b, 0, 0, 0))
    in_specs = [pl.BlockSpec((tl, D_MODEL), rows), mem_spec, mem_spec]
    args = [q, mk, mv]
    body = functools.partial(_attn_body, layer is not None)
    aliases = {}
    if prev is not None:
        in_specs.append(pl.BlockSpec(memory_space=pl.ANY))
        args.append(prev)
        aliases = {len(args) - 1: 0}
        body = functools.partial(_drop_alias_ref, body, len(args) - 1)
    return pl.pallas_call(
        body,
        out_shape=jax.ShapeDtypeStruct((ntok, D_MODEL), BF16),
        grid=(nb, nl),
        in_specs=in_specs,
        out_specs=pl.BlockSpec((tl, D_MODEL), rows),
        input_output_aliases=aliases,
        compiler_params=_cparams(("arbitrary", "arbitrary")),
        name="mem_attention",
    )(*args)


def _router_body(x_ref, g_ref, w_ref, b_ref, xn_ref, ids_ref, gates_ref):
    x = x_ref[...]
    ms = jnp.mean(x * x, axis=-1, keepdims=True)
    xn = (x * lax.rsqrt(ms + EPS)) * g_ref[...]
    xn_ref[...] = xn.astype(xn_ref.dtype)
    logits = jnp.dot(xn.astype(BF16), w_ref[...].astype(BF16), preferred_element_type=F32) + b_ref[...]
    tm = logits.shape[0]
    col = lax.broadcasted_iota(jnp.int32, (tm, ROUTER_LANES), 1).astype(F32)
    neg = jnp.float32(-jnp.inf)
    first = lambda mask: jnp.min(jnp.where(mask, col, float(ROUTER_LANES)), axis=-1, keepdims=True)
    gl = jnp.where(col < MOE_GROUPS, logits, neg)
    gmax = jnp.max(gl, axis=-1, keepdims=True)
    gidx = first(gl == gmax)
    g_w = 1.0 / jnp.sum(jnp.exp(gl - gmax), axis=-1, keepdims=True)
    lo = MOE_GROUPS + gidx * MOE_EPG
    el = jnp.where((col >= lo) & (col < lo + MOE_EPG), logits, neg)
    m1 = jnp.max(el, axis=-1, keepdims=True)
    i1 = first(el == m1)
    z = jnp.sum(jnp.exp(el - m1), axis=-1, keepdims=True)
    p1 = 1.0 / z
    el2 = jnp.where(col == i1, neg, el)
    m2 = jnp.max(el2, axis=-1, keepdims=True)
    i2 = first(el2 == m2)
    p2 = jnp.exp(m2 - m1) / z
    tot = p1 + p2
    ids = jnp.where(col == 0, i1 - MOE_GROUPS, jnp.where(col == 1, i2 - MOE_GROUPS, 0.0))
    ids_ref[...] = ids.astype(jnp.int32)
    gates_ref[...] = jnp.where(col == 0, g_w * p1 / tot, jnp.where(col == 1, g_w * p2 / tot, 0.0))


def moe_router(x, g, w_group, b_group, w_expert, b_expert, tm=256):
    m, d = x.shape
    npad = ROUTER_LANES - MOE_GROUPS - MOE_EXPERTS
    w = jnp.pad(jnp.concatenate([w_group, w_expert], axis=1).astype(F32), ((0, 0), (0, npad)))
    b = jnp.pad(jnp.concatenate([b_group, b_expert]).astype(F32), (0, npad)).reshape(1, ROUTER_LANES)
    row = lambda n: pl.BlockSpec((tm, n), lambda i: (i, 0))
    xn, ids, gates = pl.pallas_call(
        _router_body,
        out_shape=[jax.ShapeDtypeStruct((m, d), F32),
                   jax.ShapeDtypeStruct((m, ROUTER_LANES), jnp.int32),
                   jax.ShapeDtypeStruct((m, ROUTER_LANES), F32)],
        grid=(m // tm,),
        in_specs=[row(d), pl.BlockSpec((1, d), lambda i: (0, 0)),
                  pl.BlockSpec((d, ROUTER_LANES), lambda i: (0, 0)),
                  pl.BlockSpec((1, ROUTER_LANES), lambda i: (0, 0))],
        out_specs=[row(d), row(ROUTER_LANES), row(ROUTER_LANES)],
        compiler_params=_cparams(("arbitrary",)),
        name="moe_router",
    )(x, g.reshape(1, d).astype(F32), w, b)
    return xn, ids[:, :MOE_TOP_K], gates


def _gather_body(tm, src_ref, cnt_ref, x_hbm, o_ref, buf_ref, sem):
    i = pl.program_id(0)
    n_groups = cnt_ref[i]

    @pl.when(n_groups < tm // GATHER_GROUP)
    def _():
        buf_ref[...] = jnp.zeros_like(buf_ref)

    def copy(r):
        return pltpu.make_async_copy(x_hbm.at[pl.ds(src_ref[i * tm + r], 1), :],
                                     buf_ref.at[pl.ds(r, 1), :], sem)

    def start(g, _):
        for u in range(GATHER_GROUP):
            copy(g * GATHER_GROUP + u).start(priority=u % DMA_PRIORITIES)
        return 0

    def wait(g, _):
        for u in range(GATHER_GROUP):
            copy(g * GATHER_GROUP + u).wait()
        return 0

    lax.fori_loop(0, n_groups, start, 0)
    lax.fori_loop(0, n_groups, wait, 0)
    o_ref[...] = buf_ref[...].astype(o_ref.dtype)


def moe_gather(x, row_src, tile_groups, n_tiles, tm):
    d = x.shape[1]
    return pl.pallas_call(
        functools.partial(_gather_body, tm),
        out_shape=jax.ShapeDtypeStruct((n_tiles * tm, d), BF16),
        grid_spec=pltpu.PrefetchScalarGridSpec(
            num_scalar_prefetch=2,
            grid=(n_tiles,),
            in_specs=[pl.BlockSpec(memory_space=pl.ANY)],
            out_specs=pl.BlockSpec((tm, d), lambda i, src, cnt: (i, 0)),
            scratch_shapes=[pltpu.VMEM((tm, d), x.dtype), pltpu.SemaphoreType.DMA(())],
        ),
        compiler_params=_cparams(("arbitrary",)),
        name="moe_gather",
    )(row_src, tile_groups, x)


def _segment_weights(te_ref, seg_ref, meta_ref, copies, cast):
    c = pl.program_id(0)
    r = pl.program_id(1)
    nt_used = meta_ref[0]
    nseg = meta_ref[1]
    used = r < nt_used
    seg = seg_ref[MOE_EXPERTS + r]
    first = jnp.logical_and(used, jnp.logical_or(r == 0, te_ref[r] != te_ref[jnp.maximum(r - 1, 0)]))
    g = c * nseg + seg
    slot = lax.rem(g, 2)

    @pl.when(first)
    def _():
        @pl.when(g == 0)
        def _():
            for cp in copies(te_ref[r], c, 0):
                cp.start()

        for cp in copies(te_ref[r], c, slot):
            cp.wait()
        wraps = seg + 1 >= nseg
        nxt_seg = jnp.where(wraps, 0, seg + 1)
        nxt_c = jnp.where(wraps, c + 1, c)

        @pl.when(nxt_c < pl.num_programs(0))
        def _():
            for cp in copies(seg_ref[nxt_seg], nxt_c, 1 - slot):
                cp.start()

        cast(slot)

    return used


def _moe_up_body(layer, tf, te_ref, seg_ref, meta_ref, x_ref, w_hbm, h_ref, wbuf, wgb_ref, wub_ref, sem):
    f = MOE_HIDDEN

    def copies(e, c, slot):
        col = pl.multiple_of(c * tf, tf)
        return [pltpu.make_async_copy(w_hbm.at[layer, e, :, pl.ds(half * f + col, tf)], wbuf.at[slot, half],
                                      sem.at[slot, half]) for half in range(2)]

    def cast(slot):
        wgb_ref[...] = wbuf[slot, 0].astype(BF16)
        wub_ref[...] = wbuf[slot, 1].astype(BF16)

    used = _segment_weights(te_ref, seg_ref, meta_ref, copies, cast)

    @pl.when(used)
    def _():
        x = x_ref[...]
        gate = jnp.dot(x, wgb_ref[...], preferred_element_type=F32)
        up = jnp.dot(x, wub_ref[...], preferred_element_type=F32)
        h_ref[...] = (jax.nn.silu(gate) * up).astype(h_ref.dtype)

    @pl.when(jnp.logical_not(used))
    def _():
        h_ref[...] = jnp.zeros_like(h_ref)


def _moe_down_body(layer, tn, te_ref, seg_ref, meta_ref, h_ref, w_hbm, y_ref, wbuf, wdb_ref, sem):
    def copies(e, c, slot):
        col = pl.multiple_of(c * tn, tn)
        return [pltpu.make_async_copy(w_hbm.at[layer, e, :, pl.ds(col, tn)], wbuf.at[slot], sem.at[slot])]

    def cast(slot):
        wdb_ref[...] = wbuf[slot].astype(BF16)

    used = _segment_weights(te_ref, seg_ref, meta_ref, copies, cast)

    @pl.when(used)
    def _():
        y_ref[...] = jnp.dot(h_ref[...], wdb_ref[...], preferred_element_type=F32)

    @pl.when(jnp.logical_not(used))
    def _():
        y_ref[...] = jnp.zeros_like(y_ref)


def moe_experts(xs, tile_e, seg_info, meta, w_gate_up, w_down, layer, tm, tf=512, tnd=D_MODEL):
    p, d = xs.shape
    n_tiles = p // tm
    f = MOE_HIDDEN
    h = pl.pallas_call(
        functools.partial(_moe_up_body, layer, tf),
        out_shape=jax.ShapeDtypeStruct((p, f), BF16),
        grid_spec=pltpu.PrefetchScalarGridSpec(
            num_scalar_prefetch=3,
            grid=(f // tf, n_tiles),
            in_specs=[pl.BlockSpec((tm, d), lambda c, r, te, sg, mt: (r, 0)),
                      pl.BlockSpec(memory_space=pl.ANY)],
            out_specs=pl.BlockSpec((tm, tf), lambda c, r, te, sg, mt: (r, c)),
            scratch_shapes=[pltpu.VMEM((2, 2, d, tf), F32), pltpu.VMEM((d, tf), BF16), pltpu.VMEM((d, tf), BF16),
                            pltpu.SemaphoreType.DMA((2, 2))],
        ),
        compiler_params=_cparams(("arbitrary", "arbitrary")),
        name="moe_up",
    )(tile_e, seg_info, meta, xs, w_gate_up)
    return pl.pallas_call(
        functools.partial(_moe_down_body, layer, tnd),
        out_shape=jax.ShapeDtypeStruct((p, d), F32),
        grid_spec=pltpu.PrefetchScalarGridSpec(
            num_scalar_prefetch=3,
            grid=(d // tnd, n_tiles),
            in_specs=[pl.BlockSpec((tm, f), lambda c, r, te, sg, mt: (r, 0)),
                      pl.BlockSpec(memory_space=pl.ANY)],
            out_specs=pl.BlockSpec((tm, tnd), lambda c, r, te, sg, mt: (r, c)),
            scratch_shapes=[pltpu.VMEM((2, f, tnd), F32), pltpu.VMEM((f, tnd), BF16),
                            pltpu.SemaphoreType.DMA((2,))],
        ),
        compiler_params=_cparams(("arbitrary", "arbitrary")),
        name="moe_down",
    )(tile_e, seg_info, meta, h, w_down)


def _combine_body(tt, emit_norm, pos_ref, x_ref, g_ref, y_hbm, *refs):
    if emit_norm:
        gn_ref, o_ref, xn_ref, buf_ref, sem = refs
    else:
        o_ref, buf_ref, sem = refs
    i = pl.program_id(0)

    def copy(t, k):
        return pltpu.make_async_copy(y_hbm.at[pl.ds(pos_ref[(i * tt + t) * MOE_TOP_K + k], 1), :],
                                     buf_ref.at[k, pl.ds(t, 1), :], sem)

    def start(t, _):
        for k in range(MOE_TOP_K):
            copy(t, k).start(priority=k % DMA_PRIORITIES)
        return 0

    def wait(t, _):
        for k in range(MOE_TOP_K):
            copy(t, k).wait()
        return 0

    lax.fori_loop(0, tt, start, 0)
    lax.fori_loop(0, tt, wait, 0)
    g = g_ref[...]
    y = g[:, 0:1] * buf_ref[0] + g[:, 1:2] * buf_ref[1]
    x = x_ref[...] + y
    o_ref[...] = x
    if emit_norm:
        ms = jnp.mean(x * x, axis=-1, keepdims=True)
        xn_ref[...] = ((x * lax.rsqrt(ms + EPS)) * gn_ref[...]).astype(xn_ref.dtype)


def moe_combine(x, gates, pos, y_sorted, norm_g=None, tt=256):
    m, d = x.shape
    row = pl.BlockSpec((tt, d), lambda i, pos: (i, 0))
    in_specs = [row, pl.BlockSpec((tt, ROUTER_LANES), lambda i, pos: (i, 0)), pl.BlockSpec(memory_space=pl.ANY)]
    args = [pos, x, gates, y_sorted]
    out_shape = [jax.ShapeDtypeStruct((m, d), F32)]
    if norm_g is not None:
        in_specs.append(pl.BlockSpec((1, d), lambda i, pos: (0, 0)))
        args.append(norm_g.reshape(1, d).astype(F32))
        out_shape.append(jax.ShapeDtypeStruct((m, d), BF16))
    res = pl.pallas_call(
        functools.partial(_combine_body, tt, norm_g is not None),
        out_shape=out_shape,
        grid_spec=pltpu.PrefetchScalarGridSpec(
            num_scalar_prefetch=1,
            grid=(m // tt,),
            in_specs=in_specs,
            out_specs=[row] * len(out_shape),
            scratch_shapes=[pltpu.VMEM((MOE_TOP_K, tt, d), F32), pltpu.SemaphoreType.DMA(())],
        ),
        compiler_params=_cparams(("arbitrary",)),
        name="moe_combine",
    )(*args)
    return res if norm_g is not None else res[0]


def _moe_plan(ids, tm):
    t = ids.shape[0]
    n = t * MOE_TOP_K
    n_tiles = -(-(n + MOE_EXPERTS * (tm - 1)) // tm)
    i32 = jnp.int32
    flat_e = ids.reshape(n)
    order = jnp.argsort(flat_e, stable=True).astype(i32)
    inv = jnp.argsort(order).astype(i32)
    se = flat_e[order]
    experts = jnp.arange(MOE_EXPERTS, dtype=i32)
    starts = jnp.searchsorted(se, experts, side="left", method="compare_all").astype(i32)
    counts = jnp.searchsorted(se, experts, side="right", method="compare_all").astype(i32) - starts
    tile_count = (counts + tm - 1) // tm
    tile_end = jnp.cumsum(tile_count).astype(i32)
    tile_start = tile_end - tile_count
    pos = tile_start[flat_e] * tm + (inv - starts[flat_e])
    tiles = jnp.arange(n_tiles, dtype=i32)
    tile_e = jnp.minimum(jnp.searchsorted(tile_end, tiles, side="right", method="compare_all"),
                         MOE_EXPERTS - 1).astype(i32)
    tile_valid = jnp.clip(counts[tile_e] - (tiles - tile_start[tile_e]) * tm, 0, tm)
    tile_groups = (tile_valid + GATHER_GROUP - 1) // GATHER_GROUP
    within = jnp.arange(tm, dtype=i32)[None, :]
    slot = (starts[tile_e] + (tiles - tile_start[tile_e]) * tm)[:, None] + within
    src = order[jnp.clip(slot, 0, n - 1)] // MOE_TOP_K
    row_src = jnp.where(within < tile_valid[:, None], src, 0).reshape(n_tiles * tm)
    nonempty = tile_count > 0
    seg_of_expert = jnp.cumsum(nonempty.astype(i32)).astype(i32) - 1
    seg_expert = jnp.argsort(jnp.where(nonempty, 0, 1).astype(i32), stable=True).astype(i32)
    seg_info = jnp.concatenate([seg_expert, seg_of_expert[tile_e]])
    meta = jnp.stack([tile_end[-1], jnp.sum(nonempty.astype(i32))]).astype(i32)
    return row_src, pos.astype(i32), tile_e, seg_info, meta, tile_groups.astype(i32), n_tiles


def hier_moe(x, g, layer, w_group, b_group, w_expert, b_expert, w_gate_up, w_down, next_norm_g=None):
    xn, ids, gates = moe_router(x, g, w_group, b_group, w_expert, b_expert)
    row_src, pos, tile_e, seg_info, meta, tile_groups, n_tiles = _moe_plan(ids, MOE_TILE)
    xs = moe_gather(xn, row_src, tile_groups, n_tiles, MOE_TILE)
    y = moe_experts(xs, tile_e, seg_info, meta, w_gate_up, w_down, layer, MOE_TILE)
    return moe_combine(x, gates, pos, y, norm_g=next_norm_g)


def kernel(x_prompt, x_sample, mem_prompt, state_ssm_re, state_ssm_im, state_gla, cache_mem_k, cache_mem_v, norm_mixer, norm_ca, norm_moe, norm_final, ssm_lambda_re, ssm_lambda_im, ssm_log_dt, ssm_b_re, ssm_b_im, ssm_c_re, ssm_c_im, ssm_d, ssm_w_glu, gla_w_in, gla_w_gate_up, gla_b_gate, gla_norm, gla_w_out, ca_mem_norm, ca_w_q, ca_w_kv, ca_w_o, moe_w_group, moe_b_group, moe_w_expert, moe_b_expert, moe_w_gate_up, moe_w_down):
    bp, lp, d = x_prompt.shape
    bs, ls, _ = x_sample.shape
    np_ = bp * lp
    ns_ = bs * ls

    mem = mem_prompt.reshape(bp * N_MEM, d)
    pk, pv = [], []
    for i in range(DEPTH):
        mn = rmsnorm(mem, ca_mem_norm[i])
        kv = matmul(mn, ca_w_kv, i, out_dtype=F32)
        pk.append(kv[:, :d].reshape(bp, N_MEM, d))
        pv.append(kv[:, d:].reshape(bp, N_MEM, d))
    prompt_mem_k = jnp.stack(pk).reshape(DEPTH, bp, N_MEM, CA_HEADS, CA_HEAD_DIM)
    prompt_mem_v = jnp.stack(pv).reshape(DEPTH, bp, N_MEM, CA_HEADS, CA_HEAD_DIM)

    x = jnp.concatenate([x_prompt.reshape(np_, d), x_sample.reshape(ns_, d)], axis=0)
    zero_ssm = jnp.zeros((bp, SSM_GROUPS, SSM_STATE), F32)
    zero_gla = jnp.zeros((bp, GLA_HEADS, GLA_DK_HEAD, GLA_DV_HEAD), F32)
    zero_act = jnp.zeros((np_ + ns_, d), BF16)
    p_re, p_im, s_re, s_im, p_gla, s_gla = [], [], [], [], [], []
    xn_mixer = None

    for i in range(DEPTH):
        j = i // 2
        if i % 2 == 0:
            xn_p = rmsnorm(x, norm_mixer[i], out_dtype=F32, row0=0, nrows=np_)
            xn_s = rmsnorm(x, norm_mixer[i], out_dtype=F32, row0=np_, nrows=ns_)
            prm = (ssm_lambda_re[j], ssm_lambda_im[j], ssm_log_dt[j], ssm_b_re[j], ssm_b_im[j],
                   ssm_c_re[j], ssm_c_im[j], ssm_d[j])
            zp, hr, hi = s5_layer(xn_p.reshape(bp, lp, d), zero_ssm, zero_ssm, prm)
            p_re.append(hr)
            p_im.append(hi)
            zs, hr, hi = s5_layer(xn_s.reshape(bs, ls, d), state_ssm_re[j], state_ssm_im[j], prm)
            s_re.append(hr)
            s_im.append(hi)
            z = jnp.concatenate([zp, zs], axis=0)
            x = matmul(z, ssm_w_glu, j, mode="glu_res", res=x, n_out=d, tn=256)
        else:
            xn = xn_mixer if xn_mixer is not None else rmsnorm(x, norm_mixer[i])
            w_in_t = jnp.swapaxes(gla_w_in, 1, 2)
            proj = matmul(xn, w_in_t, j, out_dtype=F32, n_out=GLA_MAIN, w_is_nk=True)
            la = gla_gate(xn, w_in_t[j, GLA_MAIN:, :], gla_w_gate_up[j], gla_b_gate[j])
            o, st = gla_recurrence(proj, la, 0, bp, lp, zero_gla, gla_norm[j], prev=zero_act)
            p_gla.append(st)
            o, st = gla_recurrence(proj, la, np_, bs, ls, state_gla[j], gla_norm[j], prev=o)
            s_gla.append(st)
            x = matmul(o, gla_w_out, j, mode="res", res=x)

        xn = rmsnorm(x, norm_ca[i])
        q = matmul(xn, ca_w_q, i, out_dtype=BF16)
        att = mem_attention(q, 0, bp, lp, pk[i], pv[i], prev=zero_act)
        att = mem_attention(q, np_, bs, ls, cache_mem_k, cache_mem_v, layer=i, prev=att)
        x = matmul(att, ca_w_o, i, mode="res", res=x)

        gla_next = i + 1 < DEPTH and (i + 1) % 2 == 1
        res = hier_moe(x, norm_moe[i], i, moe_w_group[i], moe_b_group[i], moe_w_expert[i], moe_b_expert[i],
                       moe_w_gate_up, moe_w_down, next_norm_g=norm_mixer[i + 1] if gla_next else None)
        x, xn_mixer = res if gla_next else (res, None)

    y_prompt = rmsnorm(x, norm_final, out_dtype=F32, row0=0, nrows=np_)
    y_sample = rmsnorm(x, norm_final, out_dtype=F32, row0=np_, nrows=ns_)
    return (y_prompt.reshape(bp, lp, d), y_sample.reshape(bs, ls, d),
            jnp.stack(p_re), jnp.stack(p_im), jnp.stack(p_gla), prompt_mem_k, prompt_mem_v,
            jnp.stack(s_re), jnp.stack(s_im), jnp.stack(s_gla))
```

```python
import functools
import math

import jax
import jax.numpy as jnp
import numpy as np
from jax import lax
from jax.experimental import pallas as pl
from jax.experimental.pallas import tpu as pltpu

F32 = jnp.float32
BF16 = jnp.bfloat16

EPS = 1e-6
D_MODEL = 4096
DEPTH = 2

SSM_GROUP_CH = 16
SSM_GROUPS = D_MODEL // SSM_GROUP_CH
SSM_STATE = 64
SSM_SEQS = 32
SSM_SEQ_TILE = 32
SSM_SCAN_UNROLL = True
SSM_BLK_CH = 128
SSM_BLK_GROUPS = SSM_BLK_CH // SSM_GROUP_CH
SSM_NBLK = D_MODEL // SSM_BLK_CH
SSM_BLK_STATE = SSM_BLK_GROUPS * SSM_STATE

GLA_HEADS = 4
GLA_DK = D_MODEL // 2
GLA_DV = D_MODEL
GLA_DK_HEAD = GLA_DK // GLA_HEADS
GLA_DV_HEAD = GLA_DV // GLA_HEADS
GLA_GATE_RANK = 16
GLA_TAU = 16.0
GLA_MAIN = 2 * GLA_DK + 2 * GLA_DV
GLA_CHUNK = 64
GLA_HEADS_PER_STEP = 4

N_MEM = 256
CA_HEADS = 4
CA_HEAD_DIM = D_MODEL // CA_HEADS

MOE_GROUPS = 4
MOE_EPG = 8
MOE_EXPERTS = MOE_GROUPS * MOE_EPG
MOE_TOP_K = 2
MOE_HIDDEN = D_MODEL // 4
MOE_TILE = 256
GATHER_GROUP = 8
DMA_PRIORITIES = 2
ROUTER_LANES = 128

DENSE_TM = 1088
SUBLANES = 8
VMEM_LIMIT = 56 * 1024 * 1024


def _cparams(sem):
    return pltpu.CompilerParams(dimension_semantics=sem, vmem_limit_bytes=VMEM_LIMIT)


def _rms_body(n_add, emit_sum, x_ref, *refs):
    adds = refs[:n_add]
    g_ref = refs[n_add]
    outs = refs[n_add + 1:]
    x = x_ref[...].astype(F32)
    for a in adds:
        x = x + a[...].astype(F32)
    ms = jnp.mean(x * x, axis=-1, keepdims=True)
    xn = (x * lax.rsqrt(ms + EPS)) * g_ref[...]
    if emit_sum:
        outs[0][...] = x
        outs[1][...] = xn.astype(outs[1].dtype)
    else:
        outs[0][...] = xn.astype(outs[0].dtype)


def rmsnorm(x, g, adds=(), out_dtype=BF16, emit_sum=False, tm=256, row0=0, nrows=None):
    d = x.shape[1]
    m = x.shape[0] if nrows is None else nrows
    assert m % tm == 0 and row0 % tm == 0
    b0 = row0 // tm
    src = pl.BlockSpec((tm, d), lambda i: (b0 + i, 0))
    row = pl.BlockSpec((tm, d), lambda i: (i, 0))
    out_shape = [jax.ShapeDtypeStruct((m, d), out_dtype)]
    out_specs = [row]
    if emit_sum:
        out_shape.insert(0, jax.ShapeDtypeStruct((m, d), F32))
        out_specs.insert(0, row)
    res = pl.pallas_call(
        functools.partial(_rms_body, len(adds), emit_sum),
        out_shape=out_shape,
        grid=(m // tm,),
        in_specs=[src] * (1 + len(adds)) + [pl.BlockSpec((1, d), lambda i: (0, 0))],
        out_specs=out_specs,
        compiler_params=_cparams(("arbitrary",)),
        name="rmsnorm",
    )(x, *adds, g.reshape(1, d).astype(F32))
    return res if emit_sum else res[0]


def _mm_body(mode, w_is_nk, x_ref, *refs):
    if mode == "glu_res":
        w_ref, w2_ref, res_ref, o_ref, wb_ref, wb2_ref = refs
    elif mode == "res":
        w_ref, res_ref, o_ref, wb_ref = refs
    else:
        w_ref, o_ref, wb_ref = refs

    @pl.when(pl.program_id(1) == 0)
    def _():
        wb_ref[...] = w_ref[...].astype(BF16)
        if mode == "glu_res":
            wb2_ref[...] = w2_ref[...].astype(BF16)

    x = x_ref[...]
    if w_is_nk:
        acc = lax.dot_general(x, wb_ref[...], (((1,), (1,)), ((), ())), preferred_element_type=F32)
    else:
        acc = jnp.dot(x, wb_ref[...], preferred_element_type=F32)
    if mode == "glu_res":
        gate = jnp.dot(x, wb2_ref[...], preferred_element_type=F32)
        o_ref[...] = res_ref[...] + acc * jax.nn.sigmoid(gate)
    elif mode == "res":
        o_ref[...] = res_ref[...] + acc
    else:
        o_ref[...] = acc.astype(o_ref.dtype)


def matmul(x, w, layer, mode="plain", res=None, out_dtype=F32, n_out=None, tm=DENSE_TM, tn=512, w_is_nk=False):
    m, k = x.shape
    n = n_out if n_out is not None else w.shape[1 if w_is_nk else 2]
    if m % tm:
        tm = m
    assert m % tm == 0 and n % tn == 0 and not (w_is_nk and mode != "plain")
    nb = n // tn
    if w_is_nk:
        w_spec, w_tile = pl.BlockSpec((None, tn, k), lambda j, i: (layer, j, 0)), (tn, k)
    else:
        w_spec, w_tile = pl.BlockSpec((None, k, tn), lambda j, i: (layer, 0, j)), (k, tn)
    in_specs = [pl.BlockSpec((tm, k), lambda j, i: (i, 0)), w_spec]
    args = [x, w]
    scratch = [pltpu.VMEM(w_tile, BF16)]
    if mode == "glu_res":
        in_specs.append(pl.BlockSpec((None, k, tn), lambda j, i: (layer, 0, nb + j)))
        args.append(w)
        scratch.append(pltpu.VMEM((k, tn), BF16))
    if mode in ("res", "glu_res"):
        in_specs.append(pl.BlockSpec((tm, tn), lambda j, i: (i, j)))
        args.append(res)
        out_dtype = F32
    return pl.pallas_call(
        functools.partial(_mm_body, mode, w_is_nk),
        out_shape=jax.ShapeDtypeStruct((m, n), out_dtype),
        grid=(nb, m // tm),
        in_specs=in_specs,
        out_specs=pl.BlockSpec((tm, tn), lambda j, i: (i, j)),
        scratch_shapes=scratch,
        compiler_params=_cparams(("arbitrary", "arbitrary")),
        name="matmul_" + mode,
    )(*args)


def _s5_body(lt, emit_y, u_ref, h0_ref, ar_ref, ai_ref, bd_ref, *refs):
    if emit_y:
        cbd_ref, d_ref, z_ref, ht_ref, h_ref, bu_ref, hs_ref = refs
    else:
        ht_ref, h_ref, bu_ref = refs
    tb = pl.program_id(2)
    ns = SSM_BLK_STATE

    @pl.when(tb == 0)
    def _():
        h_ref[...] = h0_ref[...]

    st = SSM_SEQ_TILE
    groups = range(st // SUBLANES)
    u = u_ref[...].reshape(lt * st, SSM_BLK_CH)
    bu_ref[...] = jnp.dot(u.astype(BF16), bd_ref[...], preferred_element_type=F32)
    ar = jnp.broadcast_to(ar_ref[...], (SUBLANES, ns))
    ai = jnp.broadcast_to(ai_ref[...], (SUBLANES, ns))

    def step(t, carry):
        r0 = t * st
        out = []
        for g in groups:
            hr, hi = carry[g]
            rows = pl.ds(pl.multiple_of(r0 + g * SUBLANES, SUBLANES), SUBLANES)
            nr = ar * hr - ai * hi + bu_ref[rows, 0:ns]
            ni = ar * hi + ai * hr + bu_ref[rows, ns:2 * ns]
            if emit_y:
                hs_ref[rows, 0:ns] = nr
                hs_ref[rows, ns:2 * ns] = ni
            out.append((nr, ni))
        return tuple(out)

    grp = lambda g: slice(g * SUBLANES, (g + 1) * SUBLANES)
    init = tuple((h_ref[grp(g), 0:ns], h_ref[grp(g), ns:2 * ns]) for g in groups)
    final = lax.fori_loop(0, lt, step, init, unroll=SSM_SCAN_UNROLL)
    for g in groups:
        hr, hi = final[g]
        h_ref[grp(g), 0:ns] = hr
        h_ref[grp(g), ns:2 * ns] = hi
        ht_ref[grp(g), 0:ns] = hr
        ht_ref[grp(g), ns:2 * ns] = hi
    if emit_y:
        y = jnp.dot(hs_ref[...].astype(BF16), cbd_ref[...], preferred_element_type=F32)
        y = y + d_ref[...] * u
        z_ref[...] = jax.nn.gelu(y, approximate=True).reshape(lt, st, SSM_BLK_CH)


def _s5_scan(ut, h0, ar, ai, bd, cbd, dsk, emit_y):
    ltot = ut.shape[0]
    lt = min(ltot, 64)
    assert ltot % lt == 0
    ns2 = 2 * SSM_BLK_STATE
    st = SSM_SEQ_TILE
    nsg = SSM_SEQS // st
    in_specs = [
        pl.BlockSpec((lt, st, SSM_BLK_CH), lambda c, s, t: (t, s, c)),
        pl.BlockSpec((None, st, ns2), lambda c, s, t: (c, s, 0)),
        pl.BlockSpec((None, 1, SSM_BLK_STATE), lambda c, s, t: (c, 0, 0)),
        pl.BlockSpec((None, 1, SSM_BLK_STATE), lambda c, s, t: (c, 0, 0)),
        pl.BlockSpec((None, SSM_BLK_CH, ns2), lambda c, s, t: (c, 0, 0)),
    ]
    args = [ut, h0, ar, ai, bd]
    ht_shape = jax.ShapeDtypeStruct((SSM_NBLK, SSM_SEQS, ns2), F32)
    ht_spec = pl.BlockSpec((None, st, ns2), lambda c, s, t: (c, s, 0))
    scratch = [pltpu.VMEM((st, ns2), F32), pltpu.VMEM((lt * st, ns2), F32)]
    if emit_y:
        in_specs += [pl.BlockSpec((None, ns2, SSM_BLK_CH), lambda c, s, t: (c, 0, 0)),
                     pl.BlockSpec((None, 1, SSM_BLK_CH), lambda c, s, t: (c, 0, 0))]
        args += [cbd, dsk]
        out_shape = [jax.ShapeDtypeStruct(ut.shape, F32), ht_shape]
        out_specs = [pl.BlockSpec((lt, st, SSM_BLK_CH), lambda c, s, t: (t, s, c)), ht_spec]
        scratch.append(pltpu.VMEM((lt * st, ns2), F32))
    else:
        out_shape = [ht_shape]
        out_specs = [ht_spec]
    res = pl.pallas_call(
        functools.partial(_s5_body, lt, emit_y),
        out_shape=out_shape,
        grid=(SSM_NBLK, nsg, ltot // lt),
        in_specs=in_specs,
        out_specs=out_specs,
        scratch_shapes=scratch,
        compiler_params=_cparams(("arbitrary", "arbitrary", "arbitrary")),
        name="s5_scan_y" if emit_y else "s5_scan_state",
    )(*args)
    return (res[0], res[1]) if emit_y else (None, res[0])


def _s5_carry_body(nb, nseg, e_ref, h0_ref, ar_ref, ai_ref, hs_ref, he_ref):
    ns = SSM_BLK_STATE
    ar = ar_ref[...]
    ai = ai_ref[...]
    for b in range(nb):
        hr = h0_ref[b:b + 1, 0:ns]
        hi = h0_ref[b:b + 1, ns:2 * ns]
        for j in range(nseg):
            s = b * nseg + j
            hs_ref[s:s + 1, 0:ns] = hr
            hs_ref[s:s + 1, ns:2 * ns] = hi
            er = e_ref[s:s + 1, 0:ns]
            ei = e_ref[s:s + 1, ns:2 * ns]
            hr, hi = ar * hr - ai * hi + er, ar * hi + ai * hr + ei
        he_ref[b:b + 1, 0:ns] = hr
        he_ref[b:b + 1, ns:2 * ns] = hi


def _s5_carry(e0, h0, ar_seg, ai_seg, nb, nseg):
    ns2 = 2 * SSM_BLK_STATE
    return pl.pallas_call(
        functools.partial(_s5_carry_body, nb, nseg),
        out_shape=[jax.ShapeDtypeStruct((SSM_NBLK, SSM_SEQS, ns2), F32),
                   jax.ShapeDtypeStruct((SSM_NBLK, nb, ns2), F32)],
        grid=(SSM_NBLK,),
        in_specs=[pl.BlockSpec((None, SSM_SEQS, ns2), lambda c: (c, 0, 0)),
                  pl.BlockSpec((None, nb, ns2), lambda c: (c, 0, 0)),
                  pl.BlockSpec((None, 1, SSM_BLK_STATE), lambda c: (c, 0, 0)),
                  pl.BlockSpec((None, 1, SSM_BLK_STATE), lambda c: (c, 0, 0))],
        out_specs=[pl.BlockSpec((None, SSM_SEQS, ns2), lambda c: (c, 0, 0)),
                   pl.BlockSpec((None, nb, ns2), lambda c: (c, 0, 0))],
        compiler_params=_cparams(("arbitrary",)),
        name="s5_carry",
    )(e0, h0, ar_seg, ai_seg)


def _s5_params(lam_re, lam_im, log_dt, b_re, b_im, c_re, c_im, d_skip, seg_len):
    dt = jnp.exp(log_dt.astype(F32))[:, None]
    lr = lam_re.astype(F32)
    li = lam_im.astype(F32)
    mag = jnp.exp(lr * dt)
    ab_re = mag * jnp.cos(li * dt)
    ab_im = mag * jnp.sin(li * dt)
    nr = ab_re - 1.0
    ni = ab_im
    den = lr * lr + li * li
    f_re = (nr * lr + ni * li) / den
    f_im = (ni * lr - nr * li) / den
    br = b_re.astype(F32)
    bi = b_im.astype(F32)
    bb_re = f_re[..., None] * br - f_im[..., None] * bi
    bb_im = f_re[..., None] * bi + f_im[..., None] * br
    mag_s = jnp.exp(lr * dt * seg_len)
    as_re = mag_s * jnp.cos(li * dt * seg_len)
    as_im = mag_s * jnp.sin(li * dt * seg_len)
    eye = jnp.eye(SSM_BLK_GROUPS, dtype=F32)

    def in_proj(w):
        w = w.reshape(SSM_NBLK, SSM_BLK_GROUPS, SSM_STATE, SSM_GROUP_CH).transpose(0, 1, 3, 2)
        return jnp.einsum("bjcp,jk->bjckp", w, eye).reshape(SSM_NBLK, SSM_BLK_CH, SSM_BLK_STATE)

    def out_proj(w):
        w = w.reshape(SSM_NBLK, SSM_BLK_GROUPS, SSM_GROUP_CH, SSM_STATE).transpose(0, 1, 3, 2)
        return jnp.einsum("bjpc,jk->bjpkc", w, eye).reshape(SSM_NBLK, SSM_BLK_STATE, SSM_BLK_CH)

    bd = jnp.concatenate([in_proj(bb_re), in_proj(bb_im)], axis=-1).astype(BF16)
    cbd = jnp.concatenate([out_proj(c_re.astype(F32)), -out_proj(c_im.astype(F32))], axis=1).astype(BF16)
    blk = lambda v: v.reshape(SSM_NBLK, 1, SSM_BLK_STATE)
    return (blk(ab_re), blk(ab_im), blk(as_re), blk(as_im), bd, cbd,
            d_skip.astype(F32).reshape(SSM_NBLK, 1, SSM_BLK_CH))


def _state_to_blocks(h_re, h_im):
    nb = h_re.shape[0]
    f = lambda h: h.astype(F32).reshape(nb, SSM_NBLK, SSM_BLK_STATE).transpose(1, 0, 2)
    return jnp.concatenate([f(h_re), f(h_im)], axis=-1)


def _blocks_to_state(hb):
    nb = hb.shape[1]
    f = lambda h: h.transpose(1, 0, 2).reshape(nb, SSM_GROUPS, SSM_STATE)
    return f(hb[..., :SSM_BLK_STATE]), f(hb[..., SSM_BLK_STATE:])


def s5_layer(u, h0_re, h0_im, prm):
    nb, L, d = u.shape
    nseg = SSM_SEQS // nb
    seg = L // nseg
    lam_re, lam_im, log_dt, b_re, b_im, c_re, c_im, d_skip = prm
    ar, ai, as_re, as_im, bd, cbd, dsk = _s5_params(lam_re, lam_im, log_dt, b_re, b_im, c_re, c_im, d_skip, seg)
    ut = u.reshape(nb * nseg, seg, d).transpose(1, 0, 2)
    h0b = _state_to_blocks(h0_re, h0_im)
    if nseg == 1:
        zt, hT = _s5_scan(ut, h0b, ar, ai, bd, cbd, dsk, True)
    else:
        zero = jnp.zeros((SSM_NBLK, SSM_SEQS, 2 * SSM_BLK_STATE), F32)
        _, e0 = _s5_scan(ut, zero, ar, ai, bd, None, None, False)
        hstart, hT = _s5_carry(e0, h0b, as_re, as_im, nb, nseg)
        zt, _ = _s5_scan(ut, hstart, ar, ai, bd, cbd, dsk, True)
    z = zt.astype(BF16).transpose(1, 0, 2).reshape(nb * L, d)
    hr, hi = _blocks_to_state(hT)
    return z, hr, hi


def _gla_gate_body(x_ref, w1_ref, w2_ref, b_ref, o_ref):
    glr = lax.dot_general(x_ref[...], w1_ref[...].astype(BF16), (((1,), (1,)), ((), ())),
                          preferred_element_type=F32)
    logits = jnp.dot(glr.astype(BF16), w2_ref[...].astype(BF16), preferred_element_type=F32) + b_ref[...]
    o_ref[...] = jax.nn.log_sigmoid(logits) / GLA_TAU


def gla_gate(xn, w_glr_t, w_gate_up, b_gate, tm=512):
    m, k = xn.shape
    rp = 128
    w1 = jnp.pad(w_glr_t.astype(F32), ((0, rp - GLA_GATE_RANK), (0, 0)))
    w2 = jnp.pad(w_gate_up.astype(F32), ((0, rp - GLA_GATE_RANK), (0, 0)))
    return pl.pallas_call(
        _gla_gate_body,
        out_shape=jax.ShapeDtypeStruct((m, GLA_DK), F32),
        grid=(m // tm,),
        in_specs=[pl.BlockSpec((tm, k), lambda i: (i, 0)),
                  pl.BlockSpec((rp, k), lambda i: (0, 0)),
                  pl.BlockSpec((rp, GLA_DK), lambda i: (0, 0)),
                  pl.BlockSpec((1, GLA_DK), lambda i: (0, 0))],
        out_specs=pl.BlockSpec((tm, GLA_DK), lambda i: (i, 0)),
        compiler_params=_cparams(("arbitrary",)),
        name="gla_gate",
    )(xn, w1, w2, b_gate.reshape(1, GLA_DK).astype(F32))


def _gla_body(lc, q_ref, k_ref, v_ref, r_ref, la_ref, s0_ref, g_ref, o_ref, st_ref, s_ref):
    c = pl.program_id(2)

    @pl.when(c == 0)
    def _():
        s_ref[...] = s0_ref[...].astype(F32)

    row = lax.broadcasted_iota(jnp.int32, (lc, lc), 0)
    col = lax.broadcasted_iota(jnp.int32, (lc, lc), 1)
    tri = (col <= row).astype(BF16)
    ones = jnp.ones((lc, 128), BF16)
    tn = (((0,), (0,)), ((), ()))
    for hh in range(GLA_HEADS_PER_STEP):
        ks = slice(hh * GLA_DK_HEAD, (hh + 1) * GLA_DK_HEAD)
        vs = slice(hh * GLA_DV_HEAD, (hh + 1) * GLA_DV_HEAD)
        la = la_ref[:, ks]
        la_hi = la.astype(BF16)
        la_lo = (la - la_hi.astype(F32)).astype(BF16)
        cum = (jnp.dot(tri, la_hi, preferred_element_type=F32)
               + jnp.dot(tri, la_lo, preferred_element_type=F32))
        end = cum[lc - 1:lc, :]
        kd = (k_ref[:, ks].astype(F32) * jnp.exp(end - cum)).astype(BF16)
        end_col = (lax.dot_general(la_hi, ones, tn, preferred_element_type=F32)
                   + lax.dot_general(la_lo, ones, tn, preferred_element_type=F32))
        decay = jnp.exp(end_col[:, 0:1])
        s_new = decay * s_ref[hh] + lax.dot_general(kd, v_ref[:, vs].astype(BF16), tn, preferred_element_type=F32)
        s_ref[hh] = s_new
        qs = (q_ref[:, ks].astype(F32) * (GLA_DK_HEAD ** -0.5)).astype(BF16)
        o = jnp.dot(qs, s_new.astype(BF16), preferred_element_type=F32)
        o = o * lax.rsqrt(jnp.mean(o * o, axis=-1, keepdims=True) + EPS)
        o = o * g_ref[:, vs]
        o_ref[:, vs] = (o * jax.nn.silu(r_ref[:, vs].astype(F32))).astype(o_ref.dtype)

    @pl.when(c == pl.num_programs(2) - 1)
    def _():
        st_ref[...] = s_ref[...]


def _drop_alias_ref(body, n_in, *refs):
    return body(*refs[:n_in], *refs[n_in + 1:])


def gla_recurrence(proj, la, row0, nb, L, s0, norm_g, prev=None):
    ntok = proj.shape[0]
    lc = min(GLA_CHUNK, L)
    nc = L // lc
    base = row0 // lc
    hp = GLA_HEADS_PER_STEP
    wk, wv = hp * GLA_DK_HEAD, hp * GLA_DV_HEAD
    kq = GLA_DK // wk
    kv = 2 * GLA_DK // wv
    kr = kv + GLA_DV // wv
    rows = lambda b, h, c: base + b * nc + c
    state_spec = pl.BlockSpec((None, hp, GLA_DK_HEAD, GLA_DV_HEAD), lambda b, h, c: (b, h, 0, 0))
    in_specs = [pl.BlockSpec((lc, wk), lambda b, h, c: (rows(b, h, c), h)),
                pl.BlockSpec((lc, wk), lambda b, h, c: (rows(b, h, c), kq + h)),
                pl.BlockSpec((lc, wv), lambda b, h, c: (rows(b, h, c), kv + h)),
                pl.BlockSpec((lc, wv), lambda b, h, c: (rows(b, h, c), kr + h)),
                pl.BlockSpec((lc, wk), lambda b, h, c: (rows(b, h, c), h)),
                state_spec,
                pl.BlockSpec((1, wv), lambda b, h, c: (0, h))]
    args = [proj, proj, proj, proj, la, s0, norm_g.reshape(1, GLA_DV).astype(F32)]
    body = functools.partial(_gla_body, lc)
    aliases = {}
    if prev is not None:
        in_specs.append(pl.BlockSpec(memory_space=pl.ANY))
        args.append(prev)
        aliases = {len(args) - 1: 0}
        body = functools.partial(_drop_alias_ref, body, len(args) - 1)
    o, st = pl.pallas_call(
        body,
        out_shape=[jax.ShapeDtypeStruct((ntok, GLA_DV), BF16),
                   jax.ShapeDtypeStruct((nb, GLA_HEADS, GLA_DK_HEAD, GLA_DV_HEAD), F32)],
        grid=(nb, GLA_HEADS // hp, nc),
        in_specs=in_specs,
        out_specs=[pl.BlockSpec((lc, wv), lambda b, h, c: (rows(b, h, c), h)), state_spec],
        scratch_shapes=[pltpu.VMEM((hp, GLA_DK_HEAD, GLA_DV_HEAD), F32)],
        input_output_aliases=aliases,
        compiler_params=_cparams(("arbitrary", "arbitrary", "arbitrary")),
        name="gla_recurrence",
    )(*args)
    return o, st


def _attn_body(per_head, q_ref, k_ref, v_ref, o_ref):
    nt = (((1,), (1,)), ((), ()))
    scale = CA_HEAD_DIM ** -0.5
    heads = lambda ref: [ref[:, h * CA_HEAD_DIM:(h + 1) * CA_HEAD_DIM] for h in range(CA_HEADS)]

    def softmax(s):
        p = jnp.exp(s - jnp.max(s, axis=-1, keepdims=True))
        return p / jnp.sum(p, axis=-1, keepdims=True)

    if per_head:
        tl = q_ref.shape[0]
        k2 = k_ref[...].reshape(N_MEM * CA_HEADS, CA_HEAD_DIM).astype(BF16)
        v2 = v_ref[...].reshape(N_MEM * CA_HEADS, CA_HEAD_DIM).astype(BF16)
        q4 = jnp.concatenate(heads(q_ref), axis=0)
        s = lax.dot_general(q4, k2, nt, preferred_element_type=F32) * scale
        q_head = lax.broadcasted_iota(jnp.int32, s.shape, 0) // tl
        m_head = lax.broadcasted_iota(jnp.int32, s.shape, 1) % CA_HEADS
        p = softmax(jnp.where(q_head == m_head, s, -jnp.inf))
        o4 = jnp.dot(p.astype(BF16), v2, preferred_element_type=F32).astype(o_ref.dtype)
        for h in range(CA_HEADS):
            o_ref[:, h * CA_HEAD_DIM:(h + 1) * CA_HEAD_DIM] = o4[h * tl:(h + 1) * tl]
    else:
        for h, (qh, kh, vh) in enumerate(zip(heads(q_ref), heads(k_ref), heads(v_ref))):
            s = lax.dot_general(qh, kh.astype(BF16), nt, preferred_element_type=F32) * scale
            o = jnp.dot(softmax(s).astype(BF16), vh.astype(BF16), preferred_element_type=F32)
            o_ref[:, h * CA_HEAD_DIM:(h + 1) * CA_HEAD_DIM] = o.astype(o_ref.dtype)


def mem_attention(q, row0, nb, L, mk, mv, layer=None, prev=None):
    ntok = q.shape[0]
    tl = min(L, 512)
    nl = L // tl
    base = row0 // tl
    rows = lambda b, i: (base + b * nl + i, 0)
    if layer is None:
        mem_spec = pl.BlockSpec((None, N_MEM, D_MODEL), lambda b, i: (b, 0, 0))
    else:
        mem_spec = pl.BlockSpec((None, None, N_MEM, CA_HEADS, CA_HEAD_DIM), lambda b, i: (layer, b, 0, 0, 0))
    in_specs = [pl.BlockSpec((tl, D_MODEL), rows), mem_spec, mem_spec]
    args = [q, mk, mv]
    body = functools.partial(_attn_body, layer is not None)
    aliases = {}
    if prev is not None:
        in_specs.append(pl.BlockSpec(memory_space=pl.ANY))
        args.append(prev)
        aliases = {len(args) - 1: 0}
        body = functools.partial(_drop_alias_ref, body, len(args) - 1)
    return pl.pallas_call(
        body,
        out_shape=jax.ShapeDtypeStruct((ntok, D_MODEL), BF16),
        grid=(nb, nl),
        in_specs=in_specs,
        out_specs=pl.BlockSpec((tl, D_MODEL), rows),
        input_output_aliases=aliases,
        compiler_params=_cparams(("arbitrary", "arbitrary")),
        name="mem_attention",
    )(*args)


def _pack_halves(x):
    n = x.shape[1] // 2
    lo = lax.bitcast_convert_type(x[:, :n].astype(BF16).astype(F32), jnp.uint32)
    hi = lax.bitcast_convert_type(x[:, n:].astype(BF16).astype(F32), jnp.uint32)
    return lax.shift_right_logical(lo, jnp.uint32(16)) | hi


def _unpack_halves(w):
    lo = lax.bitcast_convert_type(lax.shift_left(w, jnp.uint32(16)), F32)
    hi = lax.bitcast_convert_type(w & jnp.uint32(0xFFFF0000), F32)
    return lo, hi


def _router_body(x_ref, g_ref, w_ref, b_ref, xn_ref, ids_ref, gates_ref):
    x = x_ref[...]
    ms = jnp.mean(x * x, axis=-1, keepdims=True)
    xn = (x * lax.rsqrt(ms + EPS)) * g_ref[...]
    xn_ref[...] = _pack_halves(xn)
    logits = jnp.dot(xn.astype(BF16), w_ref[...].astype(BF16), preferred_element_type=F32) + b_ref[...]
    tm = logits.shape[0]
    col = lax.broadcasted_iota(jnp.int32, (tm, ROUTER_LANES), 1).astype(F32)
    neg = jnp.float32(-jnp.inf)
    first = lambda mask: jnp.min(jnp.where(mask, col, float(ROUTER_LANES)), axis=-1, keepdims=True)
    gl = jnp.where(col < MOE_GROUPS, logits, neg)
    gmax = jnp.max(gl, axis=-1, keepdims=True)
    gidx = first(gl == gmax)
    g_w = 1.0 / jnp.sum(jnp.exp(gl - gmax), axis=-1, keepdims=True)
    lo = MOE_GROUPS + gidx * MOE_EPG
    el = jnp.where((col >= lo) & (col < lo + MOE_EPG), logits, neg)
    m1 = jnp.max(el, axis=-1, keepdims=True)
    i1 = first(el == m1)
    z = jnp.sum(jnp.exp(el - m1), axis=-1, keepdims=True)
    p1 = 1.0 / z
    el2 = jnp.where(col == i1, neg, el)
    m2 = jnp.max(el2, axis=-1, keepdims=True)
    i2 = first(el2 == m2)
    p2 = jnp.exp(m2 - m1) / z
    tot = p1 + p2
    ids = jnp.where(col == 0, i1 - MOE_GROUPS, jnp.where(col == 1, i2 - MOE_GROUPS, 0.0))
    ids_ref[...] = ids.astype(jnp.int32)
    gates_ref[...] = jnp.where(col == 0, g_w * p1 / tot, jnp.where(col == 1, g_w * p2 / tot, 0.0))


def moe_router(x, g, w_group, b_group, w_expert, b_expert, tm=256):
    m, d = x.shape
    npad = ROUTER_LANES - MOE_GROUPS - MOE_EXPERTS
    w = jnp.pad(jnp.concatenate([w_group, w_expert], axis=1).astype(F32), ((0, 0), (0, npad)))
    b = jnp.pad(jnp.concatenate([b_group, b_expert]).astype(F32), (0, npad)).reshape(1, ROUTER_LANES)
    row = lambda n: pl.BlockSpec((tm, n), lambda i: (i, 0))
    xn, ids, gates = pl.pallas_call(
        _router_body,
        out_shape=[jax.ShapeDtypeStruct((m, d // 2), jnp.uint32),
                   jax.ShapeDtypeStruct((m, ROUTER_LANES), jnp.int32),
                   jax.ShapeDtypeStruct((m, ROUTER_LANES), F32)],
        grid=(m // tm,),
        in_specs=[row(d), pl.BlockSpec((1, d), lambda i: (0, 0)),
                  pl.BlockSpec((d, ROUTER_LANES), lambda i: (0, 0)),
                  pl.BlockSpec((1, ROUTER_LANES), lambda i: (0, 0))],
        out_specs=[row(d // 2), row(ROUTER_LANES), row(ROUTER_LANES)],
        compiler_params=_cparams(("arbitrary",)),
        name="moe_router",
    )(x, g.reshape(1, d).astype(F32), w, b)
    return xn, ids[:, :MOE_TOP_K], gates


def _gather_body(tm, src_ref, cnt_ref, x_hbm, o_ref, buf_ref, sem):
    i = pl.program_id(0)
    n_groups = cnt_ref[i]

    @pl.when(n_groups < tm // GATHER_GROUP)
    def _():
        buf_ref[...] = jnp.zeros_like(buf_ref)

    def copy(r):
        return pltpu.make_async_copy(x_hbm.at[pl.ds(src_ref[i * tm + r], 1), :],
                                     buf_ref.at[pl.ds(r, 1), :], sem)

    def start(g, _):
        for u in range(GATHER_GROUP):
            copy(g * GATHER_GROUP + u).start(priority=u % DMA_PRIORITIES)
        return 0

    def wait(g, _):
        for u in range(GATHER_GROUP):
            copy(g * GATHER_GROUP + u).wait()
        return 0

    lax.fori_loop(0, n_groups, start, 0)
    lax.fori_loop(0, n_groups, wait, 0)
    lo, hi = _unpack_halves(buf_ref[...])
    half = buf_ref.shape[1]
    o_ref[:, :half] = lo.astype(o_ref.dtype)
    o_ref[:, half:] = hi.astype(o_ref.dtype)


def moe_gather(x, row_src, tile_groups, n_tiles, tm):
    half = x.shape[1]
    d = 2 * half
    return pl.pallas_call(
        functools.partial(_gather_body, tm),
        out_shape=jax.ShapeDtypeStruct((n_tiles * tm, d), BF16),
        grid_spec=pltpu.PrefetchScalarGridSpec(
            num_scalar_prefetch=2,
            grid=(n_tiles,),
            in_specs=[pl.BlockSpec(memory_space=pl.ANY)],
            out_specs=pl.BlockSpec((tm, d), lambda i, src, cnt: (i, 0)),
            scratch_shapes=[pltpu.VMEM((tm, half), x.dtype), pltpu.SemaphoreType.DMA(())],
        ),
        compiler_params=_cparams(("arbitrary",)),
        name="moe_gather",
    )(row_src, tile_groups, x)


def _segment_weights(te_ref, seg_ref, meta_ref, copies, cast):
    c = pl.program_id(0)
    r = pl.program_id(1)
    nt_used = meta_ref[0]
    nseg = meta_ref[1]
    used = r < nt_used
    seg = seg_ref[MOE_EXPERTS + r]
    first = jnp.logical_and(used, jnp.logical_or(r == 0, te_ref[r] != te_ref[jnp.maximum(r - 1, 0)]))
    g = c * nseg + seg
    slot = lax.rem(g, 2)

    @pl.when(first)
    def _():
        @pl.when(g == 0)
        def _():
            for cp in copies(te_ref[r], c, 0):
                cp.start()

        for cp in copies(te_ref[r], c, slot):
            cp.wait()
        wraps = seg + 1 >= nseg
        nxt_seg = jnp.where(wraps, 0, seg + 1)
        nxt_c = jnp.where(wraps, c + 1, c)

        @pl.when(nxt_c < pl.num_programs(0))
        def _():
            for cp in copies(seg_ref[nxt_seg], nxt_c, 1 - slot):
                cp.start()

        cast(slot)

    return used


def _moe_up_body(layer, tf, te_ref, seg_ref, meta_ref, x_ref, w_hbm, h_ref, wbuf, wgb_ref, wub_ref, sem):
    f = MOE_HIDDEN

    def copies(e, c, slot):
        col = pl.multiple_of(c * tf, tf)
        return [pltpu.make_async_copy(w_hbm.at[layer, e, :, pl.ds(half * f + col, tf)], wbuf.at[slot, half],
                                      sem.at[slot, half]) for half in range(2)]

    def cast(slot):
        wgb_ref[...] = wbuf[slot, 0].astype(BF16)
        wub_ref[...] = wbuf[slot, 1].astype(BF16)

    used = _segment_weights(te_ref, seg_ref, meta_ref, copies, cast)

    @pl.when(used)
    def _():
        x = x_ref[...]
        gate = jnp.dot(x, wgb_ref[...], preferred_element_type=F32)
        up = jnp.dot(x, wub_ref[...], preferred_element_type=F32)
        h_ref[...] = (jax.nn.silu(gate) * up).astype(h_ref.dtype)

    @pl.when(jnp.logical_not(used))
    def _():
        h_ref[...] = jnp.zeros_like(h_ref)


def _moe_down_body(layer, tn, te_ref, seg_ref, meta_ref, h_ref, w_hbm, y_ref, wbuf, wdb_ref, sem):
    def copies(e, c, slot):
        col = pl.multiple_of(c * tn, tn)
        return [pltpu.make_async_copy(w_hbm.at[layer, e, :, pl.ds(col, tn)], wbuf.at[slot], sem.at[slot])]

    def cast(slot):
        wdb_ref[...] = wbuf[slot].astype(BF16)

    used = _segment_weights(te_ref, seg_ref, meta_ref, copies, cast)

    @pl.when(used)
    def _():
        y_ref[...] = _pack_halves(jnp.dot(h_ref[...], wdb_ref[...], preferred_element_type=F32))

    @pl.when(jnp.logical_not(used))
    def _():
        y_ref[...] = jnp.zeros_like(y_ref)


def moe_experts(xs, tile_e, seg_info, meta, w_gate_up, w_down, layer, tm, tf=512, tnd=D_MODEL):
    p, d = xs.shape
    n_tiles = p // tm
    f = MOE_HIDDEN
    h = pl.pallas_call(
        functools.partial(_moe_up_body, layer, tf),
        out_shape=jax.ShapeDtypeStruct((p, f), BF16),
        grid_spec=pltpu.PrefetchScalarGridSpec(
            num_scalar_prefetch=3,
            grid=(f // tf, n_tiles),
            in_specs=[pl.BlockSpec((tm, d), lambda c, r, te, sg, mt: (r, 0)),
                      pl.BlockSpec(memory_space=pl.ANY)],
            out_specs=pl.BlockSpec((tm, tf), lambda c, r, te, sg, mt: (r, c)),
            scratch_shapes=[pltpu.VMEM((2, 2, d, tf), F32), pltpu.VMEM((d, tf), BF16), pltpu.VMEM((d, tf), BF16),
                            pltpu.SemaphoreType.DMA((2, 2))],
        ),
        compiler_params=_cparams(("arbitrary", "arbitrary")),
        name="moe_up",
    )(tile_e, seg_info, meta, xs, w_gate_up)
    assert tnd == d
    return pl.pallas_call(
        functools.partial(_moe_down_body, layer, tnd),
        out_shape=jax.ShapeDtypeStruct((p, d // 2), jnp.uint32),
        grid_spec=pltpu.PrefetchScalarGridSpec(
            num_scalar_prefetch=3,
            grid=(d // tnd, n_tiles),
            in_specs=[pl.BlockSpec((tm, f), lambda c, r, te, sg, mt: (r, 0)),
                      pl.BlockSpec(memory_space=pl.ANY)],
            out_specs=pl.BlockSpec((tm, tnd // 2), lambda c, r, te, sg, mt: (r, c)),
            scratch_shapes=[pltpu.VMEM((2, f, tnd), F32), pltpu.VMEM((f, tnd), BF16),
                            pltpu.SemaphoreType.DMA((2,))],
        ),
        compiler_params=_cparams(("arbitrary", "arbitrary")),
        name="moe_down",
    )(tile_e, seg_info, meta, h, w_down)


def _combine_body(tt, emit_norm, pos_ref, x_ref, g_ref, y_hbm, *refs):
    if emit_norm:
        gn_ref, o_ref, xn_ref, buf_ref, sem = refs
    else:
        o_ref, buf_ref, sem = refs
    i = pl.program_id(0)

    def copy(t, k):
        return pltpu.make_async_copy(y_hbm.at[pl.ds(pos_ref[(i * tt + t) * MOE_TOP_K + k], 1), :],
                                     buf_ref.at[k, pl.ds(t, 1), :], sem)

    def start(t, _):
        for k in range(MOE_TOP_K):
            copy(t, k).start(priority=k % DMA_PRIORITIES)
        return 0

    def wait(t, _):
        for k in range(MOE_TOP_K):
            copy(t, k).wait()
        return 0

    lax.fori_loop(0, tt, start, 0)
    lax.fori_loop(0, tt, wait, 0)
    g = g_ref[...]
    half = buf_ref.shape[2]
    lo0, hi0 = _unpack_halves(buf_ref[0])
    lo1, hi1 = _unpack_halves(buf_ref[1])
    x_lo = x_ref[:, :half] + (g[:, 0:1] * lo0 + g[:, 1:2] * lo1)
    x_hi = x_ref[:, half:] + (g[:, 0:1] * hi0 + g[:, 1:2] * hi1)
    o_ref[:, :half] = x_lo
    o_ref[:, half:] = x_hi
    if emit_norm:
        ssq = jnp.sum(x_lo * x_lo, axis=-1, keepdims=True) + jnp.sum(x_hi * x_hi, axis=-1, keepdims=True)
        inv = lax.rsqrt(ssq / (2 * half) + EPS)
        xn_ref[:, :half] = ((x_lo * inv) * gn_ref[:, :half]).astype(xn_ref.dtype)
        xn_ref[:, half:] = ((x_hi * inv) * gn_ref[:, half:]).astype(xn_ref.dtype)


def moe_combine(x, gates, pos, y_sorted, norm_g=None, tt=256):
    m, d = x.shape
    row = pl.BlockSpec((tt, d), lambda i, pos: (i, 0))
    in_specs = [row, pl.BlockSpec((tt, ROUTER_LANES), lambda i, pos: (i, 0)), pl.BlockSpec(memory_space=pl.ANY)]
    args = [pos, x, gates, y_sorted]
    out_shape = [jax.ShapeDtypeStruct((m, d), F32)]
    if norm_g is not None:
        in_specs.append(pl.BlockSpec((1, d), lambda i, pos: (0, 0)))
        args.append(norm_g.reshape(1, d).astype(F32))
        out_shape.append(jax.ShapeDtypeStruct((m, d), BF16))
    res = pl.pallas_call(
        functools.partial(_combine_body, tt, norm_g is not None),
        out_shape=out_shape,
        grid_spec=pltpu.PrefetchScalarGridSpec(
            num_scalar_prefetch=1,
            grid=(m // tt,),
            in_specs=in_specs,
            out_specs=[row] * len(out_shape),
            scratch_shapes=[pltpu.VMEM((MOE_TOP_K, tt, d // 2), y_sorted.dtype), pltpu.SemaphoreType.DMA(())],
        ),
        compiler_params=_cparams(("arbitrary",)),
        name="moe_combine",
    )(*args)
    return res if norm_g is not None else res[0]


def _moe_plan(ids, tm):
    t = ids.shape[0]
    n = t * MOE_TOP_K
    n_tiles = -(-(n + MOE_EXPERTS * (tm - 1)) // tm)
    i32 = jnp.int32
    flat_e = ids.reshape(n)
    order = jnp.argsort(flat_e, stable=True).astype(i32)
    inv = jnp.argsort(order).astype(i32)
    se = flat_e[order]
    experts = jnp.arange(MOE_EXPERTS, dtype=i32)
    starts = jnp.searchsorted(se, experts, side="left", method="compare_all").astype(i32)
    counts = jnp.searchsorted(se, experts, side="right", method="compare_all").astype(i32) - starts
    tile_count = (counts + tm - 1) // tm
    tile_end = jnp.cumsum(tile_count).astype(i32)
    tile_start = tile_end - tile_count
    pos = tile_start[flat_e] * tm + (inv - starts[flat_e])
    tiles = jnp.arange(n_tiles, dtype=i32)
    tile_e = jnp.minimum(jnp.searchsorted(tile_end, tiles, side="right", method="compare_all"),
                         MOE_EXPERTS - 1).astype(i32)
    tile_valid = jnp.clip(counts[tile_e] - (tiles - tile_start[tile_e]) * tm, 0, tm)
    tile_groups = (tile_valid + GATHER_GROUP - 1) // GATHER_GROUP
    within = jnp.arange(tm, dtype=i32)[None, :]
    slot = (starts[tile_e] + (tiles - tile_start[tile_e]) * tm)[:, None] + within
    src = order[jnp.clip(slot, 0, n - 1)] // MOE_TOP_K
    row_src = jnp.where(within < tile_valid[:, None], src, 0).reshape(n_tiles * tm)
    nonempty = tile_count > 0
    seg_of_expert = jnp.cumsum(nonempty.astype(i32)).astype(i32) - 1
    seg_expert = jnp.argsort(jnp.where(nonempty, 0, 1).astype(i32), stable=True).astype(i32)
    seg_info = jnp.concatenate([seg_expert, seg_of_expert[tile_e]])
    meta = jnp.stack([tile_end[-1], jnp.sum(nonempty.astype(i32))]).astype(i32)
    return row_src, pos.astype(i32), tile_e, seg_info, meta, tile_groups.astype(i32), n_tiles


def hier_moe(x, g, layer, w_group, b_group, w_expert, b_expert, w_gate_up, w_down, next_norm_g=None):
    xn, ids, gates = moe_router(x, g, w_group, b_group, w_expert, b_expert)
    row_src, pos, tile_e, seg_info, meta, tile_groups, n_tiles = _moe_plan(ids, MOE_TILE)
    xs = moe_gather(xn, row_src, tile_groups, n_tiles, MOE_TILE)
    y = moe_experts(xs, tile_e, seg_info, meta, w_gate_up, w_down, layer, MOE_TILE)
    return moe_combine(x, gates, pos, y, norm_g=next_norm_g)


def kernel(x_prompt, x_sample, mem_prompt, state_ssm_re, state_ssm_im, state_gla, cache_mem_k, cache_mem_v, norm_mixer, norm_ca, norm_moe, norm_final, ssm_lambda_re, ssm_lambda_im, ssm_log_dt, ssm_b_re, ssm_b_im, ssm_c_re, ssm_c_im, ssm_d, ssm_w_glu, gla_w_in, gla_w_gate_up, gla_b_gate, gla_norm, gla_w_out, ca_mem_norm, ca_w_q, ca_w_kv, ca_w_o, moe_w_group, moe_b_group, moe_w_expert, moe_b_expert, moe_w_gate_up, moe_w_down):
    bp, lp, d = x_prompt.shape
    bs, ls, _ = x_sample.shape
    np_ = bp * lp
    ns_ = bs * ls

    mem = mem_prompt.reshape(bp * N_MEM, d)
    pk, pv = [], []
    for i in range(DEPTH):
        mn = rmsnorm(mem, ca_mem_norm[i])
        kv = matmul(mn, ca_w_kv, i, out_dtype=F32)
        pk.append(kv[:, :d].reshape(bp, N_MEM, d))
        pv.append(kv[:, d:].reshape(bp, N_MEM, d))
    prompt_mem_k = jnp.stack(pk).reshape(DEPTH, bp, N_MEM, CA_HEADS, CA_HEAD_DIM)
    prompt_mem_v = jnp.stack(pv).reshape(DEPTH, bp, N_MEM, CA_HEADS, CA_HEAD_DIM)

    x = jnp.concatenate([x_prompt.reshape(np_, d), x_sample.reshape(ns_, d)], axis=0)
    zero_ssm = jnp.zeros((bp, SSM_GROUPS, SSM_STATE), F32)
    zero_gla = jnp.zeros((bp, GLA_HEADS, GLA_DK_HEAD, GLA_DV_HEAD), F32)
    zero_act = jnp.zeros((np_ + ns_, d), BF16)
    p_re, p_im, s_re, s_im, p_gla, s_gla = [], [], [], [], [], []
    xn_mixer = None

    for i in range(DEPTH):
        j = i // 2
        if i % 2 == 0:
            xn_p = rmsnorm(x, norm_mixer[i], out_dtype=F32, row0=0, nrows=np_)
            xn_s = rmsnorm(x, norm_mixer[i], out_dtype=F32, row0=np_, nrows=ns_)
            prm = (ssm_lambda_re[j], ssm_lambda_im[j], ssm_log_dt[j], ssm_b_re[j], ssm_b_im[j],
                   ssm_c_re[j], ssm_c_im[j], ssm_d[j])
            zp, hr, hi = s5_layer(xn_p.reshape(bp, lp, d), zero_ssm, zero_ssm, prm)
            p_re.append(hr)
            p_im.append(hi)
            zs, hr, hi = s5_layer(xn_s.reshape(bs, ls, d), state_ssm_re[j], state_ssm_im[j], prm)
            s_re.append(hr)
            s_im.append(hi)
            z = jnp.concatenate([zp, zs], axis=0)
            x = matmul(z, ssm_w_glu, j, mode="glu_res", res=x, n_out=d, tn=256)
        else:
            xn = xn_mixer if xn_mixer is not None else rmsnorm(x, norm_mixer[i])
            w_in_t = jnp.swapaxes(gla_w_in, 1, 2)
            proj = matmul(xn, w_in_t, j, out_dtype=F32, n_out=GLA_MAIN, w_is_nk=True)
            la = gla_gate(xn, w_in_t[j, GLA_MAIN:, :], gla_w_gate_up[j], gla_b_gate[j])
            o, st = gla_recurrence(proj, la, 0, bp, lp, zero_gla, gla_norm[j], prev=zero_act)
            p_gla.append(st)
            o, st = gla_recurrence(proj, la, np_, bs, ls, state_gla[j], gla_norm[j], prev=o)
            s_gla.append(st)
            x = matmul(o, gla_w_out, j, mode="res", res=x)

        xn = rmsnorm(x, norm_ca[i])
        q = matmul(xn, ca_w_q, i, out_dtype=BF16)
        att = mem_attention(q, 0, bp, lp, pk[i], pv[i], prev=zero_act)
        att = mem_attention(q, np_, bs, ls, cache_mem_k, cache_mem_v, layer=i, prev=att)
        x = matmul(att, ca_w_o, i, mode="res", res=x)

        gla_next = i + 1 < DEPTH and (i + 1) % 2 == 1
        res = hier_moe(x, norm_moe[i], i, moe_w_group[i], moe_b_group[i], moe_w_expert[i], moe_b_expert[i],
                       moe_w_gate_up, moe_w_down, next_norm_g=norm_mixer[i + 1] if gla_next else None)
        x, xn_mixer = res if gla_next else (res, None)

    y_prompt = rmsnorm(x, norm_final, out_dtype=F32, row0=0, nrows=np_)
    y_sample = rmsnorm(x, norm_final, out_dtype=F32, row0=np_, nrows=ns_)
    return (y_prompt.reshape(bp, lp, d), y_sample.reshape(bs, ls, d),
            jnp.stack(p_re), jnp.stack(p_im), jnp.stack(p_gla), prompt_mem_k, prompt_mem_v,
            jnp.stack(s_re), jnp.stack(s_im), jnp.stack(s_gla))
```

```python
import functools
import math

import jax
import jax.numpy as jnp
import numpy as np
from jax import lax
from jax.experimental import pallas as pl
from jax.experimental.pallas import tpu as pltpu

F32 = jnp.float32
BF16 = jnp.bfloat16

EPS = 1e-6
D_MODEL = 4096
DEPTH = 2

SSM_GROUP_CH = 16
SSM_GROUPS = D_MODEL // SSM_GROUP_CH
SSM_STATE = 64
SSM_SEQS = 32
SSM_SEQ_TILE = 32
SSM_SCAN_UNROLL = True
SSM_BLK_CH = 128
SSM_BLK_GROUPS = SSM_BLK_CH // SSM_GROUP_CH
SSM_NBLK = D_MODEL // SSM_BLK_CH
SSM_BLK_STATE = SSM_BLK_GROUPS * SSM_STATE

GLA_HEADS = 4
GLA_DK = D_MODEL // 2
GLA_DV = D_MODEL
GLA_DK_HEAD = GLA_DK // GLA_HEADS
GLA_DV_HEAD = GLA_DV // GLA_HEADS
GLA_GATE_RANK = 16
GLA_TAU = 16.0
GLA_MAIN = 2 * GLA_DK + 2 * GLA_DV
GLA_CHUNK = 64
GLA_HEADS_PER_STEP = 4

N_MEM = 256
CA_HEADS = 4
CA_HEAD_DIM = D_MODEL // CA_HEADS

MOE_GROUPS = 4
MOE_EPG = 8
MOE_EXPERTS = MOE_GROUPS * MOE_EPG
MOE_TOP_K = 2
MOE_HIDDEN = D_MODEL // 4
MOE_TILE = 256
GATHER_GROUP = 8
DMA_PRIORITIES = 2
ROUTER_LANES = 128

DENSE_TM = 1088
SUBLANES = 8
VMEM_LIMIT = 56 * 1024 * 1024


def _cparams(sem):
    return pltpu.CompilerParams(dimension_semantics=sem, vmem_limit_bytes=VMEM_LIMIT)


def _rms_body(n_add, emit_sum, x_ref, *refs):
    adds = refs[:n_add]
    g_ref = refs[n_add]
    outs = refs[n_add + 1:]
    x = x_ref[...].astype(F32)
    for a in adds:
        x = x + a[...].astype(F32)
    ms = jnp.mean(x * x, axis=-1, keepdims=True)
    xn = (x * lax.rsqrt(ms + EPS)) * g_ref[...]
    if emit_sum:
        outs[0][...] = x
        outs[1][...] = xn.astype(outs[1].dtype)
    else:
        outs[0][...] = xn.astype(outs[0].dtype)


def rmsnorm(x, g, adds=(), out_dtype=BF16, emit_sum=False, tm=256, row0=0, nrows=None):
    d = x.shape[1]
    m = x.shape[0] if nrows is None else nrows
    assert m % tm == 0 and row0 % tm == 0
    b0 = row0 // tm
    src = pl.BlockSpec((tm, d), lambda i: (b0 + i, 0))
    row = pl.BlockSpec((tm, d), lambda i: (i, 0))
    out_shape = [jax.ShapeDtypeStruct((m, d), out_dtype)]
    out_specs = [row]
    if emit_sum:
        out_shape.insert(0, jax.ShapeDtypeStruct((m, d), F32))
        out_specs.insert(0, row)
    res = pl.pallas_call(
        functools.partial(_rms_body, len(adds), emit_sum),
        out_shape=out_shape,
        grid=(m // tm,),
        in_specs=[src] * (1 + len(adds)) + [pl.BlockSpec((1, d), lambda i: (0, 0))],
        out_specs=out_specs,
        compiler_params=_cparams(("arbitrary",)),
        name="rmsnorm",
    )(x, *adds, g.reshape(1, d).astype(F32))
    return res if emit_sum else res[0]


def _mm_body(mode, w_is_nk, x_ref, *refs):
    if mode == "glu_res":
        w_ref, w2_ref, res_ref, o_ref, wb_ref, wb2_ref = refs
    elif mode == "res":
        w_ref, res_ref, o_ref, wb_ref = refs
    else:
        w_ref, o_ref, wb_ref = refs

    @pl.when(pl.program_id(1) == 0)
    def _():
        wb_ref[...] = w_ref[...].astype(BF16)
        if mode == "glu_res":
            wb2_ref[...] = w2_ref[...].astype(BF16)

    x = x_ref[...]
    if w_is_nk:
        acc = lax.dot_general(x, wb_ref[...], (((1,), (1,)), ((), ())), preferred_element_type=F32)
    else:
        acc = jnp.dot(x, wb_ref[...], preferred_element_type=F32)
    if mode == "glu_res":
        gate = jnp.dot(x, wb2_ref[...], preferred_element_type=F32)
        o_ref[...] = res_ref[...] + acc * jax.nn.sigmoid(gate)
    elif mode == "res":
        o_ref[...] = res_ref[...] + acc
    else:
        o_ref[...] = acc.astype(o_ref.dtype)


def matmul(x, w, layer, mode="plain", res=None, out_dtype=F32, n_out=None, tm=DENSE_TM, tn=512, w_is_nk=False):
    m, k = x.shape
    n = n_out if n_out is not None else w.shape[1 if w_is_nk else 2]
    if m % tm:
        tm = m
    assert m % tm == 0 and n % tn == 0 and not (w_is_nk and mode != "plain")
    nb = n // tn
    if w_is_nk:
        w_spec, w_tile = pl.BlockSpec((None, tn, k), lambda j, i: (layer, j, 0)), (tn, k)
    else:
        w_spec, w_tile = pl.BlockSpec((None, k, tn), lambda j, i: (layer, 0, j)), (k, tn)
    in_specs = [pl.BlockSpec((tm, k), lambda j, i: (i, 0)), w_spec]
    args = [x, w]
    scratch = [pltpu.VMEM(w_tile, BF16)]
    if mode == "glu_res":
        in_specs.append(pl.BlockSpec((None, k, tn), lambda j, i: (layer, 0, nb + j)))
        args.append(w)
        scratch.append(pltpu.VMEM((k, tn), BF16))
    if mode in ("res", "glu_res"):
        in_specs.append(pl.BlockSpec((tm, tn), lambda j, i: (i, j)))
        args.append(res)
        out_dtype = F32
    return pl.pallas_call(
        functools.partial(_mm_body, mode, w_is_nk),
        out_shape=jax.ShapeDtypeStruct((m, n), out_dtype),
        grid=(nb, m // tm),
        in_specs=in_specs,
        out_specs=pl.BlockSpec((tm, tn), lambda j, i: (i, j)),
        scratch_shapes=scratch,
        compiler_params=_cparams(("arbitrary", "arbitrary")),
        name="matmul_" + mode,
    )(*args)


def _s5_body(lt, emit_y, u_ref, h0_ref, ar_ref, ai_ref, bd_ref, *refs):
    if emit_y:
        cbd_ref, d_ref, z_ref, ht_ref, h_ref, bu_ref, hs_ref = refs
    else:
        ht_ref, h_ref, bu_ref = refs
    tb = pl.program_id(2)
    ns = SSM_BLK_STATE

    @pl.when(tb == 0)
    def _():
        h_ref[...] = h0_ref[...]

    st = SSM_SEQ_TILE
    groups = range(st // SUBLANES)
    u = u_ref[...].reshape(lt * st, SSM_BLK_CH)
    bu_ref[...] = jnp.dot(u.astype(BF16), bd_ref[...], preferred_element_type=F32)
    ar = jnp.broadcast_to(ar_ref[...], (SUBLANES, ns))
    ai = jnp.broadcast_to(ai_ref[...], (SUBLANES, ns))

    def step(t, carry):
        r0 = t * st
        out = []
        for g in groups:
            hr, hi = carry[g]
            rows = pl.ds(pl.multiple_of(r0 + g * SUBLANES, SUBLANES), SUBLANES)
            nr = ar * hr - ai * hi + bu_ref[rows, 0:ns]
            ni = ar * hi + ai * hr + bu_ref[rows, ns:2 * ns]
            if emit_y:
                hs_ref[rows, 0:ns] = nr
                hs_ref[rows, ns:2 * ns] = ni
            out.append((nr, ni))
        return tuple(out)

    grp = lambda g: slice(g * SUBLANES, (g + 1) * SUBLANES)
    init = tuple((h_ref[grp(g), 0:ns], h_ref[grp(g), ns:2 * ns]) for g in groups)
    final = lax.fori_loop(0, lt, step, init, unroll=SSM_SCAN_UNROLL)
    for g in groups:
        hr, hi = final[g]
        h_ref[grp(g), 0:ns] = hr
        h_ref[grp(g), ns:2 * ns] = hi
        ht_ref[grp(g), 0:ns] = hr
        ht_ref[grp(g), ns:2 * ns] = hi
    if emit_y:
        y = jnp.dot(hs_ref[...].astype(BF16), cbd_ref[...], preferred_element_type=F32)
        y = y + d_ref[...] * u
        z_ref[...] = jax.nn.gelu(y, approximate=True).reshape(lt, st, SSM_BLK_CH)


def _s5_scan(ut, h0, ar, ai, bd, cbd, dsk, emit_y):
    ltot = ut.shape[0]
    lt = min(ltot, 64)
    assert ltot % lt == 0
    ns2 = 2 * SSM_BLK_STATE
    st = SSM_SEQ_TILE
    nsg = SSM_SEQS // st
    in_specs = [
        pl.BlockSpec((lt, st, SSM_BLK_CH), lambda c, s, t: (t, s, c)),
        pl.BlockSpec((None, st, ns2), lambda c, s, t: (c, s, 0)),
        pl.BlockSpec((None, 1, SSM_BLK_STATE), lambda c, s, t: (c, 0, 0)),
        pl.BlockSpec((None, 1, SSM_BLK_STATE), lambda c, s, t: (c, 0, 0)),
        pl.BlockSpec((None, SSM_BLK_CH, ns2), lambda c, s, t: (c, 0, 0)),
    ]
    args = [ut, h0, ar, ai, bd]
    ht_shape = jax.ShapeDtypeStruct((SSM_NBLK, SSM_SEQS, ns2), F32)
    ht_spec = pl.BlockSpec((None, st, ns2), lambda c, s, t: (c, s, 0))
    scratch = [pltpu.VMEM((st, ns2), F32), pltpu.VMEM((lt * st, ns2), F32)]
    if emit_y:
        in_specs += [pl.BlockSpec((None, ns2, SSM_BLK_CH), lambda c, s, t: (c, 0, 0)),
                     pl.BlockSpec((None, 1, SSM_BLK_CH), lambda c, s, t: (c, 0, 0))]
        args += [cbd, dsk]
        out_shape = [jax.ShapeDtypeStruct(ut.shape, F32), ht_shape]
        out_specs = [pl.BlockSpec((lt, st, SSM_BLK_CH), lambda c, s, t: (t, s, c)), ht_spec]
        scratch.append(pltpu.VMEM((lt * st, ns2), F32))
    else:
        out_shape = [ht_shape]
        out_specs = [ht_spec]
    res = pl.pallas_call(
        functools.partial(_s5_body, lt, emit_y),
        out_shape=out_shape,
        grid=(SSM_NBLK, nsg, ltot // lt),
        in_specs=in_specs,
        out_specs=out_specs,
        scratch_shapes=scratch,
        compiler_params=_cparams(("arbitrary", "arbitrary", "arbitrary")),
        name="s5_scan_y" if emit_y else "s5_scan_state",
    )(*args)
    return (res[0], res[1]) if emit_y else (None, res[0])


def _s5_carry_body(nb, nseg, e_ref, h0_ref, ar_ref, ai_ref, hs_ref, he_ref):
    ns = SSM_BLK_STATE
    ar = ar_ref[...]
    ai = ai_ref[...]
    for b in range(nb):
        hr = h0_ref[b:b + 1, 0:ns]
        hi = h0_ref[b:b + 1, ns:2 * ns]
        for j in range(nseg):
            s = b * nseg + j
            hs_ref[s:s + 1, 0:ns] = hr
            hs_ref[s:s + 1, ns:2 * ns] = hi
            er = e_ref[s:s + 1, 0:ns]
            ei = e_ref[s:s + 1, ns:2 * ns]
            hr, hi = ar * hr - ai * hi + er, ar * hi + ai * hr + ei
        he_ref[b:b + 1, 0:ns] = hr
        he_ref[b:b + 1, ns:2 * ns] = hi


def _s5_carry(e0, h0, ar_seg, ai_seg, nb, nseg):
    ns2 = 2 * SSM_BLK_STATE
    return pl.pallas_call(
        functools.partial(_s5_carry_body, nb, nseg),
        out_shape=[jax.ShapeDtypeStruct((SSM_NBLK, SSM_SEQS, ns2), F32),
                   jax.ShapeDtypeStruct((SSM_NBLK, nb, ns2), F32)],
        grid=(SSM_NBLK,),
        in_specs=[pl.BlockSpec((None, SSM_SEQS, ns2), lambda c: (c, 0, 0)),
                  pl.BlockSpec((None, nb, ns2), lambda c: (c, 0, 0)),
                  pl.BlockSpec((None, 1, SSM_BLK_STATE), lambda c: (c, 0, 0)),
                  pl.BlockSpec((None, 1, SSM_BLK_STATE), lambda c: (c, 0, 0))],
        out_specs=[pl.BlockSpec((None, SSM_SEQS, ns2), lambda c: (c, 0, 0)),
                   pl.BlockSpec((None, nb, ns2), lambda c: (c, 0, 0))],
        compiler_params=_cparams(("arbitrary",)),
        name="s5_carry",
    )(e0, h0, ar_seg, ai_seg)


def _s5_params(lam_re, lam_im, log_dt, b_re, b_im, c_re, c_im, d_skip, seg_len):
    dt = jnp.exp(log_dt.astype(F32))[:, None]
    lr = lam_re.astype(F32)
    li = lam_im.astype(F32)
    mag = jnp.exp(lr * dt)
    ab_re = mag * jnp.cos(li * dt)
    ab_im = mag * jnp.sin(li * dt)
    nr = ab_re - 1.0
    ni = ab_im
    den = lr * lr + li * li
    f_re = (nr * lr + ni * li) / den
    f_im = (ni * lr - nr * li) / den
    br = b_re.astype(F32)
    bi = b_im.astype(F32)
    bb_re = f_re[..., None] * br - f_im[..., None] * bi
    bb_im = f_re[..., None] * bi + f_im[..., None] * br
    mag_s = jnp.exp(lr * dt * seg_len)
    as_re = mag_s * jnp.cos(li * dt * seg_len)
    as_im = mag_s * jnp.sin(li * dt * seg_len)
    eye = jnp.eye(SSM_BLK_GROUPS, dtype=F32)

    def in_proj(w):
        w = w.reshape(SSM_NBLK, SSM_BLK_GROUPS, SSM_STATE, SSM_GROUP_CH).transpose(0, 1, 3, 2)
        return jnp.einsum("bjcp,jk->bjckp", w, eye).reshape(SSM_NBLK, SSM_BLK_CH, SSM_BLK_STATE)

    def out_proj(w):
        w = w.reshape(SSM_NBLK, SSM_BLK_GROUPS, SSM_GROUP_CH, SSM_STATE).transpose(0, 1, 3, 2)
        return jnp.einsum("bjpc,jk->bjpkc", w, eye).reshape(SSM_NBLK, SSM_BLK_STATE, SSM_BLK_CH)

    bd = jnp.concatenate([in_proj(bb_re), in_proj(bb_im)], axis=-1).astype(BF16)
    cbd = jnp.concatenate([out_proj(c_re.astype(F32)), -out_proj(c_im.astype(F32))], axis=1).astype(BF16)
    blk = lambda v: v.reshape(SSM_NBLK, 1, SSM_BLK_STATE)
    return (blk(ab_re), blk(ab_im), blk(as_re), blk(as_im), bd, cbd,
            d_skip.astype(F32).reshape(SSM_NBLK, 1, SSM_BLK_CH))


def _state_to_blocks(h_re, h_im):
    nb = h_re.shape[0]
    f = lambda h: h.astype(F32).reshape(nb, SSM_NBLK, SSM_BLK_STATE).transpose(1, 0, 2)
    return jnp.concatenate([f(h_re), f(h_im)], axis=-1)


def _blocks_to_state(hb):
    nb = hb.shape[1]
    f = lambda h: h.transpose(1, 0, 2).reshape(nb, SSM_GROUPS, SSM_STATE)
    return f(hb[..., :SSM_BLK_STATE]), f(hb[..., SSM_BLK_STATE:])


def s5_layer(u, h0_re, h0_im, prm):
    nb, L, d = u.shape
    nseg = SSM_SEQS // nb
    seg = L // nseg
    lam_re, lam_im, log_dt, b_re, b_im, c_re, c_im, d_skip = prm
    ar, ai, as_re, as_im, bd, cbd, dsk = _s5_params(lam_re, lam_im, log_dt, b_re, b_im, c_re, c_im, d_skip, seg)
    ut = u.reshape(nb * nseg, seg, d).transpose(1, 0, 2)
    h0b = _state_to_blocks(h0_re, h0_im)
    if nseg == 1:
        zt, hT = _s5_scan(ut, h0b, ar, ai, bd, cbd, dsk, True)
    else:
        zero = jnp.zeros((SSM_NBLK, SSM_SEQS, 2 * SSM_BLK_STATE), F32)
        _, e0 = _s5_scan(ut, zero, ar, ai, bd, None, None, False)
        hstart, hT = _s5_carry(e0, h0b, as_re, as_im, nb, nseg)
        zt, _ = _s5_scan(ut, hstart, ar, ai, bd, cbd, dsk, True)
    z = zt.astype(BF16).transpose(1, 0, 2).reshape(nb * L, d)
    hr, hi = _blocks_to_state(hT)
    return z, hr, hi


def _gla_gate_body(x_ref, w1_ref, w2_ref, b_ref, o_ref):
    glr = lax.dot_general(x_ref[...], w1_ref[...].astype(BF16), (((1,), (1,)), ((), ())),
                          preferred_element_type=F32)
    logits = jnp.dot(glr.astype(BF16), w2_ref[...].astype(BF16), preferred_element_type=F32) + b_ref[...]
    o_ref[...] = jax.nn.log_sigmoid(logits) / GLA_TAU


def gla_gate(xn, w_glr_t, w_gate_up, b_gate, tm=512):
    m, k = xn.shape
    rp = 128
    w1 = jnp.pad(w_glr_t.astype(F32), ((0, rp - GLA_GATE_RANK), (0, 0)))
    w2 = jnp.pad(w_gate_up.astype(F32), ((0, rp - GLA_GATE_RANK), (0, 0)))
    return pl.pallas_call(
        _gla_gate_body,
        out_shape=jax.ShapeDtypeStruct((m, GLA_DK), F32),
        grid=(m // tm,),
        in_specs=[pl.BlockSpec((tm, k), lambda i: (i, 0)),
                  pl.BlockSpec((rp, k), lambda i: (0, 0)),
                  pl.BlockSpec((rp, GLA_DK), lambda i: (0, 0)),
                  pl.BlockSpec((1, GLA_DK), lambda i: (0, 0))],
        out_specs=pl.BlockSpec((tm, GLA_DK), lambda i: (i, 0)),
        compiler_params=_cparams(("arbitrary",)),
        name="gla_gate",
    )(xn, w1, w2, b_gate.reshape(1, GLA_DK).astype(F32))


def _gla_body(lc, q_ref, k_ref, v_ref, r_ref, la_ref, s0_ref, g_ref, o_ref, st_ref, s_ref):
    c = pl.program_id(2)

    @pl.when(c == 0)
    def _():
        s_ref[...] = s0_ref[...].astype(F32)

    row = lax.broadcasted_iota(jnp.int32, (lc, lc), 0)
    col = lax.broadcasted_iota(jnp.int32, (lc, lc), 1)
    tri = (col <= row).astype(BF16)
    ones = jnp.ones((lc, 128), BF16)
    tn = (((0,), (0,)), ((), ()))
    for hh in range(GLA_HEADS_PER_STEP):
        ks = slice(hh * GLA_DK_HEAD, (hh + 1) * GLA_DK_HEAD)
        vs = slice(hh * GLA_DV_HEAD, (hh + 1) * GLA_DV_HEAD)
        la = la_ref[:, ks]
        la_hi = la.astype(BF16)
        la_lo = (la - la_hi.astype(F32)).astype(BF16)
        cum = (jnp.dot(tri, la_hi, preferred_element_type=F32)
               + jnp.dot(tri, la_lo, preferred_element_type=F32))
        end = cum[lc - 1:lc, :]
        kd = (k_ref[:, ks].astype(F32) * jnp.exp(end - cum)).astype(BF16)
        end_col = (lax.dot_general(la_hi, ones, tn, preferred_element_type=F32)
                   + lax.dot_general(la_lo, ones, tn, preferred_element_type=F32))
        decay = jnp.exp(end_col[:, 0:1])
        s_ref[hh] = decay * s_ref[hh] + lax.dot_general(kd, v_ref[:, vs].astype(BF16), tn,
                                                        preferred_element_type=F32)
        qs = (q_ref[:, ks].astype(F32) * (GLA_DK_HEAD ** -0.5)).astype(BF16)
        o = jnp.dot(qs, s_ref[hh].astype(BF16), preferred_element_type=F32)
        o = o * lax.rsqrt(jnp.mean(o * o, axis=-1, keepdims=True) + EPS)
        o = o * g_ref[:, vs]
        o_ref[:, vs] = (o * jax.nn.silu(r_ref[:, vs].astype(F32))).astype(o_ref.dtype)

    @pl.when(c == pl.num_programs(2) - 1)
    def _():
        st_ref[...] = s_ref[...]


def _drop_alias_ref(body, n_in, *refs):
    return body(*refs[:n_in], *refs[n_in + 1:])


def gla_recurrence(proj, la, row0, nb, L, s0, norm_g, prev=None):
    ntok = proj.shape[0]
    lc = min(GLA_CHUNK, L)
    nc = L // lc
    base = row0 // lc
    hp = GLA_HEADS_PER_STEP
    wk, wv = hp * GLA_DK_HEAD, hp * GLA_DV_HEAD
    kq = GLA_DK // wk
    kv = 2 * GLA_DK // wv
    kr = kv + GLA_DV // wv
    rows = lambda b, h, c: base + b * nc + c
    state_spec = pl.BlockSpec((None, hp, GLA_DK_HEAD, GLA_DV_HEAD), lambda b, h, c: (b, h, 0, 0))
    in_specs = [pl.BlockSpec((lc, wk), lambda b, h, c: (rows(b, h, c), h)),
                pl.BlockSpec((lc, wk), lambda b, h, c: (rows(b, h, c), kq + h)),
                pl.BlockSpec((lc, wv), lambda b, h, c: (rows(b, h, c), kv + h)),
                pl.BlockSpec((lc, wv), lambda b, h, c: (rows(b, h, c), kr + h)),
                pl.BlockSpec((lc, wk), lambda b, h, c: (rows(b, h, c), h)),
                state_spec,
                pl.BlockSpec((1, wv), lambda b, h, c: (0, h))]
    args = [proj, proj, proj, proj, la, s0, norm_g.reshape(1, GLA_DV).astype(F32)]
    body = functools.partial(_gla_body, lc)
    aliases = {}
    if prev is not None:
        in_specs.append(pl.BlockSpec(memory_space=pl.ANY))
        args.append(prev)
        aliases = {len(args) - 1: 0}
        body = functools.partial(_drop_alias_ref, body, len(args) - 1)
    o, st = pl.pallas_call(
        body,
        out_shape=[jax.ShapeDtypeStruct((ntok, GLA_DV), BF16),
                   jax.ShapeDtypeStruct((nb, GLA_HEADS, GLA_DK_HEAD, GLA_DV_HEAD), F32)],
        grid=(nb, GLA_HEADS // hp, nc),
        in_specs=in_specs,
        out_specs=[pl.BlockSpec((lc, wv), lambda b, h, c: (rows(b, h, c), h)), state_spec],
        scratch_shapes=[pltpu.VMEM((hp, GLA_DK_HEAD, GLA_DV_HEAD), F32)],
        input_output_aliases=aliases,
        compiler_params=_cparams(("arbitrary", "arbitrary", "arbitrary")),
        name="gla_recurrence",
    )(*args)
    return o, st


def _attn_body(per_head, q_ref, k_ref, v_ref, o_ref):
    nt = (((1,), (1,)), ((), ()))
    scale = CA_HEAD_DIM ** -0.5
    heads = lambda ref: [ref[:, h * CA_HEAD_DIM:(h + 1) * CA_HEAD_DIM] for h in range(CA_HEADS)]

    def softmax(s):
        p = jnp.exp(s - jnp.max(s, axis=-1, keepdims=True))
        return p / jnp.sum(p, axis=-1, keepdims=True)

    if per_head:
        tl = q_ref.shape[0]
        k2 = k_ref[...].reshape(N_MEM * CA_HEADS, CA_HEAD_DIM).astype(BF16)
        v2 = v_ref[...].reshape(N_MEM * CA_HEADS, CA_HEAD_DIM).astype(BF16)
        q4 = jnp.concatenate(heads(q_ref), axis=0)
        s = lax.dot_general(q4, k2, nt, preferred_element_type=F32) * scale
        q_head = lax.broadcasted_iota(jnp.int32, s.shape, 0) // tl
        m_head = lax.broadcasted_iota(jnp.int32, s.shape, 1) % CA_HEADS
        p = softmax(jnp.where(q_head == m_head, s, -jnp.inf))
        o4 = jnp.dot(p.astype(BF16), v2, preferred_element_type=F32).astype(o_ref.dtype)
        for h in range(CA_HEADS):
            o_ref[:, h * CA_HEAD_DIM:(h + 1) * CA_HEAD_DIM] = o4[h * tl:(h + 1) * tl]
    else:
        for h, (qh, kh, vh) in enumerate(zip(heads(q_ref), heads(k_ref), heads(v_ref))):
            s = lax.dot_general(qh, kh.astype(BF16), nt, preferred_element_type=F32) * scale
            o = jnp.dot(softmax(s).astype(BF16), vh.astype(BF16), preferred_element_type=F32)
            o_ref[:, h * CA_HEAD_DIM:(h + 1) * CA_HEAD_DIM] = o.astype(o_ref.dtype)


def mem_attention(q, row0, nb, L, mk, mv, layer=None):
    ntok = q.shape[0]
    tl = min(L, 512)
    nl = L // tl
    base = row0 // tl
    rows = lambda b, i: (base + b * nl + i, 0)
    if layer is None:
        mem_spec = pl.BlockSpec((None, N_MEM, D_MODEL), lambda b, i: (b, 0, 0))
    else:
        mem_spec = pl.BlockSpec((None, None, N_MEM, CA_HEADS, CA_HEAD_DIM), lambda b, i: (layer, b, 0, 0, 0))
    return pl.pallas_call(
        functools.partial(_attn_body, layer is not None),
        out_shape=jax.ShapeDtypeStruct((ntok, D_MODEL), BF16),
        grid=(nb, nl),
        in_specs=[pl.BlockSpec((tl, D_MODEL), rows), mem_spec, mem_spec],
        out_specs=pl.BlockSpec((tl, D_MODEL), rows),
        input_output_aliases={0: 0},
        compiler_params=_cparams(("arbitrary", "arbitrary")),
        name="mem_attention",
    )(q, mk, mv)


def _pack_halves(x):
    n = x.shape[1] // 2
    lo = lax.bitcast_convert_type(x[:, :n].astype(BF16).astype(F32), jnp.uint32)
    hi = lax.bitcast_convert_type(x[:, n:].astype(BF16).astype(F32), jnp.uint32)
    return lax.shift_right_logical(lo, jnp.uint32(16)) | hi


def _unpack_halves(w):
    lo = lax.bitcast_convert_type(lax.shift_left(w, jnp.uint32(16)), F32)
    hi = lax.bitcast_convert_type(w & jnp.uint32(0xFFFF0000), F32)
    return lo, hi


def _router_body(x_ref, g_ref, w_ref, b_ref, xn_ref, ids_ref, gates_ref):
    x = x_ref[...]
    ms = jnp.mean(x * x, axis=-1, keepdims=True)
    xn = (x * lax.rsqrt(ms + EPS)) * g_ref[...]
    xn_ref[...] = _pack_halves(xn)
    logits = jnp.dot(xn.astype(BF16), w_ref[...].astype(BF16), preferred_element_type=F32) + b_ref[...]
    tm = logits.shape[0]
    col = lax.broadcasted_iota(jnp.int32, (tm, ROUTER_LANES), 1).astype(F32)
    neg = jnp.float32(-jnp.inf)
    first = lambda mask: jnp.min(jnp.where(mask, col, float(ROUTER_LANES)), axis=-1, keepdims=True)
    gl = jnp.where(col < MOE_GROUPS, logits, neg)
    gmax = jnp.max(gl, axis=-1, keepdims=True)
    gidx = first(gl == gmax)
    g_w = 1.0 / jnp.sum(jnp.exp(gl - gmax), axis=-1, keepdims=True)
    lo = MOE_GROUPS + gidx * MOE_EPG
    el = jnp.where((col >= lo) & (col < lo + MOE_EPG), logits, neg)
    m1 = jnp.max(el, axis=-1, keepdims=True)
    i1 = first(el == m1)
    z = jnp.sum(jnp.exp(el - m1), axis=-1, keepdims=True)
    p1 = 1.0 / z
    el2 = jnp.where(col == i1, neg, el)
    m2 = jnp.max(el2, axis=-1, keepdims=True)
    i2 = first(el2 == m2)
    p2 = jnp.exp(m2 - m1) / z
    tot = p1 + p2
    ids = jnp.where(col == 0, i1 - MOE_GROUPS, jnp.where(col == 1, i2 - MOE_GROUPS, 0.0))
    ids_ref[...] = ids.astype(jnp.int32)
    gates_ref[...] = jnp.where(col == 0, g_w * p1 / tot, jnp.where(col == 1, g_w * p2 / tot, 0.0))


def moe_router(x, g, w_group, b_group, w_expert, b_expert, tm=256):
    m, d = x.shape
    npad = ROUTER_LANES - MOE_GROUPS - MOE_EXPERTS
    w = jnp.pad(jnp.concatenate([w_group, w_expert], axis=1).astype(F32), ((0, 0), (0, npad)))
    b = jnp.pad(jnp.concatenate([b_group, b_expert]).astype(F32), (0, npad)).reshape(1, ROUTER_LANES)
    row = lambda n: pl.BlockSpec((tm, n), lambda i: (i, 0))
    xn, ids, gates = pl.pallas_call(
        _router_body,
        out_shape=[jax.ShapeDtypeStruct((m, d // 2), jnp.uint32),
                   jax.ShapeDtypeStruct((m, ROUTER_LANES), jnp.int32),
                   jax.ShapeDtypeStruct((m, ROUTER_LANES), F32)],
        grid=(m // tm,),
        in_specs=[row(d), pl.BlockSpec((1, d), lambda i: (0, 0)),
                  pl.BlockSpec((d, ROUTER_LANES), lambda i: (0, 0)),
                  pl.BlockSpec((1, ROUTER_LANES), lambda i: (0, 0))],
        out_specs=[row(d // 2), row(ROUTER_LANES), row(ROUTER_LANES)],
        compiler_params=_cparams(("arbitrary",)),
        name="moe_router",
    )(x, g.reshape(1, d).astype(F32), w, b)
    return xn, ids[:, :MOE_TOP_K], gates


def _gather_body(tm, src_ref, cnt_ref, x_hbm, o_ref, buf_ref, sem):
    i = pl.program_id(0)
    n_groups = cnt_ref[i]

    @pl.when(n_groups < tm // GATHER_GROUP)
    def _():
        buf_ref[...] = jnp.zeros_like(buf_ref)

    def copy(r):
        return pltpu.make_async_copy(x_hbm.at[pl.ds(src_ref[i * tm + r], 1), :],
                                     buf_ref.at[pl.ds(r, 1), :], sem)

    def start(g, _):
        for u in range(GATHER_GROUP):
            copy(g * GATHER_GROUP + u).start(priority=u % DMA_PRIORITIES)
        return 0

    def wait(g, _):
        for u in range(GATHER_GROUP):
            copy(g * GATHER_GROUP + u).wait()
        return 0

    lax.fori_loop(0, n_groups, start, 0)
    lax.fori_loop(0, n_groups, wait, 0)
    lo, hi = _unpack_halves(buf_ref[...])
    half = buf_ref.shape[1]
    o_ref[:, :half] = lo.astype(o_ref.dtype)
    o_ref[:, half:] = hi.astype(o_ref.dtype)


def moe_gather(x, row_src, tile_groups, n_tiles, tm):
    half = x.shape[1]
    d = 2 * half
    return pl.pallas_call(
        functools.partial(_gather_body, tm),
        out_shape=jax.ShapeDtypeStruct((n_tiles * tm, d), BF16),
        grid_spec=pltpu.PrefetchScalarGridSpec(
            num_scalar_prefetch=2,
            grid=(n_tiles,),
            in_specs=[pl.BlockSpec(memory_space=pl.ANY)],
            out_specs=pl.BlockSpec((tm, d), lambda i, src, cnt: (i, 0)),
            scratch_shapes=[pltpu.VMEM((tm, half), x.dtype), pltpu.SemaphoreType.DMA(())],
        ),
        compiler_params=_cparams(("arbitrary",)),
        name="moe_gather",
    )(row_src, tile_groups, x)


def _segment_weights(te_ref, seg_ref, meta_ref, copies, cast):
    c = pl.program_id(0)
    r = pl.program_id(1)
    nt_used = meta_ref[0]
    nseg = meta_ref[1]
    used = r < nt_used
    seg = seg_ref[MOE_EXPERTS + r]
    first = jnp.logical_and(used, jnp.logical_or(r == 0, te_ref[r] != te_ref[jnp.maximum(r - 1, 0)]))
    g = c * nseg + seg
    slot = lax.rem(g, 2)

    @pl.when(first)
    def _():
        @pl.when(g == 0)
        def _():
            for cp in copies(te_ref[r], c, 0):
                cp.start()

        for cp in copies(te_ref[r], c, slot):
            cp.wait()
        wraps = seg + 1 >= nseg
        nxt_seg = jnp.where(wraps, 0, seg + 1)
        nxt_c = jnp.where(wraps, c + 1, c)

        @pl.when(nxt_c < pl.num_programs(0))
        def _():
            for cp in copies(seg_ref[nxt_seg], nxt_c, 1 - slot):
                cp.start()

        cast(slot)

    return used


def _moe_up_body(layer, tf, te_ref, seg_ref, meta_ref, x_ref, w_hbm, h_ref, wbuf, wgb_ref, wub_ref, sem):
    f = MOE_HIDDEN

    def copies(e, c, slot):
        col = pl.multiple_of(c * tf, tf)
        return [pltpu.make_async_copy(w_hbm.at[layer, e, :, pl.ds(half * f + col, tf)], wbuf.at[slot, half],
                                      sem.at[slot, half]) for half in range(2)]

    def cast(slot):
        wgb_ref[...] = wbuf[slot, 0].astype(BF16)
        wub_ref[...] = wbuf[slot, 1].astype(BF16)

    used = _segment_weights(te_ref, seg_ref, meta_ref, copies, cast)

    @pl.when(used)
    def _():
        x = x_ref[...]
        gate = jnp.dot(x, wgb_ref[...], preferred_element_type=F32)
        up = jnp.dot(x, wub_ref[...], preferred_element_type=F32)
        h_ref[...] = (jax.nn.silu(gate) * up).astype(h_ref.dtype)

    @pl.when(jnp.logical_not(used))
    def _():
        h_ref[...] = jnp.zeros_like(h_ref)


def _moe_down_body(layer, tn, te_ref, seg_ref, meta_ref, h_ref, w_hbm, y_ref, wbuf, wdb_ref, sem):
    def copies(e, c, slot):
        col = pl.multiple_of(c * tn, tn)
        return [pltpu.make_async_copy(w_hbm.at[layer, e, :, pl.ds(col, tn)], wbuf.at[slot], sem.at[slot])]

    def cast(slot):
        wdb_ref[...] = wbuf[slot].astype(BF16)

    used = _segment_weights(te_ref, seg_ref, meta_ref, copies, cast)

    @pl.when(used)
    def _():
        y_ref[...] = _pack_halves(jnp.dot(h_ref[...], wdb_ref[...], preferred_element_type=F32))

    @pl.when(jnp.logical_not(used))
    def _():
        y_ref[...] = jnp.zeros_like(y_ref)


def moe_experts(xs, tile_e, seg_info, meta, w_gate_up, w_down, layer, tm, tf=512, tnd=D_MODEL):
    p, d = xs.shape
    n_tiles = p // tm
    f = MOE_HIDDEN
    h = pl.pallas_call(
        functools.partial(_moe_up_body, layer, tf),
        out_shape=jax.ShapeDtypeStruct((p, f), BF16),
        grid_spec=pltpu.PrefetchScalarGridSpec(
            num_scalar_prefetch=3,
            grid=(f // tf, n_tiles),
            in_specs=[pl.BlockSpec((tm, d), lambda c, r, te, sg, mt: (r, 0)),
                      pl.BlockSpec(memory_space=pl.ANY)],
            out_specs=pl.BlockSpec((tm, tf), lambda c, r, te, sg, mt: (r, c)),
            scratch_shapes=[pltpu.VMEM((2, 2, d, tf), F32), pltpu.VMEM((d, tf), BF16), pltpu.VMEM((d, tf), BF16),
                            pltpu.SemaphoreType.DMA((2, 2))],
        ),
        compiler_params=_cparams(("arbitrary", "arbitrary")),
        name="moe_up",
    )(tile_e, seg_info, meta, xs, w_gate_up)
    assert tnd == d
    return pl.pallas_call(
        functools.partial(_moe_down_body, layer, tnd),
        out_shape=jax.ShapeDtypeStruct((p, d // 2), jnp.uint32),
        grid_spec=pltpu.PrefetchScalarGridSpec(
            num_scalar_prefetch=3,
            grid=(d // tnd, n_tiles),
            in_specs=[pl.BlockSpec((tm, f), lambda c, r, te, sg, mt: (r, 0)),
                      pl.BlockSpec(memory_space=pl.ANY)],
            out_specs=pl.BlockSpec((tm, tnd // 2), lambda c, r, te, sg, mt: (r, c)),
            scratch_shapes=[pltpu.VMEM((2, f, tnd), F32), pltpu.VMEM((f, tnd), BF16),
                            pltpu.SemaphoreType.DMA((2,))],
        ),
        compiler_params=_cparams(("arbitrary", "arbitrary")),
        name="moe_down",
    )(tile_e, seg_info, meta, h, w_down)


def _combine_body(tt, emit_norm, pos_ref, x_ref, g_ref, y_hbm, *refs):
    if emit_norm:
        gn_ref, o_ref, xn_ref, buf_ref, sem = refs
    else:
        o_ref, buf_ref, sem = refs
    i = pl.program_id(0)

    def copy(t, k):
        return pltpu.make_async_copy(y_hbm.at[pl.ds(pos_ref[(i * tt + t) * MOE_TOP_K + k], 1), :],
                                     buf_ref.at[k, pl.ds(t, 1), :], sem)

    def start(t, _):
        for k in range(MOE_TOP_K):
            copy(t, k).start(priority=k % DMA_PRIORITIES)
        return 0

    def wait(t, _):
        for k in range(MOE_TOP_K):
            copy(t, k).wait()
        return 0

    lax.fori_loop(0, tt, start, 0)
    lax.fori_loop(0, tt, wait, 0)
    g = g_ref[...]
    half = buf_ref.shape[2]
    lo0, hi0 = _unpack_halves(buf_ref[0])
    lo1, hi1 = _unpack_halves(buf_ref[1])
    x_lo = x_ref[:, :half] + (g[:, 0:1] * lo0 + g[:, 1:2] * lo1)
    x_hi = x_ref[:, half:] + (g[:, 0:1] * hi0 + g[:, 1:2] * hi1)
    o_ref[:, :half] = x_lo
    o_ref[:, half:] = x_hi
    if emit_norm:
        ssq = jnp.sum(x_lo * x_lo, axis=-1, keepdims=True) + jnp.sum(x_hi * x_hi, axis=-1, keepdims=True)
        inv = lax.rsqrt(ssq / (2 * half) + EPS)
        xn_ref[:, :half] = ((x_lo * inv) * gn_ref[:, :half]).astype(xn_ref.dtype)
        xn_ref[:, half:] = ((x_hi * inv) * gn_ref[:, half:]).astype(xn_ref.dtype)


def moe_combine(x, gates, pos, y_sorted, norm_g=None, tt=256):
    m, d = x.shape
    row = pl.BlockSpec((tt, d), lambda i, pos: (i, 0))
    in_specs = [row, pl.BlockSpec((tt, ROUTER_LANES), lambda i, pos: (i, 0)), pl.BlockSpec(memory_space=pl.ANY)]
    args = [pos, x, gates, y_sorted]
    out_shape = [jax.ShapeDtypeStruct((m, d), F32)]
    if norm_g is not None:
        in_specs.append(pl.BlockSpec((1, d), lambda i, pos: (0, 0)))
        args.append(norm_g.reshape(1, d).astype(F32))
        out_shape.append(jax.ShapeDtypeStruct((m, d), BF16))
    res = pl.pallas_call(
        functools.partial(_combine_body, tt, norm_g is not None),
        out_shape=out_shape,
        grid_spec=pltpu.PrefetchScalarGridSpec(
            num_scalar_prefetch=1,
            grid=(m // tt,),
            in_specs=in_specs,
            out_specs=[row] * len(out_shape),
            scratch_shapes=[pltpu.VMEM((MOE_TOP_K, tt, d // 2), y_sorted.dtype), pltpu.SemaphoreType.DMA(())],
        ),
        compiler_params=_cparams(("arbitrary",)),
        name="moe_combine",
    )(*args)
    return res if norm_g is not None else res[0]


def _moe_plan(ids, tm):
    t = ids.shape[0]
    n = t * MOE_TOP_K
    n_tiles = -(-(n + MOE_EXPERTS * (tm - 1)) // tm)
    i32 = jnp.int32
    flat_e = ids.reshape(n)
    order = jnp.argsort(flat_e, stable=True).astype(i32)
    inv = jnp.argsort(order).astype(i32)
    se = flat_e[order]
    experts = jnp.arange(MOE_EXPERTS, dtype=i32)
    starts = jnp.searchsorted(se, experts, side="left", method="compare_all").astype(i32)
    counts = jnp.searchsorted(se, experts, side="right", method="compare_all").astype(i32) - starts
    tile_count = (counts + tm - 1) // tm
    tile_end = jnp.cumsum(tile_count).astype(i32)
    tile_start = tile_end - tile_count
    pos = tile_start[flat_e] * tm + (inv - starts[flat_e])
    tiles = jnp.arange(n_tiles, dtype=i32)
    tile_e = jnp.minimum(jnp.searchsorted(tile_end, tiles, side="right", method="compare_all"),
                         MOE_EXPERTS - 1).astype(i32)
    tile_valid = jnp.clip(counts[tile_e] - (tiles - tile_start[tile_e]) * tm, 0, tm)
    tile_groups = (tile_valid + GATHER_GROUP - 1) // GATHER_GROUP
    within = jnp.arange(tm, dtype=i32)[None, :]
    slot = (starts[tile_e] + (tiles - tile_start[tile_e]) * tm)[:, None] + within
    src = order[jnp.clip(slot, 0, n - 1)] // MOE_TOP_K
    row_src = jnp.where(within < tile_valid[:, None], src, 0).reshape(n_tiles * tm)
    nonempty = tile_count > 0
    seg_of_expert = jnp.cumsum(nonempty.astype(i32)).astype(i32) - 1
    seg_expert = jnp.argsort(jnp.where(nonempty, 0, 1).astype(i32), stable=True).astype(i32)
    seg_info = jnp.concatenate([seg_expert, seg_of_expert[tile_e]])
    meta = jnp.stack([tile_end[-1], jnp.sum(nonempty.astype(i32))]).astype(i32)
    return row_src, pos.astype(i32), tile_e, seg_info, meta, tile_groups.astype(i32), n_tiles


def hier_moe(x, g, layer, w_group, b_group, w_expert, b_expert, w_gate_up, w_down, next_norm_g=None):
    xn, ids, gates = moe_router(x, g, w_group, b_group, w_expert, b_expert)
    row_src, pos, tile_e, seg_info, meta, tile_groups, n_tiles = _moe_plan(ids, MOE_TILE)
    xs = moe_gather(xn, row_src, tile_groups, n_tiles, MOE_TILE)
    y = moe_experts(xs, tile_e, seg_info, meta, w_gate_up, w_down, layer, MOE_TILE)
    return moe_combine(x, gates, pos, y, norm_g=next_norm_g)


def kernel(x_prompt, x_sample, mem_prompt, state_ssm_re, state_ssm_im, state_gla, cache_mem_k, cache_mem_v, norm_mixer, norm_ca, norm_moe, norm_final, ssm_lambda_re, ssm_lambda_im, ssm_log_dt, ssm_b_re, ssm_b_im, ssm_c_re, ssm_c_im, ssm_d, ssm_w_glu, gla_w_in, gla_w_gate_up, gla_b_gate, gla_norm, gla_w_out, ca_mem_norm, ca_w_q, ca_w_kv, ca_w_o, moe_w_group, moe_b_group, moe_w_expert, moe_b_expert, moe_w_gate_up, moe_w_down):
    bp, lp, d = x_prompt.shape
    bs, ls, _ = x_sample.shape
    np_ = bp * lp
    ns_ = bs * ls

    mem = mem_prompt.reshape(bp * N_MEM, d)
    pk, pv = [], []
    for i in range(DEPTH):
        mn = rmsnorm(mem, ca_mem_norm[i])
        kv = matmul(mn, ca_w_kv, i, out_dtype=F32)
        pk.append(kv[:, :d].reshape(bp, N_MEM, d))
        pv.append(kv[:, d:].reshape(bp, N_MEM, d))
    prompt_mem_k = jnp.stack(pk).reshape(DEPTH, bp, N_MEM, CA_HEADS, CA_HEAD_DIM)
    prompt_mem_v = jnp.stack(pv).reshape(DEPTH, bp, N_MEM, CA_HEADS, CA_HEAD_DIM)

    x = jnp.concatenate([x_prompt.reshape(np_, d), x_sample.reshape(ns_, d)], axis=0)
    zero_ssm = jnp.zeros((bp, SSM_GROUPS, SSM_STATE), F32)
    zero_gla = jnp.zeros((bp, GLA_HEADS, GLA_DK_HEAD, GLA_DV_HEAD), F32)
    p_re, p_im, s_re, s_im, p_gla, s_gla = [], [], [], [], [], []
    xn_mixer = None

    for i in range(DEPTH):
        j = i // 2
        if i % 2 == 0:
            xn_p = rmsnorm(x, norm_mixer[i], out_dtype=F32, row0=0, nrows=np_)
            xn_s = rmsnorm(x, norm_mixer[i], out_dtype=F32, row0=np_, nrows=ns_)
            prm = (ssm_lambda_re[j], ssm_lambda_im[j], ssm_log_dt[j], ssm_b_re[j], ssm_b_im[j],
                   ssm_c_re[j], ssm_c_im[j], ssm_d[j])
            zp, hr, hi = s5_layer(xn_p.reshape(bp, lp, d), zero_ssm, zero_ssm, prm)
            p_re.append(hr)
            p_im.append(hi)
            zs, hr, hi = s5_layer(xn_s.reshape(bs, ls, d), state_ssm_re[j], state_ssm_im[j], prm)
            s_re.append(hr)
            s_im.append(hi)
            z = jnp.concatenate([zp, zs], axis=0)
            x = matmul(z, ssm_w_glu, j, mode="glu_res", res=x, n_out=d, tn=256)
        else:
            xn = xn_mixer if xn_mixer is not None else rmsnorm(x, norm_mixer[i])
            w_in_t = jnp.swapaxes(gla_w_in, 1, 2)
            proj = matmul(xn, w_in_t, j, out_dtype=F32, n_out=GLA_MAIN, w_is_nk=True)
            la = gla_gate(xn, w_in_t[j, GLA_MAIN:, :], gla_w_gate_up[j], gla_b_gate[j])
            o, st = gla_recurrence(proj, la, 0, bp, lp, zero_gla, gla_norm[j], prev=xn)
            p_gla.append(st)
            o, st = gla_recurrence(proj, la, np_, bs, ls, state_gla[j], gla_norm[j], prev=o)
            s_gla.append(st)
            x = matmul(o, gla_w_out, j, mode="res", res=x)

        xn = rmsnorm(x, norm_ca[i])
        q = matmul(xn, ca_w_q, i, out_dtype=BF16)
        att = mem_attention(q, 0, bp, lp, pk[i], pv[i])
        att = mem_attention(att, np_, bs, ls, cache_mem_k, cache_mem_v, layer=i)
        x = matmul(att, ca_w_o, i, mode="res", res=x)

        gla_next = i + 1 < DEPTH and (i + 1) % 2 == 1
        res = hier_moe(x, norm_moe[i], i, moe_w_group[i], moe_b_group[i], moe_w_expert[i], moe_b_expert[i],
                       moe_w_gate_up, moe_w_down, next_norm_g=norm_mixer[i + 1] if gla_next else None)
        x, xn_mixer = res if gla_next else (res, None)

    y_prompt = rmsnorm(x, norm_final, out_dtype=F32, row0=0, nrows=np_)
    y_sample = rmsnorm(x, norm_final, out_dtype=F32, row0=np_, nrows=ns_)
    return (y_prompt.reshape(bp, lp, d), y_sample.reshape(bs, ls, d),
            jnp.stack(p_re), jnp.stack(p_im), jnp.stack(p_gla), prompt_mem_k, prompt_mem_v,
            jnp.stack(s_re), jnp.stack(s_im), jnp.stack(s_gla))
```

```python
import functools
import math

import jax
import jax.numpy as jnp
import numpy as np
from jax import lax
from jax.experimental import pallas as pl
from jax.experimental.pallas import tpu as pltpu

F32 = jnp.float32
BF16 = jnp.bfloat16

EPS = 1e-6
D_MODEL = 4096
DEPTH = 2

SSM_GROUP_CH = 16
SSM_GROUPS = D_MODEL // SSM_GROUP_CH
SSM_STATE = 64
SSM_SEQS = 32
SSM_SEQ_TILE = 32
SSM_SCAN_UNROLL = True
SSM_BLK_CH = 128
SSM_BLK_GROUPS = SSM_BLK_CH // SSM_GROUP_CH
SSM_NBLK = D_MODEL // SSM_BLK_CH
SSM_BLK_STATE = SSM_BLK_GROUPS * SSM_STATE

GLA_HEADS = 4
GLA_DK = D_MODEL // 2
GLA_DV = D_MODEL
GLA_DK_HEAD = GLA_DK // GLA_HEADS
GLA_DV_HEAD = GLA_DV // GLA_HEADS
GLA_GATE_RANK = 16
GLA_TAU = 16.0
GLA_MAIN = 2 * GLA_DK + 2 * GLA_DV
GLA_CHUNK = 64
GLA_HEADS_PER_STEP = 4

N_MEM = 256
CA_HEADS = 4
CA_HEAD_DIM = D_MODEL // CA_HEADS

MOE_GROUPS = 4
MOE_EPG = 8
MOE_EXPERTS = MOE_GROUPS * MOE_EPG
MOE_TOP_K = 2
MOE_HIDDEN = D_MODEL // 4
MOE_TILE = 256
GATHER_GROUP = 8
DMA_PRIORITIES = 2
ROUTER_LANES = 128

DENSE_TM = 1088
SUBLANES = 8
VMEM_LIMIT = 56 * 1024 * 1024


def _cparams(sem):
    return pltpu.CompilerParams(dimension_semantics=sem, vmem_limit_bytes=VMEM_LIMIT)


def _rms_body(n_add, emit_sum, x_ref, *refs):
    adds = refs[:n_add]
    g_ref = refs[n_add]
    outs = refs[n_add + 1:]
    x = x_ref[...].astype(F32)
    for a in adds:
        x = x + a[...].astype(F32)
    ms = jnp.mean(x * x, axis=-1, keepdims=True)
    xn = (x * lax.rsqrt(ms + EPS)) * g_ref[...]
    if emit_sum:
        outs[0][...] = x
        outs[1][...] = xn.astype(outs[1].dtype)
    else:
        outs[0][...] = xn.astype(outs[0].dtype)


def rmsnorm(x, g, adds=(), out_dtype=BF16, emit_sum=False, tm=256, row0=0, nrows=None):
    d = x.shape[1]
    m = x.shape[0] if nrows is None else nrows
    assert m % tm == 0 and row0 % tm == 0
    b0 = row0 // tm
    src = pl.BlockSpec((tm, d), lambda i: (b0 + i, 0))
    row = pl.BlockSpec((tm, d), lambda i: (i, 0))
    out_shape = [jax.ShapeDtypeStruct((m, d), out_dtype)]
    out_specs = [row]
    if emit_sum:
        out_shape.insert(0, jax.ShapeDtypeStruct((m, d), F32))
        out_specs.insert(0, row)
    res = pl.pallas_call(
        functools.partial(_rms_body, len(adds), emit_sum),
        out_shape=out_shape,
        grid=(m // tm,),
        in_specs=[src] * (1 + len(adds)) + [pl.BlockSpec((1, d), lambda i: (0, 0))],
        out_specs=out_specs,
        compiler_params=_cparams(("arbitrary",)),
        name="rmsnorm",
    )(x, *adds, g.reshape(1, d).astype(F32))
    return res if emit_sum else res[0]


def _mm_body(mode, w_is_nk, x_ref, *refs):
    if mode == "glu_res":
        w_ref, w2_ref, res_ref, o_ref, wb_ref, wb2_ref = refs
    elif mode == "res":
        w_ref, res_ref, o_ref, wb_ref = refs
    else:
        w_ref, o_ref, wb_ref = refs

    @pl.when(pl.program_id(1) == 0)
    def _():
        wb_ref[...] = w_ref[...].astype(BF16)
        if mode == "glu_res":
            wb2_ref[...] = w2_ref[...].astype(BF16)

    x = x_ref[...]
    if w_is_nk:
        acc = lax.dot_general(x, wb_ref[...], (((1,), (1,)), ((), ())), preferred_element_type=F32)
    else:
        acc = jnp.dot(x, wb_ref[...], preferred_element_type=F32)
    if mode == "glu_res":
        gate = jnp.dot(x, wb2_ref[...], preferred_element_type=F32)
        o_ref[...] = res_ref[...] + acc * jax.nn.sigmoid(gate)
    elif mode == "res":
        o_ref[...] = res_ref[...] + acc
    else:
        o_ref[...] = acc.astype(o_ref.dtype)


def matmul(x, w, layer, mode="plain", res=None, out_dtype=F32, n_out=None, tm=DENSE_TM, tn=512, w_is_nk=False):
    m, k = x.shape
    n = n_out if n_out is not None else w.shape[1 if w_is_nk else 2]
    if m % tm:
        tm = m
    assert m % tm == 0 and n % tn == 0 and not (w_is_nk and mode != "plain")
    nb = n // tn
    if w_is_nk:
        w_spec, w_tile = pl.BlockSpec((None, tn, k), lambda j, i: (layer, j, 0)), (tn, k)
    else:
        w_spec, w_tile = pl.BlockSpec((None, k, tn), lambda j, i: (layer, 0, j)), (k, tn)
    in_specs = [pl.BlockSpec((tm, k), lambda j, i: (i, 0)), w_spec]
    args = [x, w]
    scratch = [pltpu.VMEM(w_tile, BF16)]
    if mode == "glu_res":
        in_specs.append(pl.BlockSpec((None, k, tn), lambda j, i: (layer, 0, nb + j)))
        args.append(w)
        scratch.append(pltpu.VMEM((k, tn), BF16))
    if mode in ("res", "glu_res"):
        in_specs.append(pl.BlockSpec((tm, tn), lambda j, i: (i, j)))
        args.append(res)
        out_dtype = F32
    return pl.pallas_call(
        functools.partial(_mm_body, mode, w_is_nk),
        out_shape=jax.ShapeDtypeStruct((m, n), out_dtype),
        grid=(nb, m // tm),
        in_specs=in_specs,
        out_specs=pl.BlockSpec((tm, tn), lambda j, i: (i, j)),
        scratch_shapes=scratch,
        compiler_params=_cparams(("arbitrary", "arbitrary")),
        name="matmul_" + mode,
    )(*args)


def _s5_body(lt, emit_y, u_ref, h0_ref, ar_ref, ai_ref, bd_ref, *refs):
    if emit_y:
        cbd_ref, d_ref, z_ref, ht_ref, h_ref, bu_ref, hs_ref = refs
    else:
        ht_ref, h_ref, bu_ref = refs
    tb = pl.program_id(2)
    ns = SSM_BLK_STATE

    @pl.when(tb == 0)
    def _():
        h_ref[...] = h0_ref[...]

    st = SSM_SEQ_TILE
    groups = range(st // SUBLANES)
    u = u_ref[...].reshape(lt * st, SSM_BLK_CH)
    bu_ref[...] = jnp.dot(u.astype(BF16), bd_ref[...], preferred_element_type=F32)
    ar = jnp.broadcast_to(ar_ref[...], (SUBLANES, ns))
    ai = jnp.broadcast_to(ai_ref[...], (SUBLANES, ns))

    def step(t, carry):
        r0 = t * st
        out = []
        for g in groups:
            hr, hi = carry[g]
            rows = pl.ds(pl.multiple_of(r0 + g * SUBLANES, SUBLANES), SUBLANES)
            nr = ar * hr - ai * hi + bu_ref[rows, 0:ns]
            ni = ar * hi + ai * hr + bu_ref[rows, ns:2 * ns]
            if emit_y:
                hs_ref[rows, 0:ns] = nr
                hs_ref[rows, ns:2 * ns] = ni
            out.append((nr, ni))
        return tuple(out)

    grp = lambda g: slice(g * SUBLANES, (g + 1) * SUBLANES)
    init = tuple((h_ref[grp(g), 0:ns], h_ref[grp(g), ns:2 * ns]) for g in groups)
    final = lax.fori_loop(0, lt, step, init, unroll=SSM_SCAN_UNROLL)
    for g in groups:
        hr, hi = final[g]
        h_ref[grp(g), 0:ns] = hr
        h_ref[grp(g), ns:2 * ns] = hi
        ht_ref[grp(g), 0:ns] = hr
        ht_ref[grp(g), ns:2 * ns] = hi
    if emit_y:
        y = jnp.dot(hs_ref[...].astype(BF16), cbd_ref[...], preferred_element_type=F32)
        y = y + d_ref[...] * u
        z_ref[...] = jax.nn.gelu(y, approximate=True).reshape(lt, st, SSM_BLK_CH)


def _s5_scan(ut, h0, ar, ai, bd, cbd, dsk, emit_y):
    ltot = ut.shape[0]
    lt = min(ltot, 64)
    assert ltot % lt == 0
    ns2 = 2 * SSM_BLK_STATE
    st = SSM_SEQ_TILE
    nsg = SSM_SEQS // st
    in_specs = [
        pl.BlockSpec((lt, st, SSM_BLK_CH), lambda c, s, t: (t, s, c)),
        pl.BlockSpec((None, st, ns2), lambda c, s, t: (c, s, 0)),
        pl.BlockSpec((None, 1, SSM_BLK_STATE), lambda c, s, t: (c, 0, 0)),
        pl.BlockSpec((None, 1, SSM_BLK_STATE), lambda c, s, t: (c, 0, 0)),
        pl.BlockSpec((None, SSM_BLK_CH, ns2), lambda c, s, t: (c, 0, 0)),
    ]
    args = [ut, h0, ar, ai, bd]
    ht_shape = jax.ShapeDtypeStruct((SSM_NBLK, SSM_SEQS, ns2), F32)
    ht_spec = pl.BlockSpec((None, st, ns2), lambda c, s, t: (c, s, 0))
    scratch = [pltpu.VMEM((st, ns2), F32), pltpu.VMEM((lt * st, ns2), F32)]
    if emit_y:
        in_specs += [pl.BlockSpec((None, ns2, SSM_BLK_CH), lambda c, s, t: (c, 0, 0)),
                     pl.BlockSpec((None, 1, SSM_BLK_CH), lambda c, s, t: (c, 0, 0))]
        args += [cbd, dsk]
        out_shape = [jax.ShapeDtypeStruct(ut.shape, F32), ht_shape]
        out_specs = [pl.BlockSpec((lt, st, SSM_BLK_CH), lambda c, s, t: (t, s, c)), ht_spec]
        scratch.append(pltpu.VMEM((lt * st, ns2), F32))
    else:
        out_shape = [ht_shape]
        out_specs = [ht_spec]
    res = pl.pallas_call(
        functools.partial(_s5_body, lt, emit_y),
        out_shape=out_shape,
        grid=(SSM_NBLK, nsg, ltot // lt),
        in_specs=in_specs,
        out_specs=out_specs,
        scratch_shapes=scratch,
        compiler_params=_cparams(("arbitrary", "arbitrary", "arbitrary")),
        name="s5_scan_y" if emit_y else "s5_scan_state",
    )(*args)
    return (res[0], res[1]) if emit_y else (None, res[0])


def _s5_carry_body(nb, nseg, e_ref, h0_ref, ar_ref, ai_ref, hs_ref, he_ref):
    ns = SSM_BLK_STATE
    ar = ar_ref[...]
    ai = ai_ref[...]
    for b in range(nb):
        hr = h0_ref[b:b + 1, 0:ns]
        hi = h0_ref[b:b + 1, ns:2 * ns]
        for j in range(nseg):
            s = b * nseg + j
            hs_ref[s:s + 1, 0:ns] = hr
            hs_ref[s:s + 1, ns:2 * ns] = hi
            er = e_ref[s:s + 1, 0:ns]
            ei = e_ref[s:s + 1, ns:2 * ns]
            hr, hi = ar * hr - ai * hi + er, ar * hi + ai * hr + ei
        he_ref[b:b + 1, 0:ns] = hr
        he_ref[b:b + 1, ns:2 * ns] = hi


def _s5_carry(e0, h0, ar_seg, ai_seg, nb, nseg):
    ns2 = 2 * SSM_BLK_STATE
    return pl.pallas_call(
        functools.partial(_s5_carry_body, nb, nseg),
        out_shape=[jax.ShapeDtypeStruct((SSM_NBLK, SSM_SEQS, ns2), F32),
                   jax.ShapeDtypeStruct((SSM_NBLK, nb, ns2), F32)],
        grid=(SSM_NBLK,),
        in_specs=[pl.BlockSpec((None, SSM_SEQS, ns2), lambda c: (c, 0, 0)),
                  pl.BlockSpec((None, nb, ns2), lambda c: (c, 0, 0)),
                  pl.BlockSpec((None, 1, SSM_BLK_STATE), lambda c: (c, 0, 0)),
                  pl.BlockSpec((None, 1, SSM_BLK_STATE), lambda c: (c, 0, 0))],
        out_specs=[pl.BlockSpec((None, SSM_SEQS, ns2), lambda c: (c, 0, 0)),
                   pl.BlockSpec((None, nb, ns2), lambda c: (c, 0, 0))],
        compiler_params=_cparams(("arbitrary",)),
        name="s5_carry",
    )(e0, h0, ar_seg, ai_seg)


def _s5_params(lam_re, lam_im, log_dt, b_re, b_im, c_re, c_im, d_skip, seg_len):
    dt = jnp.exp(log_dt.astype(F32))[:, None]
    lr = lam_re.astype(F32)
    li = lam_im.astype(F32)
    mag = jnp.exp(lr * dt)
    ab_re = mag * jnp.cos(li * dt)
    ab_im = mag * jnp.sin(li * dt)
    nr = ab_re - 1.0
    ni = ab_im
    den = lr * lr + li * li
    f_re = (nr * lr + ni * li) / den
    f_im = (ni * lr - nr * li) / den
    br = b_re.astype(F32)
    bi = b_im.astype(F32)
    bb_re = f_re[..., None] * br - f_im[..., None] * bi
    bb_im = f_re[..., None] * bi + f_im[..., None] * br
    mag_s = jnp.exp(lr * dt * seg_len)
    as_re = mag_s * jnp.cos(li * dt * seg_len)
    as_im = mag_s * jnp.sin(li * dt * seg_len)
    eye = jnp.eye(SSM_BLK_GROUPS, dtype=F32)

    def in_proj(w):
        w = w.reshape(SSM_NBLK, SSM_BLK_GROUPS, SSM_STATE, SSM_GROUP_CH).transpose(0, 1, 3, 2)
        return jnp.einsum("bjcp,jk->bjckp", w, eye).reshape(SSM_NBLK, SSM_BLK_CH, SSM_BLK_STATE)

    def out_proj(w):
        w = w.reshape(SSM_NBLK, SSM_BLK_GROUPS, SSM_GROUP_CH, SSM_STATE).transpose(0, 1, 3, 2)
        return jnp.einsum("bjpc,jk->bjpkc", w, eye).reshape(SSM_NBLK, SSM_BLK_STATE, SSM_BLK_CH)

    bd = jnp.concatenate([in_proj(bb_re), in_proj(bb_im)], axis=-1).astype(BF16)
    cbd = jnp.concatenate([out_proj(c_re.astype(F32)), -out_proj(c_im.astype(F32))], axis=1).astype(BF16)
    blk = lambda v: v.reshape(SSM_NBLK, 1, SSM_BLK_STATE)
    return (blk(ab_re), blk(ab_im), blk(as_re), blk(as_im), bd, cbd,
            d_skip.astype(F32).reshape(SSM_NBLK, 1, SSM_BLK_CH))


def _state_to_blocks(h_re, h_im):
    nb = h_re.shape[0]
    f = lambda h: h.astype(F32).reshape(nb, SSM_NBLK, SSM_BLK_STATE).transpose(1, 0, 2)
    return jnp.concatenate([f(h_re), f(h_im)], axis=-1)


def _blocks_to_state(hb):
    nb = hb.shape[1]
    f = lambda h: h.transpose(1, 0, 2).reshape(nb, SSM_GROUPS, SSM_STATE)
    return f(hb[..., :SSM_BLK_STATE]), f(hb[..., SSM_BLK_STATE:])


def s5_layer(u, h0_re, h0_im, prm):
    nb, L, d = u.shape
    nseg = SSM_SEQS // nb
    seg = L // nseg
    lam_re, lam_im, log_dt, b_re, b_im, c_re, c_im, d_skip = prm
    ar, ai, as_re, as_im, bd, cbd, dsk = _s5_params(lam_re, lam_im, log_dt, b_re, b_im, c_re, c_im, d_skip, seg)
    ut = u.reshape(nb * nseg, seg, d).transpose(1, 0, 2)
    h0b = _state_to_blocks(h0_re, h0_im)
    if nseg == 1:
        zt, hT = _s5_scan(ut, h0b, ar, ai, bd, cbd, dsk, True)
    else:
        zero = jnp.zeros((SSM_NBLK, SSM_SEQS, 2 * SSM_BLK_STATE), F32)
        _, e0 = _s5_scan(ut, zero, ar, ai, bd, None, None, False)
        hstart, hT = _s5_carry(e0, h0b, as_re, as_im, nb, nseg)
        zt, _ = _s5_scan(ut, hstart, ar, ai, bd, cbd, dsk, True)
    z = zt.astype(BF16).transpose(1, 0, 2).reshape(nb * L, d)
    hr, hi = _blocks_to_state(hT)
    return z, hr, hi


def _gla_gate_body(x_ref, w1_ref, w2_ref, b_ref, o_ref):
    glr = lax.dot_general(x_ref[...], w1_ref[...].astype(BF16), (((1,), (1,)), ((), ())),
                          preferred_element_type=F32)
    logits = jnp.dot(glr.astype(BF16), w2_ref[...].astype(BF16), preferred_element_type=F32) + b_ref[...]
    o_ref[...] = jax.nn.log_sigmoid(logits) / GLA_TAU


def gla_gate(xn, w_glr_t, w_gate_up, b_gate, tm=512):
    m, k = xn.shape
    rp = 128
    w1 = jnp.pad(w_glr_t.astype(F32), ((0, rp - GLA_GATE_RANK), (0, 0)))
    w2 = jnp.pad(w_gate_up.astype(F32), ((0, rp - GLA_GATE_RANK), (0, 0)))
    return pl.pallas_call(
        _gla_gate_body,
        out_shape=jax.ShapeDtypeStruct((m, GLA_DK), F32),
        grid=(m // tm,),
        in_specs=[pl.BlockSpec((tm, k), lambda i: (i, 0)),
                  pl.BlockSpec((rp, k), lambda i: (0, 0)),
                  pl.BlockSpec((rp, GLA_DK), lambda i: (0, 0)),
                  pl.BlockSpec((1, GLA_DK), lambda i: (0, 0))],
        out_specs=pl.BlockSpec((tm, GLA_DK), lambda i: (i, 0)),
        compiler_params=_cparams(("arbitrary",)),
        name="gla_gate",
    )(xn, w1, w2, b_gate.reshape(1, GLA_DK).astype(F32))


def _gla_body(lc, q_ref, k_ref, v_ref, r_ref, la_ref, s0_ref, g_ref, o_ref, st_ref, s_ref):
    c = pl.program_id(2)

    @pl.when(c == 0)
    def _():
        s_ref[...] = s0_ref[...].astype(F32)

    row = lax.broadcasted_iota(jnp.int32, (lc, lc), 0)
    col = lax.broadcasted_iota(jnp.int32, (lc, lc), 1)
    tri = (col <= row).astype(BF16)
    ones = jnp.ones((lc, 128), BF16)
    tn = (((0,), (0,)), ((), ()))
    for hh in range(GLA_HEADS_PER_STEP):
        ks = slice(hh * GLA_DK_HEAD, (hh + 1) * GLA_DK_HEAD)
        vs = slice(hh * GLA_DV_HEAD, (hh + 1) * GLA_DV_HEAD)
        la = la_ref[:, ks]
        la_hi = la.astype(BF16)
        la_lo = (la - la_hi.astype(F32)).astype(BF16)
        cum = (jnp.dot(tri, la_hi, preferred_element_type=F32)
               + jnp.dot(tri, la_lo, preferred_element_type=F32))
        end = cum[lc - 1:lc, :]
        kd = (k_ref[:, ks].astype(F32) * jnp.exp(end - cum)).astype(BF16)
        end_col = (lax.dot_general(la_hi, ones, tn, preferred_element_type=F32)
                   + lax.dot_general(la_lo, ones, tn, preferred_element_type=F32))
        decay = jnp.exp(end_col[:, 0:1])
        s_ref[hh] = decay * s_ref[hh] + lax.dot_general(kd, v_ref[:, vs].astype(BF16), tn,
                                                        preferred_element_type=F32)
        qs = (q_ref[:, ks].astype(F32) * (GLA_DK_HEAD ** -0.5)).astype(BF16)
        o = jnp.dot(qs, s_ref[hh].astype(BF16), preferred_element_type=F32)
        o = o * lax.rsqrt(jnp.mean(o * o, axis=-1, keepdims=True) + EPS)
        o = o * g_ref[:, vs]
        o_ref[:, vs] = (o * jax.nn.silu(r_ref[:, vs].astype(F32))).astype(o_ref.dtype)

    @pl.when(c == pl.num_programs(2) - 1)
    def _():
        st_ref[...] = s_ref[...]


def _drop_alias_ref(body, n_in, *refs):
    return body(*refs[:n_in], *refs[n_in + 1:])


def gla_recurrence(proj, la, row0, nb, L, s0, norm_g, prev=None):
    ntok = proj.shape[0]
    lc = min(GLA_CHUNK, L)
    nc = L // lc
    base = row0 // lc
    hp = GLA_HEADS_PER_STEP
    wk, wv = hp * GLA_DK_HEAD, hp * GLA_DV_HEAD
    kq = GLA_DK // wk
    kv = 2 * GLA_DK // wv
    kr = kv + GLA_DV // wv
    rows = lambda b, h, c: base + b * nc + c
    state_spec = pl.BlockSpec((None, hp, GLA_DK_HEAD, GLA_DV_HEAD), lambda b, h, c: (b, h, 0, 0))
    in_specs = [pl.BlockSpec((lc, wk), lambda b, h, c: (rows(b, h, c), h)),
                pl.BlockSpec((lc, wk), lambda b, h, c: (rows(b, h, c), kq + h)),
                pl.BlockSpec((lc, wv), lambda b, h, c: (rows(b, h, c), kv + h)),
                pl.BlockSpec((lc, wv), lambda b, h, c: (rows(b, h, c), kr + h)),
                pl.BlockSpec((lc, wk), lambda b, h, c: (rows(b, h, c), h)),
                state_spec,
                pl.BlockSpec((1, wv), lambda b, h, c: (0, h))]
    args = [proj, proj, proj, proj, la, s0, norm_g.reshape(1, GLA_DV).astype(F32)]
    body = functools.partial(_gla_body, lc)
    aliases = {}
    if prev is not None:
        in_specs.append(pl.BlockSpec(memory_space=pl.ANY))
        args.append(prev)
        aliases = {len(args) - 1: 0}
        body = functools.partial(_drop_alias_ref, body, len(args) - 1)
    o, st = pl.pallas_call(
        body,
        out_shape=[jax.ShapeDtypeStruct((ntok, GLA_DV), BF16),
                   jax.ShapeDtypeStruct((nb, GLA_HEADS, GLA_DK_HEAD, GLA_DV_HEAD), F32)],
        grid=(nb, GLA_HEADS // hp, nc),
        in_specs=in_specs,
        out_specs=[pl.BlockSpec((lc, wv), lambda b, h, c: (rows(b, h, c), h)), state_spec],
        scratch_shapes=[pltpu.VMEM((hp, GLA_DK_HEAD, GLA_DV_HEAD), F32)],
        input_output_aliases=aliases,
        compiler_params=_cparams(("arbitrary", "arbitrary", "arbitrary")),
        name="gla_recurrence",
    )(*args)
    return o, st


def _attn_body(per_head, q_ref, k_ref, v_ref, o_ref):
    nt = (((1,), (1,)), ((), ()))
    scale = CA_HEAD_DIM ** -0.5
    heads = lambda ref: [ref[:, h * CA_HEAD_DIM:(h + 1) * CA_HEAD_DIM] for h in range(CA_HEADS)]

    def softmax(s):
        p = jnp.exp(s - jnp.max(s, axis=-1, keepdims=True))
        return p / jnp.sum(p, axis=-1, keepdims=True)

    if per_head:
        tl = q_ref.shape[0]
        k2 = k_ref[...].reshape(N_MEM * CA_HEADS, CA_HEAD_DIM).astype(BF16)
        v2 = v_ref[...].reshape(N_MEM * CA_HEADS, CA_HEAD_DIM).astype(BF16)
        q4 = jnp.concatenate(heads(q_ref), axis=0)
        s = lax.dot_general(q4, k2, nt, preferred_element_type=F32) * scale
        q_head = lax.broadcasted_iota(jnp.int32, s.shape, 0) // tl
        m_head = lax.broadcasted_iota(jnp.int32, s.shape, 1) % CA_HEADS
        p = softmax(jnp.where(q_head == m_head, s, -jnp.inf))
        o4 = jnp.dot(p.astype(BF16), v2, preferred_element_type=F32).astype(o_ref.dtype)
        for h in range(CA_HEADS):
            o_ref[:, h * CA_HEAD_DIM:(h + 1) * CA_HEAD_DIM] = o4[h * tl:(h + 1) * tl]
    else:
        for h, (qh, kh, vh) in enumerate(zip(heads(q_ref), heads(k_ref), heads(v_ref))):
            s = lax.dot_general(qh, kh.astype(BF16), nt, preferred_element_type=F32) * scale
            o = jnp.dot(softmax(s).astype(BF16), vh.astype(BF16), preferred_element_type=F32)
            o_ref[:, h * CA_HEAD_DIM:(h + 1) * CA_HEAD_DIM] = o.astype(o_ref.dtype)


def mem_attention(q, row0, nb, L, mk, mv, layer=None):
    ntok = q.shape[0]
    tl = min(L, 512)
    nl = L // tl
    base = row0 // tl
    rows = lambda b, i: (base + b * nl + i, 0)
    if layer is None:
        mem_spec = pl.BlockSpec((None, N_MEM, D_MODEL), lambda b, i: (b, 0, 0))
    else:
        mem_spec = pl.BlockSpec((None, None, N_MEM, CA_HEADS, CA_HEAD_DIM), lambda b, i: (layer, b, 0, 0, 0))
    return pl.pallas_call(
        functools.partial(_attn_body, layer is not None),
        out_shape=jax.ShapeDtypeStruct((ntok, D_MODEL), BF16),
        grid=(nb, nl),
        in_specs=[pl.BlockSpec((tl, D_MODEL), rows), mem_spec, mem_spec],
        out_specs=pl.BlockSpec((tl, D_MODEL), rows),
        input_output_aliases={0: 0},
        compiler_params=_cparams(("arbitrary", "arbitrary")),
        name="mem_attention",
    )(q, mk, mv)


def _pack_halves(x):
    n = x.shape[1] // 2
    lo = lax.bitcast_convert_type(x[:, :n].astype(BF16).astype(F32), jnp.uint32)
    hi = lax.bitcast_convert_type(x[:, n:].astype(BF16).astype(F32), jnp.uint32)
    return lax.shift_right_logical(lo, jnp.uint32(16)) | hi


def _unpack_halves(w):
    lo = lax.bitcast_convert_type(lax.shift_left(w, jnp.uint32(16)), F32)
    hi = lax.bitcast_convert_type(w & jnp.uint32(0xFFFF0000), F32)
    return lo, hi


def _router_body(x_ref, g_ref, w_ref, b_ref, xn_ref, ids_ref, gates_ref):
    x = x_ref[...]
    ms = jnp.mean(x * x, axis=-1, keepdims=True)
    xn = (x * lax.rsqrt(ms + EPS)) * g_ref[...]
    xn_ref[...] = _pack_halves(xn)
    logits = jnp.dot(xn.astype(BF16), w_ref[...].astype(BF16), preferred_element_type=F32) + b_ref[...]
    tm = logits.shape[0]
    col = lax.broadcasted_iota(jnp.int32, (tm, ROUTER_LANES), 1).astype(F32)
    neg = jnp.float32(-jnp.inf)
    first = lambda mask: jnp.min(jnp.where(mask, col, float(ROUTER_LANES)), axis=-1, keepdims=True)
    gl = jnp.where(col < MOE_GROUPS, logits, neg)
    gmax = jnp.max(gl, axis=-1, keepdims=True)
    gidx = first(gl == gmax)
    g_w = 1.0 / jnp.sum(jnp.exp(gl - gmax), axis=-1, keepdims=True)
    lo = MOE_GROUPS + gidx * MOE_EPG
    el = jnp.where((col >= lo) & (col < lo + MOE_EPG), logits, neg)
    m1 = jnp.max(el, axis=-1, keepdims=True)
    i1 = first(el == m1)
    z = jnp.sum(jnp.exp(el - m1), axis=-1, keepdims=True)
    p1 = 1.0 / z
    el2 = jnp.where(col == i1, neg, el)
    m2 = jnp.max(el2, axis=-1, keepdims=True)
    i2 = first(el2 == m2)
    p2 = jnp.exp(m2 - m1) / z
    tot = p1 + p2
    ids = jnp.where(col == 0, i1 - MOE_GROUPS, jnp.where(col == 1, i2 - MOE_GROUPS, 0.0))
    ids_ref[...] = ids.astype(jnp.int32)
    gates_ref[...] = jnp.where(col == 0, g_w * p1 / tot, jnp.where(col == 1, g_w * p2 / tot, 0.0))


def moe_router(x, g, w_group, b_group, w_expert, b_expert, tm=256):
    m, d = x.shape
    npad = ROUTER_LANES - MOE_GROUPS - MOE_EXPERTS
    w = jnp.pad(jnp.concatenate([w_group, w_expert], axis=1).astype(F32), ((0, 0), (0, npad)))
    b = jnp.pad(jnp.concatenate([b_group, b_expert]).astype(F32), (0, npad)).reshape(1, ROUTER_LANES)
    row = lambda n: pl.BlockSpec((tm, n), lambda i: (i, 0))
    xn, ids, gates = pl.pallas_call(
        _router_body,
        out_shape=[jax.ShapeDtypeStruct((m, d // 2), jnp.uint32),
                   jax.ShapeDtypeStruct((m, ROUTER_LANES), jnp.int32),
                   jax.ShapeDtypeStruct((m, ROUTER_LANES), F32)],
        grid=(m // tm,),
        in_specs=[row(d), pl.BlockSpec((1, d), lambda i: (0, 0)),
                  pl.BlockSpec((d, ROUTER_LANES), lambda i: (0, 0)),
                  pl.BlockSpec((1, ROUTER_LANES), lambda i: (0, 0))],
        out_specs=[row(d // 2), row(ROUTER_LANES), row(ROUTER_LANES)],
        compiler_params=_cparams(("arbitrary",)),
        name="moe_router",
    )(x, g.reshape(1, d).astype(F32), w, b)
    return xn, ids[:, :MOE_TOP_K], gates


def _gather_body(tm, src_ref, cnt_ref, x_hbm, o_ref, buf_ref, sem):
    i = pl.program_id(0)
    slot = lax.rem(i, 2)

    def copy(tile, r, s):
        return pltpu.make_async_copy(x_hbm.at[pl.ds(src_ref[tile * tm + r], 1), :],
                                     buf_ref.at[s, pl.ds(r, 1), :], sem.at[s])

    def request(tile, s):
        n_groups = cnt_ref[tile]

        @pl.when(n_groups < tm // GATHER_GROUP)
        def _():
            buf_ref[s] = jnp.zeros(buf_ref.shape[1:], buf_ref.dtype)

        def start(g, _):
            for u in range(GATHER_GROUP):
                copy(tile, g * GATHER_GROUP + u, s).start(priority=u % DMA_PRIORITIES)
            return 0

        lax.fori_loop(0, n_groups, start, 0)

    @pl.when(i == 0)
    def _():
        request(0, 0)

    @pl.when(i + 1 < pl.num_programs(0))
    def _():
        request(i + 1, 1 - slot)

    def wait(g, _):
        for u in range(GATHER_GROUP):
            copy(i, g * GATHER_GROUP + u, slot).wait()
        return 0

    lax.fori_loop(0, cnt_ref[i], wait, 0)
    lo, hi = _unpack_halves(buf_ref[slot])
    half = buf_ref.shape[2]
    o_ref[:, :half] = lo.astype(o_ref.dtype)
    o_ref[:, half:] = hi.astype(o_ref.dtype)


def moe_gather(x, row_src, tile_groups, n_tiles, tm):
    half = x.shape[1]
    d = 2 * half
    return pl.pallas_call(
        functools.partial(_gather_body, tm),
        out_shape=jax.ShapeDtypeStruct((n_tiles * tm, d), BF16),
        grid_spec=pltpu.PrefetchScalarGridSpec(
            num_scalar_prefetch=2,
            grid=(n_tiles,),
            in_specs=[pl.BlockSpec(memory_space=pl.ANY)],
            out_specs=pl.BlockSpec((tm, d), lambda i, src, cnt: (i, 0)),
            scratch_shapes=[pltpu.VMEM((2, tm, half), x.dtype), pltpu.SemaphoreType.DMA((2,))],
        ),
        compiler_params=_cparams(("arbitrary",)),
        name="moe_gather",
    )(row_src, tile_groups, x)


def _segment_weights(te_ref, seg_ref, meta_ref, copies, cast):
    c = pl.program_id(0)
    r = pl.program_id(1)
    nt_used = meta_ref[0]
    nseg = meta_ref[1]
    used = r < nt_used
    seg = seg_ref[MOE_EXPERTS + r]
    first = jnp.logical_and(used, jnp.logical_or(r == 0, te_ref[r] != te_ref[jnp.maximum(r - 1, 0)]))
    g = c * nseg + seg
    slot = lax.rem(g, 2)

    @pl.when(first)
    def _():
        @pl.when(g == 0)
        def _():
            for cp in copies(te_ref[r], c, 0):
                cp.start()

        for cp in copies(te_ref[r], c, slot):
            cp.wait()
        wraps = seg + 1 >= nseg
        nxt_seg = jnp.where(wraps, 0, seg + 1)
        nxt_c = jnp.where(wraps, c + 1, c)

        @pl.when(nxt_c < pl.num_programs(0))
        def _():
            for cp in copies(seg_ref[nxt_seg], nxt_c, 1 - slot):
                cp.start()

        cast(slot)

    return used


def _moe_up_body(layer, tf, te_ref, seg_ref, meta_ref, x_ref, w_hbm, h_ref, wbuf, wgb_ref, wub_ref, sem):
    f = MOE_HIDDEN

    def copies(e, c, slot):
        col = pl.multiple_of(c * tf, tf)
        return [pltpu.make_async_copy(w_hbm.at[layer, e, :, pl.ds(half * f + col, tf)], wbuf.at[slot, half],
                                      sem.at[slot, half]) for half in range(2)]

    def cast(slot):
        wgb_ref[...] = wbuf[slot, 0].astype(BF16)
        wub_ref[...] = wbuf[slot, 1].astype(BF16)

    used = _segment_weights(te_ref, seg_ref, meta_ref, copies, cast)

    @pl.when(used)
    def _():
        x = x_ref[...]
        gate = jnp.dot(x, wgb_ref[...], preferred_element_type=F32)
        up = jnp.dot(x, wub_ref[...], preferred_element_type=F32)
        h_ref[...] = (jax.nn.silu(gate) * up).astype(h_ref.dtype)

    @pl.when(jnp.logical_not(used))
    def _():
        h_ref[...] = jnp.zeros_like(h_ref)


def _moe_down_body(layer, tn, te_ref, seg_ref, meta_ref, h_ref, w_hbm, y_ref, wbuf, wdb_ref, sem):
    def copies(e, c, slot):
        col = pl.multiple_of(c * tn, tn)
        return [pltpu.make_async_copy(w_hbm.at[layer, e, :, pl.ds(col, tn)], wbuf.at[slot], sem.at[slot])]

    def cast(slot):
        wdb_ref[...] = wbuf[slot].astype(BF16)

    used = _segment_weights(te_ref, seg_ref, meta_ref, copies, cast)

    @pl.when(used)
    def _():
        y_ref[...] = _pack_halves(jnp.dot(h_ref[...], wdb_ref[...], preferred_element_type=F32))

    @pl.when(jnp.logical_not(used))
    def _():
        y_ref[...] = jnp.zeros_like(y_ref)


def moe_experts(xs, tile_e, seg_info, meta, w_gate_up, w_down, layer, tm, tf=512, tnd=D_MODEL):
    p, d = xs.shape
    n_tiles = p // tm
    f = MOE_HIDDEN
    h = pl.pallas_call(
        functools.partial(_moe_up_body, layer, tf),
        out_shape=jax.ShapeDtypeStruct((p, f), BF16),
        grid_spec=pltpu.PrefetchScalarGridSpec(
            num_scalar_prefetch=3,
            grid=(f // tf, n_tiles),
            in_specs=[pl.BlockSpec((tm, d), lambda c, r, te, sg, mt: (r, 0)),
                      pl.BlockSpec(memory_space=pl.ANY)],
            out_specs=pl.BlockSpec((tm, tf), lambda c, r, te, sg, mt: (r, c)),
            scratch_shapes=[pltpu.VMEM((2, 2, d, tf), F32), pltpu.VMEM((d, tf), BF16), pltpu.VMEM((d, tf), BF16),
                            pltpu.SemaphoreType.DMA((2, 2))],
        ),
        compiler_params=_cparams(("arbitrary", "arbitrary")),
        name="moe_up",
    )(tile_e, seg_info, meta, xs, w_gate_up)
    assert tnd == d
    return pl.pallas_call(
        functools.partial(_moe_down_body, layer, tnd),
        out_shape=jax.ShapeDtypeStruct((p, d // 2), jnp.uint32),
        grid_spec=pltpu.PrefetchScalarGridSpec(
            num_scalar_prefetch=3,
            grid=(d // tnd, n_tiles),
            in_specs=[pl.BlockSpec((tm, f), lambda c, r, te, sg, mt: (r, 0)),
                      pl.BlockSpec(memory_space=pl.ANY)],
            out_specs=pl.BlockSpec((tm, tnd // 2), lambda c, r, te, sg, mt: (r, c)),
            scratch_shapes=[pltpu.VMEM((2, f, tnd), F32), pltpu.VMEM((f, tnd), BF16),
                            pltpu.SemaphoreType.DMA((2,))],
        ),
        compiler_params=_cparams(("arbitrary", "arbitrary")),
        name="moe_down",
    )(tile_e, seg_info, meta, h, w_down)


def _combine_body(tt, emit_norm, pos_ref, x_ref, g_ref, y_hbm, *refs):
    if emit_norm:
        gn_ref, o_ref, xn_ref, buf_ref, sem = refs
    else:
        o_ref, buf_ref, sem = refs
    i = pl.program_id(0)
    slot = lax.rem(i, 2)

    def copy(tile, t, k, s):
        return pltpu.make_async_copy(y_hbm.at[pl.ds(pos_ref[(tile * tt + t) * MOE_TOP_K + k], 1), :],
                                     buf_ref.at[s, k, pl.ds(t, 1), :], sem.at[s])

    def request(tile, s):
        def start(t, _):
            for k in range(MOE_TOP_K):
                copy(tile, t, k, s).start(priority=k % DMA_PRIORITIES)
            return 0

        lax.fori_loop(0, tt, start, 0)

    @pl.when(i == 0)
    def _():
        request(0, 0)

    @pl.when(i + 1 < pl.num_programs(0))
    def _():
        request(i + 1, 1 - slot)

    def wait(t, _):
        for k in range(MOE_TOP_K):
            copy(i, t, k, slot).wait()
        return 0

    lax.fori_loop(0, tt, wait, 0)
    g = g_ref[...]
    half = buf_ref.shape[3]
    lo0, hi0 = _unpack_halves(buf_ref[slot, 0])
    lo1, hi1 = _unpack_halves(buf_ref[slot, 1])
    x_lo = x_ref[:, :half] + (g[:, 0:1] * lo0 + g[:, 1:2] * lo1)
    x_hi = x_ref[:, half:] + (g[:, 0:1] * hi0 + g[:, 1:2] * hi1)
    o_ref[:, :half] = x_lo
    o_ref[:, half:] = x_hi
    if emit_norm:
        ssq = jnp.sum(x_lo * x_lo, axis=-1, keepdims=True) + jnp.sum(x_hi * x_hi, axis=-1, keepdims=True)
        inv = lax.rsqrt(ssq / (2 * half) + EPS)
        xn_ref[:, :half] = ((x_lo * inv) * gn_ref[:, :half]).astype(xn_ref.dtype)
        xn_ref[:, half:] = ((x_hi * inv) * gn_ref[:, half:]).astype(xn_ref.dtype)


def moe_combine(x, gates, pos, y_sorted, norm_g=None, tt=256):
    m, d = x.shape
    row = pl.BlockSpec((tt, d), lambda i, pos: (i, 0))
    in_specs = [row, pl.BlockSpec((tt, ROUTER_LANES), lambda i, pos: (i, 0)), pl.BlockSpec(memory_space=pl.ANY)]
    args = [pos, x, gates, y_sorted]
    out_shape = [jax.ShapeDtypeStruct((m, d), F32)]
    if norm_g is not None:
        in_specs.append(pl.BlockSpec((1, d), lambda i, pos: (0, 0)))
        args.append(norm_g.reshape(1, d).astype(F32))
        out_shape.append(jax.ShapeDtypeStruct((m, d), BF16))
    res = pl.pallas_call(
        functools.partial(_combine_body, tt, norm_g is not None),
        out_shape=out_shape,
        grid_spec=pltpu.PrefetchScalarGridSpec(
            num_scalar_prefetch=1,
            grid=(m // tt,),
            in_specs=in_specs,
            out_specs=[row] * len(out_shape),
            scratch_shapes=[pltpu.VMEM((2, MOE_TOP_K, tt, d // 2), y_sorted.dtype),
                            pltpu.SemaphoreType.DMA((2,))],
        ),
        compiler_params=_cparams(("arbitrary",)),
        name="moe_combine",
    )(*args)
    return res if norm_g is not None else res[0]


def _moe_plan(ids, tm):
    t = ids.shape[0]
    n = t * MOE_TOP_K
    n_tiles = -(-(n + MOE_EXPERTS * (tm - 1)) // tm)
    i32 = jnp.int32
    flat_e = ids.reshape(n)
    order = jnp.argsort(flat_e, stable=True).astype(i32)
    inv = jnp.argsort(order).astype(i32)
    se = flat_e[order]
    experts = jnp.arange(MOE_EXPERTS, dtype=i32)
    starts = jnp.searchsorted(se, experts, side="left", method="compare_all").astype(i32)
    counts = jnp.searchsorted(se, experts, side="right", method="compare_all").astype(i32) - starts
    tile_count = (counts + tm - 1) // tm
    tile_end = jnp.cumsum(tile_count).astype(i32)
    tile_start = tile_end - tile_count
    pos = tile_start[flat_e] * tm + (inv - starts[flat_e])
    tiles = jnp.arange(n_tiles, dtype=i32)
    tile_e = jnp.minimum(jnp.searchsorted(tile_end, tiles, side="right", method="compare_all"),
                         MOE_EXPERTS - 1).astype(i32)
    tile_valid = jnp.clip(counts[tile_e] - (tiles - tile_start[tile_e]) * tm, 0, tm)
    tile_groups = (tile_valid + GATHER_GROUP - 1) // GATHER_GROUP
    within = jnp.arange(tm, dtype=i32)[None, :]
    slot = (starts[tile_e] + (tiles - tile_start[tile_e]) * tm)[:, None] + within
    src = order[jnp.clip(slot, 0, n - 1)] // MOE_TOP_K
    row_src = jnp.where(within < tile_valid[:, None], src, 0).reshape(n_tiles * tm)
    nonempty = tile_count > 0
    seg_of_expert = jnp.cumsum(nonempty.astype(i32)).astype(i32) - 1
    seg_expert = jnp.argsort(jnp.where(nonempty, 0, 1).astype(i32), stable=True).astype(i32)
    seg_info = jnp.concatenate([seg_expert, seg_of_expert[tile_e]])
    meta = jnp.stack([tile_end[-1], jnp.sum(nonempty.astype(i32))]).astype(i32)
    return row_src, pos.astype(i32), tile_e, seg_info, meta, tile_groups.astype(i32), n_tiles


def hier_moe(x, g, layer, w_group, b_group, w_expert, b_expert, w_gate_up, w_down, next_norm_g=None):
    xn, ids, gates = moe_router(x, g, w_group, b_group, w_expert, b_expert)
    row_src, pos, tile_e, seg_info, meta, tile_groups, n_tiles = _moe_plan(ids, MOE_TILE)
    xs = moe_gather(xn, row_src, tile_groups, n_tiles, MOE_TILE)
    y = moe_experts(xs, tile_e, seg_info, meta, w_gate_up, w_down, layer, MOE_TILE)
    return moe_combine(x, gates, pos, y, norm_g=next_norm_g)


def kernel(x_prompt, x_sample, mem_prompt, state_ssm_re, state_ssm_im, state_gla, cache_mem_k, cache_mem_v, norm_mixer, norm_ca, norm_moe, norm_final, ssm_lambda_re, ssm_lambda_im, ssm_log_dt, ssm_b_re, ssm_b_im, ssm_c_re, ssm_c_im, ssm_d, ssm_w_glu, gla_w_in, gla_w_gate_up, gla_b_gate, gla_norm, gla_w_out, ca_mem_norm, ca_w_q, ca_w_kv, ca_w_o, moe_w_group, moe_b_group, moe_w_expert, moe_b_expert, moe_w_gate_up, moe_w_down):
    bp, lp, d = x_prompt.shape
    bs, ls, _ = x_sample.shape
    np_ = bp * lp
    ns_ = bs * ls

    mem = mem_prompt.reshape(bp * N_MEM, d)
    pk, pv = [], []
    for i in range(DEPTH):
        mn = rmsnorm(mem, ca_mem_norm[i])
        kv = matmul(mn, ca_w_kv, i, out_dtype=F32)
        pk.append(kv[:, :d].reshape(bp, N_MEM, d))
        pv.append(kv[:, d:].reshape(bp, N_MEM, d))
    prompt_mem_k = jnp.stack(pk).reshape(DEPTH, bp, N_MEM, CA_HEADS, CA_HEAD_DIM)
    prompt_mem_v = jnp.stack(pv).reshape(DEPTH, bp, N_MEM, CA_HEADS, CA_HEAD_DIM)

    x = jnp.concatenate([x_prompt.reshape(np_, d), x_sample.reshape(ns_, d)], axis=0)
    zero_ssm = jnp.zeros((bp, SSM_GROUPS, SSM_STATE), F32)
    zero_gla = jnp.zeros((bp, GLA_HEADS, GLA_DK_HEAD, GLA_DV_HEAD), F32)
    p_re, p_im, s_re, s_im, p_gla, s_gla = [], [], [], [], [], []
    xn_mixer = None

    for i in range(DEPTH):
        j = i // 2
        if i % 2 == 0:
            xn_p = rmsnorm(x, norm_mixer[i], out_dtype=F32, row0=0, nrows=np_)
            xn_s = rmsnorm(x, norm_mixer[i], out_dtype=F32, row0=np_, nrows=ns_)
            prm = (ssm_lambda_re[j], ssm_lambda_im[j], ssm_log_dt[j], ssm_b_re[j], ssm_b_im[j],
                   ssm_c_re[j], ssm_c_im[j], ssm_d[j])
            zp, hr, hi = s5_layer(xn_p.reshape(bp, lp, d), zero_ssm, zero_ssm, prm)
            p_re.append(hr)
            p_im.append(hi)
            zs, hr, hi = s5_layer(xn_s.reshape(bs, ls, d), state_ssm_re[j], state_ssm_im[j], prm)
            s_re.append(hr)
            s_im.append(hi)
            z = jnp.concatenate([zp, zs], axis=0)
            x = matmul(z, ssm_w_glu, j, mode="glu_res", res=x, n_out=d, tn=256)
        else:
            xn = xn_mixer if xn_mixer is not None else rmsnorm(x, norm_mixer[i])
            w_in_t = jnp.swapaxes(gla_w_in, 1, 2)
            proj = matmul(xn, w_in_t, j, out_dtype=F32, n_out=GLA_MAIN, w_is_nk=True)
            la = gla_gate(xn, w_in_t[j, GLA_MAIN:, :], gla_w_gate_up[j], gla_b_gate[j])
            o, st = gla_recurrence(proj, la, 0, bp, lp, zero_gla, gla_norm[j], prev=xn)
            p_gla.append(st)
            o, st = gla_recurrence(proj, la, np_, bs, ls, state_gla[j], gla_norm[j], prev=o)
            s_gla.append(st)
            x = matmul(o, gla_w_out, j, mode="res", res=x)

        xn = rmsnorm(x, norm_ca[i])
        q = matmul(xn, ca_w_q, i, out_dtype=BF16)
        att = mem_attention(q, 0, bp, lp, pk[i], pv[i])
        att = mem_attention(att, np_, bs, ls, cache_mem_k, cache_mem_v, layer=i)
        x = matmul(att, ca_w_o, i, mode="res", res=x)

        gla_next = i + 1 < DEPTH and (i + 1) % 2 == 1
        res = hier_moe(x, norm_moe[i], i, moe_w_group[i], moe_b_group[i], moe_w_expert[i], moe_b_expert[i],
                       moe_w_gate_up, moe_w_down, next_norm_g=norm_mixer[i + 1] if gla_next else None)
        x, xn_mixer = res if gla_next else (res, None)

    y_prompt = rmsnorm(x, norm_final, out_dtype=F32, row0=0, nrows=np_)
    y_sample = rmsnorm(x, norm_final, out_dtype=F32, row0=np_, nrows=ns_)
    return (y_prompt.reshape(bp, lp, d), y_sample.reshape(bs, ls, d),
            jnp.stack(p_re), jnp.stack(p_im), jnp.stack(p_gla), prompt_mem_k, prompt_mem_v,
            jnp.stack(s_re), jnp.stack(s_im), jnp.stack(s_gla))
```
